```python
import math
import jax, jax.numpy as jnp
from jax import lax
import numpy as np

D_MODEL = 2048
BATCH = 16
SEQ = 2048
DEPTH = 1

HEAD_DIM = 64
D_A = D_MODEL // 2
D_B = D_MODEL - D_A
D_MIX = D_A + D_B
H_A = D_A // HEAD_DIM
KV_A = 2
G_A = H_A // KV_A
D_KV_A = KV_A * HEAD_DIM
H_B = D_B // HEAD_DIM
QKV_DIM = D_A + 2 * D_KV_A + 3 * D_B
WINDOW = 128
Q_BLOCK = 128
NUM_BUCKETS = 32
MAX_DISTANCE = 128
N_GROUPS = 4
EXPERTS_PER_GROUP = 8
N_EXPERTS = N_GROUPS * EXPERTS_PER_GROUP
TOP_K = 2
D_EXPERT = D_MODEL // 4
MOE_BLOCK = 128
ALPHA = (2.0 * DEPTH) ** 0.25
BETA = (8.0 * DEPTH) ** -0.25
ATTN_SCALE = 1.0 / math.sqrt(HEAD_DIM)
EPS = 1e-5
NEG_INF = -1e30

kernel_name = "hybrid_swa_sink_stickbreak_hmoe_deepnorm"


def layer_norm(x, g, b):
    xf = x.astype(jnp.float32)
    mu = jnp.mean(xf, axis=-1, keepdims=True)
    var = jnp.mean(jnp.square(xf - mu), axis=-1, keepdims=True)
    y = (xf - mu) * lax.rsqrt(var + EPS) * g.astype(jnp.float32) + b.astype(jnp.float32)
    return y.astype(x.dtype)


def rms_norm(x, g):
    xf = x.astype(jnp.float32)
    y = xf * lax.rsqrt(jnp.mean(jnp.square(xf), axis=-1, keepdims=True) + EPS)
    return (y * g.astype(jnp.float32)).astype(x.dtype)


def t5_bucket(dist):
    n = jnp.maximum(dist, 0)
    max_exact = NUM_BUCKETS // 2
    ratio = jnp.maximum(n, max_exact).astype(jnp.float32) / max_exact
    large = max_exact + (jnp.log(ratio) / math.log(MAX_DISTANCE / max_exact)
                         * (NUM_BUCKETS - max_exact)).astype(jnp.int32)
    large = jnp.minimum(large, NUM_BUCKETS - 1)
    return jnp.where(n < max_exact, n, large)


def sliding_sink_attention(q, k, v, sinks, rel_bias):
    B, S, _ = q.shape
    nb = S // WINDOW
    q = q.reshape(B, nb, WINDOW, KV_A, G_A, HEAD_DIM)
    k = k.reshape(B, nb, WINDOW, KV_A, HEAD_DIM)
    v = v.reshape(B, nb, WINDOW, KV_A, HEAD_DIM)
    pad = ((0, 0), (1, 0), (0, 0), (0, 0), (0, 0))
    kk = jnp.concatenate([jnp.pad(k, pad)[:, :-1], k], axis=2)
    vv = jnp.concatenate([jnp.pad(v, pad)[:, :-1], v], axis=2)
    scores = jnp.einsum('bnqhgd,bnkhd->bnhgqk', q, kk).astype(jnp.float32) * ATTN_SCALE
    qi = jnp.arange(WINDOW)[:, None]
    kj = jnp.arange(2 * WINDOW)[None, :]
    dist = qi + WINDOW - kj
    band = (dist >= 0) & (dist < WINDOW)
    not_first = (jnp.arange(nb)[:, None, None] > 0) | (kj >= WINDOW)[None]
    mask = band[None] & not_first
    bias = rel_bias[t5_bucket(dist)].astype(jnp.float32)
    bias = jnp.transpose(bias, (2, 0, 1)).reshape(KV_A, G_A, WINDOW, 2 * WINDOW)
    logits = jnp.where(mask[None, :, None, None], scores + bias, NEG_INF)
    sink = sinks.astype(jnp.float32).reshape(KV_A, G_A)[:, :, None, None]
    m = jnp.maximum(jnp.max(logits, axis=-1, keepdims=True), sink)
    p = jnp.exp(logits - m)
    probs = p / (jnp.sum(p, axis=-1, keepdims=True) + jnp.exp(sink - m))
    o = jnp.einsum('bnhgqk,bnkhd->bnqhgd', probs.astype(v.dtype), vv)
    return o.reshape(B, S, D_A)


def stick_breaking_attention(q, k, v):
    B, S, _ = q.shape
    q = q.reshape(B, S, H_B, HEAD_DIM).transpose(0, 2, 1, 3)
    k = k.reshape(B, S, H_B, HEAD_DIM).transpose(0, 2, 1, 3)
    v = v.reshape(B, S, H_B, HEAD_DIM).transpose(0, 2, 1, 3)
    outs = []
    for i in range(S // Q_BLOCK):
        end = (i + 1) * Q_BLOCK
        q_i = q[:, :, i * Q_BLOCK:end]
        k_i = k[:, :, :end]
        v_i = v[:, :, :end]
        t = i * Q_BLOCK + jnp.arange(Q_BLOCK)
        s = jnp.arange(end)
        mask = s[None, :] < t[:, None]
        z = jnp.einsum('bhqd,bhkd->bhqk', q_i, k_i).astype(jnp.float32) * ATTN_SCALE
        u = jnp.where(mask, jax.nn.log_sigmoid(-z), 0.0)
        suffix = lax.cumsum(u, axis=3, reverse=True) - u
        a = jnp.where(mask, jnp.exp(jax.nn.log_sigmoid(z) + suffix), 0.0)
        outs.append(jnp.einsum('bhqk,bhkd->bhqd', a.astype(v.dtype), v_i))
    o = jnp.concatenate(outs, axis=2)
    return o.transpose(0, 2, 1, 3).reshape(B, S, D_B)


def hybrid_mixer(h, w_in, w_out, sinks, rel_bias, norm_a, norm_b):
    qkv = h @ w_in
    cuts = np.cumsum([D_A, D_KV_A, D_KV_A, D_B, D_B]).tolist()
    q_a, k_a, v_a, q_b, k_b, v_b = jnp.split(qkv, cuts, axis=-1)
    o_a = rms_norm(sliding_sink_attention(q_a, k_a, v_a, sinks, rel_bias), norm_a)
    o_b = rms_norm(stick_breaking_attention(q_b, k_b, v_b), norm_b)
    return jnp.concatenate([o_a, o_b], axis=-1) @ w_out


def hierarchical_moe(h, w_grp, b_grp, w_rtr, b_rtr, w_gate, w_up, w_down):
    B, S, D = h.shape
    T = B * S
    xt = h.reshape(T, D)
    glog = (xt @ w_grp + b_grp).astype(jnp.float32)
    g_sel = jnp.argmax(glog, axis=-1)
    p_g = jnp.take_along_axis(jax.nn.softmax(glog, axis=-1), g_sel[:, None], axis=1)[:, 0]
    elog = (xt @ w_rtr + b_rtr).astype(jnp.float32).reshape(T, N_GROUPS, EXPERTS_PER_GROUP)
    elog = jnp.take_along_axis(elog, g_sel[:, None, None], axis=1)[:, 0]
    top_v, top_i = lax.top_k(elog, TOP_K)
    gates = p_g[:, None] * jax.nn.softmax(top_v, axis=-1)
    expert = g_sel[:, None].astype(jnp.int32) * EXPERTS_PER_GROUP + top_i.astype(jnp.int32)
    M = T * TOP_K
    e_flat = expert.reshape(M)
    tok_flat = jnp.arange(M, dtype=jnp.int32) // TOP_K
    order = jnp.argsort(e_flat)
    e_sorted = e_flat[order]
    tok_sorted = tok_flat[order]
    gate_sorted = gates.reshape(M)[order]
    counts = jax.ops.segment_sum(jnp.ones((M,), jnp.int32), e_flat, num_segments=N_EXPERTS)
    start = jnp.cumsum(counts) - counts
    padded = (counts + MOE_BLOCK - 1) // MOE_BLOCK * MOE_BLOCK
    pend = jnp.cumsum(padded)
    pstart = pend - padded
    dest = pstart[e_sorted] + (jnp.arange(M, dtype=jnp.int32) - start[e_sorted])
    m_pad = M + N_EXPERTS * MOE_BLOCK
    n_blk = m_pad // MOE_BLOCK
    slot_tok = jnp.full((m_pad,), T, jnp.int32).at[dest].set(tok_sorted)
    blk_expert = jnp.clip(jnp.searchsorted(pend, jnp.arange(n_blk) * MOE_BLOCK, side='right'),
                          0, N_EXPERTS - 1)
    x_pad = jnp.concatenate([xt, jnp.zeros((1, D), xt.dtype)], axis=0)
    xs = x_pad[slot_tok].reshape(n_blk, MOE_BLOCK, D)

    def expert_block(args):
        xb, e = args
        return (jax.nn.silu(xb @ w_gate[e]) * (xb @ w_up[e])) @ w_down[e]

    ys = lax.map(expert_block, (xs, blk_expert)).reshape(m_pad, D)
    y_assign = ys[dest] * gate_sorted[:, None].astype(ys.dtype)
    y = jax.ops.segment_sum(y_assign, tok_sorted, num_segments=T)
    return y.reshape(B, S, D)


def setup_inputs(seed: int = 0) -> dict:
    key = jax.random.key(seed)
    ks = jax.random.split(key, 24)
    f32 = jnp.float32
    D = D_MODEL
    nrm = lambda k, shape, s: jax.random.normal(k, shape, f32) * s
    x = nrm(ks[0], (BATCH, SEQ, D), 1.0)
    c = nrm(ks[1], (BATCH, D), 1.0)
    col_scale = jnp.concatenate([
        jnp.ones((D_A + D_KV_A,), f32), jnp.full((D_KV_A,), BETA, f32),
        jnp.ones((2 * D_B,), f32), jnp.full((D_B,), BETA, f32)])
    w_in = nrm(ks[2], (DEPTH, D, QKV_DIM), D ** -0.5) * col_scale
    w_out = nrm(ks[3], (DEPTH, D_MIX, D), BETA * D_MIX ** -0.5)
    sinks = nrm(ks[4], (DEPTH, H_A), 0.5)
    rel_bias = nrm(ks[5], (NUM_BUCKETS, H_A), 0.2)
    norm_a = 1.0 + nrm(ks[6], (DEPTH, D_A), 0.02)
    norm_b = 1.0 + nrm(ks[7], (DEPTH, D_B), 0.02)
    w_ada = nrm(ks[8], (DEPTH, D, 6 * D), 0.1 * D ** -0.5)
    b_ada = nrm(ks[9], (DEPTH, 6 * D), 0.02)
    ln1_g = 1.0 + nrm(ks[10], (DEPTH, D), 0.02)
    ln1_b = nrm(ks[11], (DEPTH, D), 0.02)
    ln2_g = 1.0 + nrm(ks[12], (DEPTH, D), 0.02)
    ln2_b = nrm(ks[13], (DEPTH, D), 0.02)
    w_grp = nrm(ks[14], (DEPTH, D, N_GROUPS), D ** -0.5)
    b_grp = nrm(ks[15], (DEPTH, N_GROUPS), 0.01)
    w_rtr = nrm(ks[16], (DEPTH, D, N_EXPERTS), D ** -0.5)
    b_rtr = nrm(ks[17], (DEPTH, N_EXPERTS), 0.01)
    w_gate = nrm(ks[18], (DEPTH, N_EXPERTS, D, D_EXPERT), BETA * D ** -0.5)
    w_up = nrm(ks[19], (DEPTH, N_EXPERTS, D, D_EXPERT), BETA * D ** -0.5)
    w_down = nrm(ks[20], (DEPTH, N_EXPERTS, D_EXPERT, D), BETA * D_EXPERT ** -0.5)
    return {"x": x, "c": c, "w_in": w_in, "w_out": w_out, "sinks": sinks,
            "rel_bias": rel_bias, "norm_a": norm_a, "norm_b": norm_b,
            "w_ada": w_ada, "b_ada": b_ada, "ln1_g": ln1_g, "ln1_b": ln1_b,
            "ln2_g": ln2_g, "ln2_b": ln2_b, "w_grp": w_grp, "b_grp": b_grp,
            "w_rtr": w_rtr, "b_rtr": b_rtr, "w_gate": w_gate, "w_up": w_up,
            "w_down": w_down}


def reference(x, c, w_in, w_out, sinks, rel_bias, norm_a, norm_b, w_ada, b_ada,
              ln1_g, ln1_b, ln2_g, ln2_b, w_grp, b_grp, w_rtr, b_rtr,
              w_gate, w_up, w_down):
    c_act = jax.nn.silu(c)
    for l in range(DEPTH):
        mod = c_act @ w_ada[l] + b_ada[l]
        shift1, scale1, gate1, shift2, scale2, gate2 = [
            m[:, None, :] for m in jnp.split(mod, 6, axis=-1)]
        h = x * (1.0 + scale1) + shift1
        mix = hybrid_mixer(h, w_in[l], w_out[l], sinks[l], rel_bias, norm_a[l], norm_b[l])
        x = layer_norm(ALPHA * x + (1.0 + gate1) * mix, ln1_g[l], ln1_b[l])
        h = x * (1.0 + scale2) + shift2
        ffn = hierarchical_moe(h, w_grp[l], b_grp[l], w_rtr[l], b_rtr[l],
                               w_gate[l], w_up[l], w_down[l])
        x = layer_norm(ALPHA * x + (1.0 + gate2) * ffn, ln2_g[l], ln2_b[l])
    return x
```

```python
import functools
import math

import jax
import jax.numpy as jnp
import numpy as np
from jax import lax
from jax.experimental import pallas as pl
from jax.experimental.pallas import tpu as pltpu

F32 = jnp.float32
BF16 = jnp.bfloat16
I32 = jnp.int32

HEAD_DIM = 64
KV_A = 2
NUM_BUCKETS = 32
MAX_DISTANCE = 128
WINDOW = 128
Q_BLOCK = 128
N_GROUPS = 4
EXPERTS_PER_GROUP = 8
N_EXPERTS = N_GROUPS * EXPERTS_PER_GROUP
DEPTH = 1
ALPHA = (2.0 * DEPTH) ** 0.25
ATTN_SCALE = 1.0 / math.sqrt(HEAD_DIM)
EPS = 1e-5
NEG_INF = -1e30

LANES = 128
MOE_TM = 256
VMEM_LIMIT = 48 * 1024 * 1024


def _params(sem, vmem=VMEM_LIMIT):
    return pltpu.CompilerParams(dimension_semantics=sem, vmem_limit_bytes=vmem)


def _adaln_kernel(c_ref, w_ref, b_ref, o_ref):
    c = c_ref[...]
    ca = (c * jax.nn.sigmoid(c)).astype(BF16)
    o_ref[...] = jnp.dot(ca, w_ref[0].astype(BF16), preferred_element_type=F32) + b_ref[...]


def _adaln(c, w_ada, b_ada, tn=1024):
    bsz, d = c.shape
    n = w_ada.shape[-1]
    return pl.pallas_call(
        _adaln_kernel,
        grid=(n // tn,),
        in_specs=[pl.BlockSpec((bsz, d), lambda j: (0, 0)),
                  pl.BlockSpec((1, d, tn), lambda j: (0, 0, j)),
                  pl.BlockSpec((1, tn), lambda j: (0, j))],
        out_specs=pl.BlockSpec((bsz, tn), lambda j: (0, j)),
        out_shape=jax.ShapeDtypeStruct((bsz, n), F32),
        compiler_params=_params(("arbitrary",)),
        name="adaln",
    )(c, w_ada, b_ada)


def _qkv_kernel(x_ref, mod_ref, w_ref, o_ref):
    shift = mod_ref[0, 0:1, :]
    scale = mod_ref[0, 1:2, :]
    h = (x_ref[0] * (1.0 + scale) + shift).astype(BF16)
    o_ref[0] = jnp.dot(h, w_ref[...], preferred_element_type=F32).astype(BF16)


def _qkv(x, mod3, w_in_bf, tm=512, nj=2):
    bsz, s, d = x.shape
    n = w_in_bf.shape[1]
    tn = n // nj
    tm = min(tm, s)
    return pl.pallas_call(
        _qkv_kernel,
        grid=(nj, bsz, s // tm),
        in_specs=[pl.BlockSpec((1, tm, d), lambda j, b, i: (b, i, 0)),
                  pl.BlockSpec((1, 6, d), lambda j, b, i: (b, 0, 0)),
                  pl.BlockSpec((d, tn), lambda j, b, i: (0, j))],
        out_specs=pl.BlockSpec((1, tm, tn), lambda j, b, i: (b, i, j)),
        out_shape=jax.ShapeDtypeStruct((bsz, s, n), BF16),
        compiler_params=_params(("arbitrary", "arbitrary", "arbitrary")),
        name="qkv",
    )(x, mod3, w_in_bf)


def _bucket_map():
    qi = np.arange(WINDOW)[:, None]
    kj = np.arange(2 * WINDOW)[None, :]
    dist = qi + WINDOW - kj
    n = np.maximum(dist, 0)
    max_exact = NUM_BUCKETS // 2
    ratio = np.maximum(n, max_exact).astype(np.float32) / np.float32(max_exact)
    large = max_exact + (np.log(ratio) / np.float32(math.log(MAX_DISTANCE / max_exact))
                         * np.float32(NUM_BUCKETS - max_exact)).astype(np.int32)
    large = np.minimum(large, NUM_BUCKETS - 1)
    bucket = np.where(n < max_exact, n, large)
    band = (dist >= 0) & (dist < WINDOW)
    return np.where(band, bucket, -1).astype(np.int32)


def _swa_bias_kernel(rb_ref, bucket_ref, o_ref):
    h = pl.program_id(0)
    bucket = bucket_ref[...]
    acc = jnp.full(bucket.shape, NEG_INF, F32)
    for b in range(NUM_BUCKETS):
        acc = jnp.where(bucket == b, rb_ref[b, h], acc)
    o_ref[0] = acc


def _swa_bias(rel_bias):
    nh = rel_bias.shape[1]
    bucket = jnp.asarray(_bucket_map())
    return pl.pallas_call(
        _swa_bias_kernel,
        grid=(nh,),
        in_specs=[pl.BlockSpec(memory_space=pltpu.SMEM),
                  pl.BlockSpec((WINDOW, 2 * WINDOW), lambda h: (0, 0))],
        out_specs=pl.BlockSpec((1, WINDOW, 2 * WINDOW), lambda h: (h, 0, 0)),
        out_shape=jax.ShapeDtypeStruct((nh, WINDOW, 2 * WINDOW), F32),
        compiler_params=_params(("arbitrary",)),
        name="swa_bias",
    )(rel_bias, bucket)


def _swa_kernel(sink_ref, q_ref, kvc_ref, kvp_ref, bias_ref, o_ref, *, n_heads):
    i = pl.program_id(1)
    group = n_heads // KV_A
    kv = jnp.concatenate([kvp_ref[0], kvc_ref[0]], axis=0)
    lane = lax.broadcasted_iota(I32, (2 * WINDOW, LANES), 1)
    low = lane < HEAD_DIM

    def placed(pair):
        swapped = pltpu.roll(pair.astype(F32), HEAD_DIM, axis=1).astype(BF16)
        zero = jnp.zeros_like(pair)
        return [[jnp.where(low, pair, zero), jnp.where(low, zero, swapped)],
                [jnp.where(low, swapped, zero), jnp.where(low, zero, pair)]]

    kz = placed(kv[:, 0:LANES])
    vz = placed(kv[:, LANES:2 * LANES])
    col = lax.broadcasted_iota(I32, (1, 2 * WINDOW), 1)
    first_pen = jnp.where(jnp.logical_and(i == 0, col < WINDOW), NEG_INF, 0.0).astype(F32)

    for p in range(n_heads // 2):
        qp = q_ref[0, :, p * LANES:(p + 1) * LANES]
        acc = jnp.zeros((WINDOW, LANES), F32)
        for pos in range(2):
            h = 2 * p + pos
            g = h // group
            s = lax.dot_general(qp, kz[g][pos], (((1,), (1,)), ((), ())),
                                preferred_element_type=F32)
            logits = s + bias_ref[h] + first_pen
            sink = sink_ref[h]
            m = jnp.maximum(jnp.max(logits, axis=-1, keepdims=True), sink)
            e = jnp.exp(logits - m)
            den = jnp.sum(e, axis=-1, keepdims=True) + jnp.exp(sink - m)
            o = jnp.dot(e.astype(BF16), vz[g][pos], preferred_element_type=F32)
            acc = acc + o * (1.0 / den)
        o_ref[0, :, p * LANES:(p + 1) * LANES] = acc.astype(BF16)


def _swa(qkv, sinks, bias, d_a):
    bsz, s, _ = qkv.shape
    n_heads = d_a // HEAD_DIM
    kv_blk = d_a // (2 * LANES)
    return pl.pallas_call(
        functools.partial(_swa_kernel, n_heads=n_heads),
        grid=(bsz, s // WINDOW),
        in_specs=[pl.BlockSpec(memory_space=pltpu.SMEM),
                  pl.BlockSpec((1, WINDOW, d_a), lambda b, i: (b, i, 0)),
                  pl.BlockSpec((1, WINDOW, 2 * LANES), lambda b, i: (b, i, kv_blk)),
                  pl.BlockSpec((1, WINDOW, 2 * LANES),
                               lambda b, i: (b, jnp.maximum(i - 1, 0), kv_blk)),
                  pl.BlockSpec((n_heads, WINDOW, 2 * WINDOW), lambda b, i: (0, 0, 0))],
        out_specs=pl.BlockSpec((1, WINDOW, d_a), lambda b, i: (b, i, 0)),
        out_shape=jax.ShapeDtypeStruct((bsz, s, d_a), BF16),
        compiler_params=_params(("arbitrary", "arbitrary")),
        name="swa",
    )(sinks, qkv, qkv, qkv, bias)


def _suffix_matrix():
    j = np.arange(Q_BLOCK)[:, None]
    s = np.arange(Q_BLOCK)[None, :]
    return np.concatenate([(j > s), np.ones((Q_BLOCK, Q_BLOCK), bool)], axis=1).astype(np.float32)


def _sb_kernel(q_ref, k_ref, v_ref, lt_ref, o_ref, acc_ref, carry_ref):
    i = pl.program_id(2)
    lane = lax.broadcasted_iota(I32, (Q_BLOCK, LANES), 1)
    low = lane < HEAD_DIM
    q = q_ref[0]
    zero = jnp.zeros_like(q)
    qz = jnp.concatenate([jnp.where(low, q, zero), jnp.where(low, zero, q)], axis=0)
    row = lax.broadcasted_iota(I32, (2 * Q_BLOCK, Q_BLOCK), 0)
    col = lax.broadcasted_iota(I32, (2 * Q_BLOCK, Q_BLOCK), 1)
    strict = col < jnp.where(row >= Q_BLOCK, row - Q_BLOCK, row)

    acc_ref[...] = jnp.zeros_like(acc_ref)
    carry_ref[...] = jnp.zeros_like(carry_ref)

    def block(j, diagonal):
        start = pl.multiple_of(j * Q_BLOCK, Q_BLOCK)
        kj = k_ref[0, pl.ds(start, Q_BLOCK), :]
        vj = v_ref[0, pl.ds(start, Q_BLOCK), :]
        z = lax.dot_general(qz, kj, (((1,), (1,)), ((), ())), preferred_element_type=F32)
        l = jnp.log(1.0 + jnp.exp(-jnp.abs(z)))
        sp = jnp.maximum(z, 0.0) + l
        if diagonal:
            sp = jnp.where(strict, sp, 0.0)
        cs = jnp.dot(sp.astype(BF16), lt_ref[...], preferred_element_type=F32)
        arg = jnp.minimum(z, 0.0) - l - cs[:, :Q_BLOCK] - carry_ref[...]
        a = jnp.exp(arg)
        if diagonal:
            a = jnp.where(strict, a, 0.0)
        a = a.astype(BF16)
        a2 = jnp.concatenate([a[:Q_BLOCK], a[Q_BLOCK:]], axis=1)
        vzero = jnp.zeros_like(vj)
        vz = jnp.concatenate([jnp.where(low, vj, vzero), jnp.where(low, vzero, vj)], axis=0)
        acc_ref[...] += jnp.dot(a2, vz, preferred_element_type=F32)
        carry_ref[...] += cs[:, Q_BLOCK:]

    block(i, True)

    def body(jj, c):
        block(i - 1 - jj, False)
        return c

    lax.fori_loop(0, i, body, 0)
    o_ref[0] = acc_ref[...].astype(BF16)


def _sb(qkv, d_a, d_b):
    bsz, s, _ = qkv.shape
    pairs = d_b // LANES
    q0 = (d_a + 2 * KV_A * HEAD_DIM) // LANES
    k0 = q0 + pairs
    v0 = k0 + pairs
    lt = jnp.asarray(_suffix_matrix(), BF16)
    return pl.pallas_call(
        _sb_kernel,
        grid=(bsz, pairs, s // Q_BLOCK),
        in_specs=[pl.BlockSpec((1, Q_BLOCK, LANES), lambda b, p, i: (b, i, q0 + p)),
                  pl.BlockSpec((1, s, LANES), lambda b, p, i: (b, 0, k0 + p)),
                  pl.BlockSpec((1, s, LANES), lambda b, p, i: (b, 0, v0 + p)),
                  pl.BlockSpec((Q_BLOCK, 2 * Q_BLOCK), lambda b, p, i: (0, 0))],
        out_specs=pl.BlockSpec((1, Q_BLOCK, LANES), lambda b, p, i: (b, i, p)),
        out_shape=jax.ShapeDtypeStruct((bsz, s, d_b), BF16),
        scratch_shapes=[pltpu.VMEM((Q_BLOCK, LANES), F32),
                        pltpu.VMEM((2 * Q_BLOCK, LANES), F32)],
        compiler_params=_params(("arbitrary", "arbitrary", "arbitrary")),
        name="sb",
    )(qkv, qkv, qkv, lt)


def _layer_norm(y, g, b):
    mu = jnp.mean(y, axis=-1, keepdims=True)
    yc = y - mu
    var = jnp.mean(yc * yc, axis=-1, keepdims=True)
    return yc * lax.rsqrt(var + EPS) * g + b


def _rms(o, g):
    return o * lax.rsqrt(jnp.mean(o * o, axis=-1, keepdims=True) + EPS) * g


def _mix_ln1_kernel(oa_ref, ob_ref, x_ref, mod_ref, na_ref, nb_ref, wo_ref, g_ref, b_ref,
                    wrh_ref, wrl_ref, br_ref, x1_ref, h2_ref, lg_ref, *, d_a):
    ra = _rms(oa_ref[0].astype(F32), na_ref[...]).astype(BF16)
    rb = _rms(ob_ref[0].astype(F32), nb_ref[...]).astype(BF16)
    mix = (jnp.dot(ra, wo_ref[:d_a, :], preferred_element_type=F32)
           + jnp.dot(rb, wo_ref[d_a:, :], preferred_element_type=F32))
    gate1 = mod_ref[0, 2:3, :]
    shift2 = mod_ref[0, 3:4, :]
    scale2 = mod_ref[0, 4:5, :]
    x1 = _layer_norm(ALPHA * x_ref[0] + (1.0 + gate1) * mix, g_ref[...], b_ref[...])
    x1_ref[0] = x1
    h2 = x1 * (1.0 + scale2) + shift2
    h2_ref[0] = h2
    hi = h2.astype(BF16)
    lo = (h2 - hi.astype(F32)).astype(BF16)
    wh = wrh_ref[...]
    lg_ref[0] = (jnp.dot(hi, wh, preferred_element_type=F32)
                 + jnp.dot(lo, wh, preferred_element_type=F32)
                 + jnp.dot(hi, wrl_ref[...], preferred_element_type=F32)
                 + br_ref[...])


def _mix_ln1(o_a, o_b, x, mod3, norm_a, norm_b, w_out_bf, ln_g, ln_b, wr_hi, wr_lo, b_r, tm=256):
    bsz, s, d = x.shape
    d_a = o_a.shape[-1]
    d_b = o_b.shape[-1]
    tm = min(tm, s)
    row = lambda b, i: (b, i, 0)
    const2 = lambda b, i: (0, 0)
    return pl.pallas_call(
        functools.partial(_mix_ln1_kernel, d_a=d_a),
        grid=(bsz, s // tm),
        in_specs=[pl.BlockSpec((1, tm, d_a), row),
                  pl.BlockSpec((1, tm, d_b), row),
                  pl.BlockSpec((1, tm, d), row),
                  pl.BlockSpec((1, 6, d), lambda b, i: (b, 0, 0)),
                  pl.BlockSpec((1, d_a), const2),
                  pl.BlockSpec((1, d_b), const2),
                  pl.BlockSpec((d_a + d_b, d), const2),
                  pl.BlockSpec((1, d), const2),
                  pl.BlockSpec((1, d), const2),
                  pl.BlockSpec((d, LANES), const2),
                  pl.BlockSpec((d, LANES), const2),
                  pl.BlockSpec((1, LANES), const2)],
        out_specs=[pl.BlockSpec((1, tm, d), row),
                   pl.BlockSpec((1, tm, d), row),
                   pl.BlockSpec((1, tm, LANES), row)],
        out_shape=[jax.ShapeDtypeStruct((bsz, s, d), F32),
                   jax.ShapeDtypeStruct((bsz, s, d), F32),
                   jax.ShapeDtypeStruct((bsz, s, LANES), F32)],
        compiler_params=_params(("arbitrary", "arbitrary")),
        name="mix_ln1",
    )(o_a, o_b, x, mod3, norm_a, norm_b, w_out_bf, ln_g, ln_b, wr_hi, wr_lo, b_r)


def _route_kernel(lg_ref, tri_ref, sel_ref, gate_ref, cnt_ref, base_ref):
    step = pl.program_id(0)

    @pl.when(step == 0)
    def _():
        base_ref[...] = jnp.zeros_like(base_ref)

    lg = lg_ref[...]
    tm = lg.shape[0]
    lane = lax.broadcasted_iota(I32, (tm, LANES), 1)
    big = jnp.int32(2 * LANES)
    glog = jnp.where(lane < N_GROUPS, lg, -jnp.inf)
    gmax = jnp.max(glog, axis=-1, keepdims=True)
    g_sel = jnp.min(jnp.where(glog == gmax, lane, big), axis=-1, keepdims=True)
    p_g = 1.0 / jnp.sum(jnp.exp(glog - gmax), axis=-1, keepdims=True)
    lo = N_GROUPS + g_sel * EXPERTS_PER_GROUP
    in_grp = jnp.logical_and(lane >= lo, lane < lo + EXPERTS_PER_GROUP)
    el = jnp.where(in_grp, lg, -jnp.inf)
    v1 = jnp.max(el, axis=-1, keepdims=True)
    i1 = jnp.min(jnp.where(el == v1, lane, big), axis=-1, keepdims=True)
    el2 = jnp.where(lane == i1, -jnp.inf, el)
    v2 = jnp.max(el2, axis=-1, keepdims=True)
    i2 = jnp.min(jnp.where(el2 == v2, lane, big), axis=-1, keepdims=True)
    r = jnp.exp(v2 - v1)
    w1 = 1.0 / (1.0 + r)
    g1 = p_g * w1
    g2 = p_g * (r * w1)
    e1 = i1 - N_GROUPS
    e2 = i2 - N_GROUPS
    oh1 = (lane == e1)
    oh2 = (lane == e2)
    occ = oh1.astype(F32) + oh2.astype(F32)
    before = jnp.dot(tri_ref[...], occ.astype(BF16), preferred_element_type=F32) + base_ref[...]
    r1 = jnp.sum(jnp.where(oh1, before, 0.0), axis=-1, keepdims=True)
    r2 = jnp.sum(jnp.where(oh2, before, 0.0), axis=-1, keepdims=True)
    base_ref[...] += jnp.sum(occ, axis=0, keepdims=True)
    cnt_ref[...] = base_ref[...]
    sel = jnp.where(lane == 0, e1, jnp.where(lane == 1, e2, 0))
    sel = jnp.where(lane == 2, r1.astype(I32), jnp.where(lane == 3, r2.astype(I32), sel))
    sel_ref[...] = sel
    gate_ref[...] = jnp.where(lane == 0, g1, jnp.where(lane == 1, g2, 0.0))


def _route(logits, tm=256):
    t = logits.shape[0]
    tm = min(tm, t)
    tri = jnp.asarray(np.tril(np.ones((tm, tm), np.float32), -1), BF16)
    return pl.pallas_call(
        _route_kernel,
        grid=(t // tm,),
        in_specs=[pl.BlockSpec((tm, LANES), lambda i: (i, 0)),
                  pl.BlockSpec((tm, tm), lambda i: (0, 0))],
        out_specs=[pl.BlockSpec((tm, LANES), lambda i: (i, 0)),
                   pl.BlockSpec((tm, LANES), lambda i: (i, 0)),
                   pl.BlockSpec((1, LANES), lambda i: (0, 0))],
        out_shape=[jax.ShapeDtypeStruct((t, LANES), I32),
                   jax.ShapeDtypeStruct((t, LANES), F32),
                   jax.ShapeDtypeStruct((1, LANES), F32)],
        scratch_shapes=[pltpu.VMEM((1, LANES), F32)],
        compiler_params=_params(("arbitrary",)),
        name="route",
    )(logits, tri)


def _lane_prefix(x, lane):
    shift = 1
    while shift < LANES:
        x = x + jnp.where(lane >= shift, pltpu.roll(x, shift, axis=1), 0)
        shift *= 2
    return x


def _dest_kernel(sel_ref, cnt_ref, dest_ref, blk_ref, *, n_blk_pad):
    tm = sel_ref.shape[0]
    lane1 = lax.broadcasted_iota(I32, (8, LANES), 1)
    cnt = jnp.broadcast_to(cnt_ref[...].astype(I32), (8, LANES))
    cnt = jnp.where(lane1 < N_EXPERTS, cnt, 0)
    padded = jnp.bitwise_and(cnt + (MOE_TM - 1), -MOE_TM)
    pend = _lane_prefix(padded, lane1)
    pstart = (pend - padded)[0:1, :]
    sel = sel_ref[...]
    lane = lax.broadcasted_iota(I32, (tm, LANES), 1)
    e1 = sel[:, 0:1]
    e2 = sel[:, 1:2]
    d1 = jnp.sum(jnp.where(lane == e1, pstart, 0), axis=-1, keepdims=True) + sel[:, 2:3]
    d2 = jnp.sum(jnp.where(lane == e2, pstart, 0), axis=-1, keepdims=True) + sel[:, 3:4]
    dest_ref[...] = jnp.where(lane == 0, d1, jnp.where(lane == 1, d2, 0))
    brow = lax.broadcasted_iota(I32, (n_blk_pad, LANES), 0) * MOE_TM
    blane = lax.broadcasted_iota(I32, (n_blk_pad, LANES), 1)
    ended = jnp.logical_and(blane < N_EXPERTS, pend[0:1, :] <= brow)
    be = jnp.minimum(jnp.sum(ended.astype(I32), axis=-1, keepdims=True), N_EXPERTS - 1)
    blk_ref[...] = jnp.broadcast_to(be, (n_blk_pad, LANES))


def _dest(sel, counts, n_blk, tm=256):
    t = sel.shape[0]
    tm = min(tm, t)
    n_blk_pad = -(-n_blk // 8) * 8
    return pl.pallas_call(
        functools.partial(_dest_kernel, n_blk_pad=n_blk_pad),
        grid=(t // tm,),
        in_specs=[pl.BlockSpec((tm, LANES), lambda i: (i, 0)),
                  pl.BlockSpec((1, LANES), lambda i: (0, 0))],
        out_specs=[pl.BlockSpec((tm, LANES), lambda i: (i, 0)),
                   pl.BlockSpec((n_blk_pad, LANES), lambda i: (0, 0))],
        out_shape=[jax.ShapeDtypeStruct((t, LANES), I32),
                   jax.ShapeDtypeStruct((n_blk_pad, LANES), I32)],
        compiler_params=_params(("arbitrary",)),
        name="dest",
    )(sel, counts)


def _dispatch_kernel(dest_ref, h_ref, xs_in_ref, xs_ref, sem):
    del xs_in_ref
    tm = h_ref.shape[0]
    base = pl.program_id(0) * (2 * tm)

    def copy(r, k):
        return pltpu.make_async_copy(h_ref.at[pl.ds(r, 1)],
                                     xs_ref.at[pl.ds(dest_ref[base + 2 * r + k], 1)], sem)

    def start(r, c):
        copy(r, 0).start()
        copy(r, 1).start()
        return c

    lax.fori_loop(0, tm, start, 0)

    def wait(r, c):
        copy(r, 0).wait()
        copy(r, 1).wait()
        return c

    lax.fori_loop(0, tm, wait, 0)


def _dispatch(dest_flat, h2, m_pad, tm=256):
    t, d = h2.shape
    tm = min(tm, t)
    xs0 = jnp.zeros((m_pad, d), h2.dtype)
    return pl.pallas_call(
        _dispatch_kernel,
        grid_spec=pltpu.PrefetchScalarGridSpec(
            num_scalar_prefetch=1,
            grid=(t // tm,),
            in_specs=[pl.BlockSpec((tm, d), lambda i, dest: (i, 0)),
                      pl.BlockSpec(memory_space=pl.ANY)],
            out_specs=pl.BlockSpec(memory_space=pl.ANY),
            scratch_shapes=[pltpu.SemaphoreType.DMA(())]),
        out_shape=jax.ShapeDtypeStruct((m_pad, d), h2.dtype),
        input_output_aliases={2: 0},
        compiler_params=_params(("arbitrary",)),
        name="dispatch",
    )(dest_flat, h2, xs0)


def _experts_kernel(blk_ref, xs_ref, wg_ref, wu_ref, wd_ref, ys_ref, wg_bf, wu_bf, wd_bf):
    i = pl.program_id(0)
    changed = jnp.logical_or(i == 0, blk_ref[i] != blk_ref[jnp.maximum(i - 1, 0)])

    @pl.when(changed)
    def _():
        wg_bf[...] = wg_ref[0, 0].astype(BF16)
        wu_bf[...] = wu_ref[0, 0].astype(BF16)
        wd_bf[...] = wd_ref[0, 0].astype(BF16)

    xb = xs_ref[...].astype(BF16)
    g = jnp.dot(xb, wg_bf[...], preferred_element_type=F32)
    u = jnp.dot(xb, wu_bf[...], preferred_element_type=F32)
    hmid = (g * jax.nn.sigmoid(g) * u).astype(BF16)
    ys_ref[...] = jnp.dot(hmid, wd_bf[...], preferred_element_type=F32)


def _experts(blk_expert, xs, w_gate, w_up, w_down):
    m_pad, d = xs.shape
    de = w_gate.shape[-1]
    n_blk = m_pad // MOE_TM
    return pl.pallas_call(
        _experts_kernel,
        grid_spec=pltpu.PrefetchScalarGridSpec(
            num_scalar_prefetch=1,
            grid=(n_blk,),
            in_specs=[pl.BlockSpec((MOE_TM, d), lambda i, blk: (i, 0)),
                      pl.BlockSpec((1, 1, d, de), lambda i, blk: (0, blk[i], 0, 0)),
                      pl.BlockSpec((1, 1, d, de), lambda i, blk: (0, blk[i], 0, 0)),
                      pl.BlockSpec((1, 1, de, d), lambda i, blk: (0, blk[i], 0, 0))],
            out_specs=pl.BlockSpec((MOE_TM, d), lambda i, blk: (i, 0)),
            scratch_shapes=[pltpu.VMEM((d, de), BF16),
                            pltpu.VMEM((d, de), BF16),
                            pltpu.VMEM((de, d), BF16)]),
        out_shape=jax.ShapeDtypeStruct((m_pad, d), F32),
        compiler_params=_params(("arbitrary",)),
        name="experts",
    )(blk_expert, xs, w_gate, w_up, w_down)


def _combine_kernel(dest_ref, ys_ref, gate_ref, x1_ref, mod_ref, g_ref, b_ref, o_ref,
                    y0_buf, y1_buf, sem):
    tm = x1_ref.shape[1]
    step = pl.program_id(0) * pl.num_programs(1) + pl.program_id(1)
    base = step * (2 * tm)

    def copy(r, k):
        buf = y0_buf if k == 0 else y1_buf
        return pltpu.make_async_copy(ys_ref.at[pl.ds(dest_ref[base + 2 * r + k], 1)],
                                     buf.at[pl.ds(r, 1)], sem)

    def start(r, c):
        copy(r, 0).start()
        copy(r, 1).start()
        return c

    lax.fori_loop(0, tm, start, 0)

    def wait(r, c):
        copy(r, 0).wait()
        copy(r, 1).wait()
        return c

    lax.fori_loop(0, tm, wait, 0)

    gates = gate_ref[0]
    ffn = gates[:, 0:1] * y0_buf[...] + gates[:, 1:2] * y1_buf[...]
    gate2 = mod_ref[0, 5:6, :]
    o_ref[0] = _layer_norm(ALPHA * x1_ref[0] + (1.0 + gate2) * ffn, g_ref[...], b_ref[...])


def _combine(dest_flat, ys, gates3, x1, mod3, ln_g, ln_b, tm=256):
    bsz, s, d = x1.shape
    tm = min(tm, s)
    return pl.pallas_call(
        _combine_kernel,
        grid_spec=pltpu.PrefetchScalarGridSpec(
            num_scalar_prefetch=1,
            grid=(bsz, s // tm),
            in_specs=[pl.BlockSpec(memory_space=pl.ANY),
                      pl.BlockSpec((1, tm, LANES), lambda b, i, dest: (b, i, 0)),
                      pl.BlockSpec((1, tm, d), lambda b, i, dest: (b, i, 0)),
                      pl.BlockSpec((1, 6, d), lambda b, i, dest: (b, 0, 0)),
                      pl.BlockSpec((1, d), lambda b, i, dest: (0, 0)),
                      pl.BlockSpec((1, d), lambda b, i, dest: (0, 0))],
            out_specs=pl.BlockSpec((1, tm, d), lambda b, i, dest: (b, i, 0)),
            scratch_shapes=[pltpu.VMEM((tm, d), F32),
                            pltpu.VMEM((tm, d), F32),
                            pltpu.SemaphoreType.DMA(())]),
        out_shape=jax.ShapeDtypeStruct((bsz, s, d), F32),
        compiler_params=_params(("arbitrary", "arbitrary")),
        name="combine",
    )(dest_flat, ys, gates3, x1, mod3, ln_g, ln_b)


def kernel(x, c, w_in, w_out, sinks, rel_bias, norm_a, norm_b, w_ada, b_ada, ln1_g, ln1_b,
           ln2_g, ln2_b, w_grp, b_grp, w_rtr, b_rtr, w_gate, w_up, w_down):
    bsz, s, d = x.shape
    t = bsz * s
    d_a = norm_a.shape[-1]
    d_b = norm_b.shape[-1]

    mod3 = _adaln(c, w_ada, b_ada).reshape(bsz, 6, d)

    qkv_dim = w_in.shape[-1]
    colv = np.ones((qkv_dim,), np.float32)
    colv[:d_a] = ATTN_SCALE
    qb0 = d_a + 2 * KV_A * HEAD_DIM
    colv[qb0:qb0 + d_b] = ATTN_SCALE
    w_in_bf = (w_in[0] * colv).astype(BF16)
    qkv = _qkv(x, mod3, w_in_bf)

    o_a = _swa(qkv, sinks[0], _swa_bias(rel_bias), d_a)
    o_b = _sb(qkv, d_a, d_b)

    w_r = jnp.concatenate([w_grp[0], w_rtr[0]], axis=1)
    w_r = jnp.pad(w_r, ((0, 0), (0, LANES - w_r.shape[1])))
    b_r = jnp.pad(jnp.concatenate([b_grp[0], b_rtr[0]]), (0, LANES - N_GROUPS - N_EXPERTS))[None, :]
    wr_hi = w_r.astype(BF16)
    wr_lo = (w_r - wr_hi.astype(F32)).astype(BF16)
    x1, h2, logits = _mix_ln1(o_a, o_b, x, mod3, norm_a, norm_b, w_out[0].astype(BF16),
                              ln1_g, ln1_b, wr_hi, wr_lo, b_r)

    sel, gates, counts = _route(logits.reshape(t, LANES))
    m_pad = 2 * t + N_EXPERTS * MOE_TM
    n_blk = m_pad // MOE_TM
    dest, blk = _dest(sel, counts, n_blk)
    dest_flat = dest[:, :2].reshape(2 * t)
    blk_expert = blk[:n_blk, 0]

    xs = _dispatch(dest_flat, h2.reshape(t, d), m_pad)
    ys = _experts(blk_expert, xs, w_gate, w_up, w_down)
    return _combine(dest_flat, ys, gates.reshape(bsz, s, LANES), x1, mod3, ln2_g, ln2_b)
```

```python
import functools
import math

import jax
import jax.numpy as jnp
import numpy as np
from jax import lax
from jax.experimental import pallas as pl
from jax.experimental.pallas import tpu as pltpu

F32 = jnp.float32
BF16 = jnp.bfloat16
I32 = jnp.int32

HEAD_DIM = 64
KV_A = 2
NUM_BUCKETS = 32
MAX_DISTANCE = 128
WINDOW = 128
Q_BLOCK = 128
N_GROUPS = 4
EXPERTS_PER_GROUP = 8
N_EXPERTS = N_GROUPS * EXPERTS_PER_GROUP
DEPTH = 1
ALPHA = (2.0 * DEPTH) ** 0.25
ATTN_SCALE = 1.0 / math.sqrt(HEAD_DIM)
EPS = 1e-5
NEG_INF = -1e30

LANES = 128
MOE_TM = 256
ROW_UNROLL = 8
SB_GROUP = 8
SB_SKIP_BITS = 160.0
VMEM_LIMIT = 48 * 1024 * 1024


def _params(sem, vmem=VMEM_LIMIT):
    return pltpu.CompilerParams(dimension_semantics=sem, vmem_limit_bytes=vmem)


def _pack_halves(y):
    n = y.shape[-1] // 2
    lo = pltpu.bitcast(y[:, :n].astype(BF16).astype(F32), jnp.uint32)
    hi = pltpu.bitcast(y[:, n:].astype(BF16).astype(F32), jnp.uint32)
    return hi | (lo >> 16)


def _unpack_halves(p):
    lo = pltpu.bitcast(p << 16, F32)
    hi = pltpu.bitcast(p & jnp.uint32(0xFFFF0000), F32)
    return lo, hi


def _adaln_kernel(c_ref, w_ref, b_ref, o_ref):
    c = c_ref[...]
    ca = (c * jax.nn.sigmoid(c)).astype(BF16)
    o_ref[...] = jnp.dot(ca, w_ref[0].astype(BF16), preferred_element_type=F32) + b_ref[...]


def _adaln(c, w_ada, b_ada, tn=1024):
    bsz, d = c.shape
    n = w_ada.shape[-1]
    return pl.pallas_call(
        _adaln_kernel,
        grid=(n // tn,),
        in_specs=[pl.BlockSpec((bsz, d), lambda j: (0, 0)),
                  pl.BlockSpec((1, d, tn), lambda j: (0, 0, j)),
                  pl.BlockSpec((1, tn), lambda j: (0, j))],
        out_specs=pl.BlockSpec((bsz, tn), lambda j: (0, j)),
        out_shape=jax.ShapeDtypeStruct((bsz, n), F32),
        compiler_params=_params(("arbitrary",)),
        name="adaln",
    )(c, w_ada, b_ada)


def _qkv_kernel(x_ref, mod_ref, w_ref, o_ref):
    shift = mod_ref[0, 0:1, :]
    scale = mod_ref[0, 1:2, :]
    h = (x_ref[0] * (1.0 + scale) + shift).astype(BF16)
    o_ref[0] = jnp.dot(h, w_ref[...], preferred_element_type=F32).astype(BF16)


def _qkv(x, mod3, w_in_bf, tm=512, nj=2):
    bsz, s, d = x.shape
    n = w_in_bf.shape[1]
    tn = n // nj
    tm = min(tm, s)
    return pl.pallas_call(
        _qkv_kernel,
        grid=(nj, bsz, s // tm),
        in_specs=[pl.BlockSpec((1, tm, d), lambda j, b, i: (b, i, 0)),
                  pl.BlockSpec((1, 6, d), lambda j, b, i: (b, 0, 0)),
                  pl.BlockSpec((d, tn), lambda j, b, i: (0, j))],
        out_specs=pl.BlockSpec((1, tm, tn), lambda j, b, i: (b, i, j)),
        out_shape=jax.ShapeDtypeStruct((bsz, s, n), BF16),
        compiler_params=_params(("arbitrary", "arbitrary", "arbitrary")),
        name="qkv",
    )(x, mod3, w_in_bf)


def _bucket_map():
    qi = np.arange(WINDOW)[:, None]
    kj = np.arange(2 * WINDOW)[None, :]
    dist = qi + WINDOW - kj
    n = np.maximum(dist, 0)
    max_exact = NUM_BUCKETS // 2
    ratio = np.maximum(n, max_exact).astype(np.float32) / np.float32(max_exact)
    large = max_exact + (np.log(ratio) / np.float32(math.log(MAX_DISTANCE / max_exact))
                         * np.float32(NUM_BUCKETS - max_exact)).astype(np.int32)
    large = np.minimum(large, NUM_BUCKETS - 1)
    bucket = np.where(n < max_exact, n, large)
    band = (dist >= 0) & (dist < WINDOW)
    return np.where(band, bucket, -1).astype(np.int32)


def _swa_bias_kernel(rb_ref, bucket_ref, o_ref):
    h = pl.program_id(0)
    bucket = bucket_ref[...]
    acc = jnp.full(bucket.shape, NEG_INF, F32)
    for b in range(NUM_BUCKETS):
        acc = jnp.where(bucket == b, rb_ref[b, h], acc)
    o_ref[0] = acc


def _swa_bias(rel_bias):
    nh = rel_bias.shape[1]
    bucket = jnp.asarray(_bucket_map())
    return pl.pallas_call(
        _swa_bias_kernel,
        grid=(nh,),
        in_specs=[pl.BlockSpec(memory_space=pltpu.SMEM),
                  pl.BlockSpec((WINDOW, 2 * WINDOW), lambda h: (0, 0))],
        out_specs=pl.BlockSpec((1, WINDOW, 2 * WINDOW), lambda h: (h, 0, 0)),
        out_shape=jax.ShapeDtypeStruct((nh, WINDOW, 2 * WINDOW), F32),
        compiler_params=_params(("arbitrary",)),
        name="swa_bias",
    )(rel_bias, bucket)


def _swa_kernel(sink_ref, q_ref, kvc_ref, kvp_ref, bias_ref, o_ref, *, n_heads):
    i = pl.program_id(1)
    group = n_heads // KV_A
    kv = jnp.concatenate([kvp_ref[0], kvc_ref[0]], axis=0)
    lane = lax.broadcasted_iota(I32, (2 * WINDOW, LANES), 1)
    low = lane < HEAD_DIM

    def placed(pair):
        swapped = pltpu.roll(pair.astype(F32), HEAD_DIM, axis=1).astype(BF16)
        zero = jnp.zeros_like(pair)
        return [[jnp.where(low, pair, zero), jnp.where(low, zero, swapped)],
                [jnp.where(low, swapped, zero), jnp.where(low, zero, pair)]]

    kz = placed(kv[:, 0:LANES])
    vz = placed(kv[:, LANES:2 * LANES])
    col = lax.broadcasted_iota(I32, (1, 2 * WINDOW), 1)
    first_pen = jnp.where(jnp.logical_and(i == 0, col < WINDOW), NEG_INF, 0.0).astype(F32)

    for p in range(n_heads // 2):
        qp = q_ref[0, :, p * LANES:(p + 1) * LANES]
        acc = jnp.zeros((WINDOW, LANES), F32)
        for pos in range(2):
            h = 2 * p + pos
            g = h // group
            s = lax.dot_general(qp, kz[g][pos], (((1,), (1,)), ((), ())),
                                preferred_element_type=F32)
            logits = s + bias_ref[h] + first_pen
            sink = sink_ref[h]
            m = jnp.maximum(jnp.max(logits, axis=-1, keepdims=True), sink)
            e = jnp.exp(logits - m)
            den = jnp.sum(e, axis=-1, keepdims=True) + jnp.exp(sink - m)
            o = jnp.dot(e.astype(BF16), vz[g][pos], preferred_element_type=F32)
            acc = acc + o * (1.0 / den)
        o_ref[0, :, p * LANES:(p + 1) * LANES] = acc.astype(BF16)


def _swa(qkv, sinks, bias, d_a, d_b):
    bsz, s, _ = qkv.shape
    n_heads = d_a // HEAD_DIM
    q_blk = 3 * d_b // d_a
    kv_blk = (3 * d_b + d_a) // (2 * LANES)
    return pl.pallas_call(
        functools.partial(_swa_kernel, n_heads=n_heads),
        grid=(bsz, s // WINDOW),
        in_specs=[pl.BlockSpec(memory_space=pltpu.SMEM),
                  pl.BlockSpec((1, WINDOW, d_a), lambda b, i: (b, i, q_blk)),
                  pl.BlockSpec((1, WINDOW, 2 * LANES), lambda b, i: (b, i, kv_blk)),
                  pl.BlockSpec((1, WINDOW, 2 * LANES),
                               lambda b, i: (b, jnp.maximum(i - 1, 0), kv_blk)),
                  pl.BlockSpec((n_heads, WINDOW, 2 * WINDOW), lambda b, i: (0, 0, 0))],
        out_specs=pl.BlockSpec((1, WINDOW, d_a), lambda b, i: (b, i, 0)),
        out_shape=jax.ShapeDtypeStruct((bsz, s, d_a), BF16),
        compiler_params=_params(("arbitrary", "arbitrary")),
        name="swa",
    )(sinks, qkv, qkv, qkv, bias)


def _suffix_matrix():
    j = np.arange(Q_BLOCK)[:, None]
    s = np.arange(Q_BLOCK)[None, :]
    return np.concatenate([(j > s), np.ones((Q_BLOCK, Q_BLOCK), bool)], axis=1).astype(np.float32)


def _sb_kernel(q_ref, k_ref, v_ref, lt_ref, o_ref, acc_ref, carry_ref, *, group):
    i = pl.program_id(2)
    lane = lax.broadcasted_iota(I32, (Q_BLOCK, LANES), 1)
    low = lane < HEAD_DIM
    row = lax.broadcasted_iota(I32, (2 * Q_BLOCK, Q_BLOCK), 0)
    col = lax.broadcasted_iota(I32, (2 * Q_BLOCK, Q_BLOCK), 1)
    strict = col < jnp.where(row >= Q_BLOCK, row - Q_BLOCK, row)
    sign = jnp.uint32(0x80000000)

    qz = []
    for g in range(group):
        q = q_ref[0, :, g * LANES:(g + 1) * LANES]
        zero = jnp.zeros_like(q)
        qz.append(jnp.concatenate([jnp.where(low, q, zero), jnp.where(low, zero, q)], axis=0))

    acc_ref[...] = jnp.zeros_like(acc_ref)
    carry_ref[...] = jnp.zeros_like(carry_ref)

    def block(j, diagonal):
        start = pl.multiple_of(j * Q_BLOCK, Q_BLOCK)
        gs = range(group)
        z = [lax.dot_general(qz[g], k_ref[0, pl.ds(start, Q_BLOCK), g * LANES:(g + 1) * LANES],
                             (((1,), (1,)), ((), ())), preferred_element_type=F32) for g in gs]
        sp = []
        for g in gs:
            neg_abs = pltpu.bitcast(pltpu.bitcast(z[g], jnp.uint32) | sign, F32)
            s = jnp.maximum(z[g], 0.0) + jnp.log2(1.0 + jnp.exp2(neg_abs))
            sp.append(jnp.where(strict, s, 0.0) if diagonal else s)
        cs = [jnp.dot(sp[g].astype(BF16), lt_ref[...], preferred_element_type=F32) for g in gs]
        a2 = []
        for g in gs:
            a = jnp.exp2(z[g] - sp[g] - cs[g][:, :Q_BLOCK] - carry_ref[g])
            if diagonal:
                a = jnp.where(strict, a, 0.0)
            a = a.astype(BF16)
            a2.append(jnp.concatenate([a[:Q_BLOCK], a[Q_BLOCK:]], axis=1))
        carry_min = None
        for g in gs:
            carry = carry_ref[g] + cs[g][:, Q_BLOCK:]
            carry_ref[g] = carry
            carry_min = carry if carry_min is None else jnp.minimum(carry_min, carry)
        for g in gs:
            vj = v_ref[0, pl.ds(start, Q_BLOCK), g * LANES:(g + 1) * LANES]
            vzero = jnp.zeros_like(vj)
            vz = jnp.concatenate([jnp.where(low, vj, vzero), jnp.where(low, vzero, vj)], axis=0)
            acc_ref[g] += jnp.dot(a2[g], vz, preferred_element_type=F32)
        return jnp.min(carry_min)

    block(i, True)

    def more(state):
        jj, smallest_carry = state
        return jnp.logical_and(jj < i, smallest_carry < SB_SKIP_BITS)

    def body(state):
        jj, _ = state
        return jj + 1, block(i - 1 - jj, False)

    lax.while_loop(more, body, (jnp.int32(0), jnp.float32(0.0)))
    for g in range(group):
        o_ref[0, :, g * LANES:(g + 1) * LANES] = acc_ref[g].astype(BF16)


def _sb(qkv, d_b, group=SB_GROUP):
    bsz, s, _ = qkv.shape
    pairs = d_b // LANES
    ng = pairs // group
    w = group * LANES
    lt = jnp.asarray(_suffix_matrix(), BF16)
    return pl.pallas_call(
        functools.partial(_sb_kernel, group=group),
        grid=(bsz, ng, s // Q_BLOCK),
        in_specs=[pl.BlockSpec((1, Q_BLOCK, w), lambda b, p, i: (b, i, p)),
                  pl.BlockSpec((1, s, w), lambda b, p, i: (b, 0, ng + p)),
                  pl.BlockSpec((1, s, w), lambda b, p, i: (b, 0, 2 * ng + p)),
                  pl.BlockSpec((Q_BLOCK, 2 * Q_BLOCK), lambda b, p, i: (0, 0))],
        out_specs=pl.BlockSpec((1, Q_BLOCK, w), lambda b, p, i: (b, i, p)),
        out_shape=jax.ShapeDtypeStruct((bsz, s, d_b), BF16),
        scratch_shapes=[pltpu.VMEM((group, Q_BLOCK, LANES), F32),
                        pltpu.VMEM((group, 2 * Q_BLOCK, LANES), F32)],
        compiler_params=_params(("arbitrary", "arbitrary", "arbitrary")),
        name="sb",
    )(qkv, qkv, qkv, lt)


def _layer_norm(y, g, b):
    mu = jnp.mean(y, axis=-1, keepdims=True)
    yc = y - mu
    var = jnp.mean(yc * yc, axis=-1, keepdims=True)
    return yc * lax.rsqrt(var + EPS) * g + b


def _rms(o, g):
    return o * lax.rsqrt(jnp.mean(o * o, axis=-1, keepdims=True) + EPS) * g


def _mix_ln1_kernel(oa_ref, ob_ref, x_ref, mod_ref, na_ref, nb_ref, wo_ref, g_ref, b_ref,
                    wrh_ref, wrl_ref, br_ref, x1_ref, h2_ref, lg_ref, *, d_a):
    ra = _rms(oa_ref[0].astype(F32), na_ref[...]).astype(BF16)
    rb = _rms(ob_ref[0].astype(F32), nb_ref[...]).astype(BF16)
    mix = (jnp.dot(ra, wo_ref[:d_a, :], preferred_element_type=F32)
           + jnp.dot(rb, wo_ref[d_a:, :], preferred_element_type=F32))
    gate1 = mod_ref[0, 2:3, :]
    shift2 = mod_ref[0, 3:4, :]
    scale2 = mod_ref[0, 4:5, :]
    x1 = _layer_norm(ALPHA * x_ref[0] + (1.0 + gate1) * mix, g_ref[...], b_ref[...])
    x1_ref[0] = x1
    h2 = x1 * (1.0 + scale2) + shift2
    h2_ref[0] = _pack_halves(h2)
    hi = h2.astype(BF16)
    lo = (h2 - hi.astype(F32)).astype(BF16)
    wh = wrh_ref[...]
    lg_ref[0] = (jnp.dot(hi, wh, preferred_element_type=F32)
                 + jnp.dot(lo, wh, preferred_element_type=F32)
                 + jnp.dot(hi, wrl_ref[...], preferred_element_type=F32)
                 + br_ref[...])


def _mix_ln1(o_a, o_b, x, mod3, norm_a, norm_b, w_out_bf, ln_g, ln_b, wr_hi, wr_lo, b_r, tm=256):
    bsz, s, d = x.shape
    d_a = o_a.shape[-1]
    d_b = o_b.shape[-1]
    tm = min(tm, s)
    row = lambda b, i: (b, i, 0)
    const2 = lambda b, i: (0, 0)
    return pl.pallas_call(
        functools.partial(_mix_ln1_kernel, d_a=d_a),
        grid=(bsz, s // tm),
        in_specs=[pl.BlockSpec((1, tm, d_a), row),
                  pl.BlockSpec((1, tm, d_b), row),
                  pl.BlockSpec((1, tm, d), row),
                  pl.BlockSpec((1, 6, d), lambda b, i: (b, 0, 0)),
                  pl.BlockSpec((1, d_a), const2),
                  pl.BlockSpec((1, d_b), const2),
                  pl.BlockSpec((d_a + d_b, d), const2),
                  pl.BlockSpec((1, d), const2),
                  pl.BlockSpec((1, d), const2),
                  pl.BlockSpec((d, LANES), const2),
                  pl.BlockSpec((d, LANES), const2),
                  pl.BlockSpec((1, LANES), const2)],
        out_specs=[pl.BlockSpec((1, tm, d), row),
                   pl.BlockSpec((1, tm, d // 2), row),
                   pl.BlockSpec((1, tm, LANES), row)],
        out_shape=[jax.ShapeDtypeStruct((bsz, s, d), F32),
                   jax.ShapeDtypeStruct((bsz, s, d // 2), jnp.uint32),
                   jax.ShapeDtypeStruct((bsz, s, LANES), F32)],
        compiler_params=_params(("arbitrary", "arbitrary")),
        name="mix_ln1",
    )(o_a, o_b, x, mod3, norm_a, norm_b, w_out_bf, ln_g, ln_b, wr_hi, wr_lo, b_r)


def _route_kernel(lg_ref, tri_ref, sel_ref, gate_ref, cnt_ref, base_ref):
    step = pl.program_id(0)

    @pl.when(step == 0)
    def _():
        base_ref[...] = jnp.zeros_like(base_ref)

    lg = lg_ref[...]
    tm = lg.shape[0]
    lane = lax.broadcasted_iota(I32, (tm, LANES), 1)
    big = jnp.int32(2 * LANES)
    glog = jnp.where(lane < N_GROUPS, lg, -jnp.inf)
    gmax = jnp.max(glog, axis=-1, keepdims=True)
    g_sel = jnp.min(jnp.where(glog == gmax, lane, big), axis=-1, keepdims=True)
    p_g = 1.0 / jnp.sum(jnp.exp(glog - gmax), axis=-1, keepdims=True)
    lo = N_GROUPS + g_sel * EXPERTS_PER_GROUP
    in_grp = jnp.logical_and(lane >= lo, lane < lo + EXPERTS_PER_GROUP)
    el = jnp.where(in_grp, lg, -jnp.inf)
    v1 = jnp.max(el, axis=-1, keepdims=True)
    i1 = jnp.min(jnp.where(el == v1, lane, big), axis=-1, keepdims=True)
    el2 = jnp.where(lane == i1, -jnp.inf, el)
    v2 = jnp.max(el2, axis=-1, keepdims=True)
    i2 = jnp.min(jnp.where(el2 == v2, lane, big), axis=-1, keepdims=True)
    r = jnp.exp(v2 - v1)
    w1 = 1.0 / (1.0 + r)
    g1 = p_g * w1
    g2 = p_g * (r * w1)
    e1 = i1 - N_GROUPS
    e2 = i2 - N_GROUPS
    oh1 = (lane == e1)
    oh2 = (lane == e2)
    occ = oh1.astype(F32) + oh2.astype(F32)
    before = jnp.dot(tri_ref[...], occ.astype(BF16), preferred_element_type=F32) + base_ref[...]
    r1 = jnp.sum(jnp.where(oh1, before, 0.0), axis=-1, keepdims=True)
    r2 = jnp.sum(jnp.where(oh2, before, 0.0), axis=-1, keepdims=True)
    base_ref[...] += jnp.sum(occ, axis=0, keepdims=True)
    cnt_ref[...] = base_ref[...]
    sel = jnp.where(lane == 0, e1, jnp.where(lane == 1, e2, 0))
    sel = jnp.where(lane == 2, r1.astype(I32), jnp.where(lane == 3, r2.astype(I32), sel))
    sel_ref[...] = sel
    gate_ref[...] = jnp.where(lane == 0, g1, jnp.where(lane == 1, g2, 0.0))


def _route(logits, tm=256):
    t = logits.shape[0]
    tm = min(tm, t)
    tri = jnp.asarray(np.tril(np.ones((tm, tm), np.float32), -1), BF16)
    return pl.pallas_call(
        _route_kernel,
        grid=(t // tm,),
        in_specs=[pl.BlockSpec((tm, LANES), lambda i: (i, 0)),
                  pl.BlockSpec((tm, tm), lambda i: (0, 0))],
        out_specs=[pl.BlockSpec((tm, LANES), lambda i: (i, 0)),
                   pl.BlockSpec((tm, LANES), lambda i: (i, 0)),
                   pl.BlockSpec((1, LANES), lambda i: (0, 0))],
        out_shape=[jax.ShapeDtypeStruct((t, LANES), I32),
                   jax.ShapeDtypeStruct((t, LANES), F32),
                   jax.ShapeDtypeStruct((1, LANES), F32)],
        scratch_shapes=[pltpu.VMEM((1, LANES), F32)],
        compiler_params=_params(("arbitrary",)),
        name="route",
    )(logits, tri)


def _lane_prefix(x, lane):
    shift = 1
    while shift < LANES:
        x = x + jnp.where(lane >= shift, pltpu.roll(x, shift, axis=1), 0)
        shift *= 2
    return x


def _dest_kernel(sel_ref, cnt_ref, dest_ref, blk_ref, *, n_blk_pad):
    tm = sel_ref.shape[0]
    lane1 = lax.broadcasted_iota(I32, (8, LANES), 1)
    cnt = jnp.broadcast_to(cnt_ref[...].astype(I32), (8, LANES))
    cnt = jnp.where(lane1 < N_EXPERTS, cnt, 0)
    padded = jnp.bitwise_and(cnt + (MOE_TM - 1), -MOE_TM)
    pend = _lane_prefix(padded, lane1)
    pstart = (pend - padded)[0:1, :]
    sel = sel_ref[...]
    lane = lax.broadcasted_iota(I32, (tm, LANES), 1)
    e1 = sel[:, 0:1]
    e2 = sel[:, 1:2]
    d1 = jnp.sum(jnp.where(lane == e1, pstart, 0), axis=-1, keepdims=True) + sel[:, 2:3]
    d2 = jnp.sum(jnp.where(lane == e2, pstart, 0), axis=-1, keepdims=True) + sel[:, 3:4]
    dest_ref[...] = jnp.where(lane == 0, d1, jnp.where(lane == 1, d2, 0))
    brow = lax.broadcasted_iota(I32, (n_blk_pad, LANES), 0) * MOE_TM
    blane = lax.broadcasted_iota(I32, (n_blk_pad, LANES), 1)
    ended = jnp.logical_and(blane < N_EXPERTS, pend[0:1, :] <= brow)
    be = jnp.minimum(jnp.sum(ended.astype(I32), axis=-1, keepdims=True), N_EXPERTS - 1)
    blk_ref[...] = jnp.broadcast_to(be, (n_blk_pad, LANES))


def _dest(sel, counts, n_blk, tm=256):
    t = sel.shape[0]
    tm = min(tm, t)
    n_blk_pad = -(-n_blk // 8) * 8
    return pl.pallas_call(
        functools.partial(_dest_kernel, n_blk_pad=n_blk_pad),
        grid=(t // tm,),
        in_specs=[pl.BlockSpec((tm, LANES), lambda i: (i, 0)),
                  pl.BlockSpec((1, LANES), lambda i: (0, 0))],
        out_specs=[pl.BlockSpec((tm, LANES), lambda i: (i, 0)),
                   pl.BlockSpec((n_blk_pad, LANES), lambda i: (0, 0))],
        out_shape=[jax.ShapeDtypeStruct((t, LANES), I32),
                   jax.ShapeDtypeStruct((n_blk_pad, LANES), I32)],
        compiler_params=_params(("arbitrary",)),
        name="dest",
    )(sel, counts)


def _dispatch_kernel(dest_ref, h_ref, xs_in_ref, xs_ref, stage, sems):
    del xs_in_ref
    tm = h_ref.shape[0]
    step = pl.program_id(0)
    last = pl.num_programs(0) - 1
    slot = lax.rem(step, 2)
    base = step * (2 * tm)

    def wait_tile(sl):
        for _ in range(2):
            pltpu.make_async_copy(stage.at[sl], xs_ref.at[pl.ds(0, tm)], sems.at[sl]).wait()

    @pl.when(step >= 2)
    def _():
        wait_tile(slot)

    stage[slot] = h_ref[...]

    def start(c, carry):
        for u in range(ROW_UNROLL):
            r = c * ROW_UNROLL + u
            for k in range(2):
                pltpu.make_async_copy(stage.at[slot, pl.ds(r, 1)],
                                      xs_ref.at[pl.ds(dest_ref[base + 2 * r + k], 1)],
                                      sems.at[slot]).start()
        return carry

    lax.fori_loop(0, tm // ROW_UNROLL, start, 0)

    @pl.when(step == last)
    def _():
        wait_tile(slot)

        @pl.when(step >= 1)
        def _():
            wait_tile(1 - slot)


def _dispatch(dest_flat, h2p, m_pad, tm=256):
    t, dp = h2p.shape
    tm = min(tm, t)
    xs0 = jnp.zeros((m_pad, dp), h2p.dtype)
    return pl.pallas_call(
        _dispatch_kernel,
        grid_spec=pltpu.PrefetchScalarGridSpec(
            num_scalar_prefetch=1,
            grid=(t // tm,),
            in_specs=[pl.BlockSpec((tm, dp), lambda i, dest: (i, 0)),
                      pl.BlockSpec(memory_space=pl.ANY)],
            out_specs=pl.BlockSpec(memory_space=pl.ANY),
            scratch_shapes=[pltpu.VMEM((2, tm, dp), h2p.dtype),
                            pltpu.SemaphoreType.DMA((2,))]),
        out_shape=jax.ShapeDtypeStruct((m_pad, dp), h2p.dtype),
        input_output_aliases={2: 0},
        compiler_params=_params(("arbitrary",)),
        name="dispatch",
    )(dest_flat, h2p, xs0)


def _experts_kernel(blk_ref, xs_ref, wg_ref, wu_ref, wd_ref, ys_ref, wg_bf, wu_bf, wd_bf):
    i = pl.program_id(0)
    changed = jnp.logical_or(i == 0, blk_ref[i] != blk_ref[jnp.maximum(i - 1, 0)])

    @pl.when(changed)
    def _():
        wg_bf[...] = wg_ref[0, 0].astype(BF16)
        wu_bf[...] = wu_ref[0, 0].astype(BF16)
        wd_bf[...] = wd_ref[0, 0].astype(BF16)

    x_lo, x_hi = _unpack_halves(xs_ref[...])
    xb = jnp.concatenate([x_lo.astype(BF16), x_hi.astype(BF16)], axis=1)
    g = jnp.dot(xb, wg_bf[...], preferred_element_type=F32)
    u = jnp.dot(xb, wu_bf[...], preferred_element_type=F32)
    hmid = (g * jax.nn.sigmoid(g) * u).astype(BF16)
    ys_ref[...] = _pack_halves(jnp.dot(hmid, wd_bf[...], preferred_element_type=F32))


def _experts(blk_expert, xs, w_gate, w_up, w_down):
    m_pad, dp = xs.shape
    d, de = w_gate.shape[-2:]
    n_blk = m_pad // MOE_TM
    return pl.pallas_call(
        _experts_kernel,
        grid_spec=pltpu.PrefetchScalarGridSpec(
            num_scalar_prefetch=1,
            grid=(n_blk,),
            in_specs=[pl.BlockSpec((MOE_TM, dp), lambda i, blk: (i, 0)),
                      pl.BlockSpec((1, 1, d, de), lambda i, blk: (0, blk[i], 0, 0)),
                      pl.BlockSpec((1, 1, d, de), lambda i, blk: (0, blk[i], 0, 0)),
                      pl.BlockSpec((1, 1, de, d), lambda i, blk: (0, blk[i], 0, 0))],
            out_specs=pl.BlockSpec((MOE_TM, dp), lambda i, blk: (i, 0)),
            scratch_shapes=[pltpu.VMEM((d, de), BF16),
                            pltpu.VMEM((d, de), BF16),
                            pltpu.VMEM((de, d), BF16)]),
        out_shape=jax.ShapeDtypeStruct((m_pad, dp), jnp.uint32),
        compiler_params=_params(("arbitrary",)),
        name="experts",
    )(blk_expert, xs, w_gate, w_up, w_down)


def _combine_kernel(dest_ref, ys_ref, gate_ref, x1_ref, mod_ref, g_ref, b_ref, o_ref, ybuf, sems):
    tm = x1_ref.shape[1]
    n_steps = pl.num_programs(0) * pl.num_programs(1)
    step = pl.program_id(0) * pl.num_programs(1) + pl.program_id(1)
    slot = lax.rem(step, 2)

    def issue(st, sl):
        base = st * (2 * tm)

        def start(c, carry):
            for u in range(ROW_UNROLL):
                r = c * ROW_UNROLL + u
                for k in range(2):
                    pltpu.make_async_copy(ys_ref.at[pl.ds(dest_ref[base + 2 * r + k], 1)],
                                          ybuf.at[sl, k, pl.ds(r, 1)], sems.at[sl]).start()
            return carry

        lax.fori_loop(0, tm // ROW_UNROLL, start, 0)

    @pl.when(step == 0)
    def _():
        issue(0, 0)

    @pl.when(step + 1 < n_steps)
    def _():
        issue(step + 1, 1 - slot)

    for k in range(2):
        pltpu.make_async_copy(ys_ref.at[pl.ds(0, tm)], ybuf.at[slot, k], sems.at[slot]).wait()

    gates = gate_ref[0]
    y0_lo, y0_hi = _unpack_halves(ybuf[slot, 0])
    y1_lo, y1_hi = _unpack_halves(ybuf[slot, 1])
    g0 = gates[:, 0:1]
    g1 = gates[:, 1:2]
    ffn = jnp.concatenate([g0 * y0_lo + g1 * y1_lo, g0 * y0_hi + g1 * y1_hi], axis=1)
    gate2 = mod_ref[0, 5:6, :]
    o_ref[0] = _layer_norm(ALPHA * x1_ref[0] + (1.0 + gate2) * ffn, g_ref[...], b_ref[...])


def _combine(dest_flat, ys, gates3, x1, mod3, ln_g, ln_b, tm=256):
    bsz, s, d = x1.shape
    tm = min(tm, s)
    return pl.pallas_call(
        _combine_kernel,
        grid_spec=pltpu.PrefetchScalarGridSpec(
            num_scalar_prefetch=1,
            grid=(bsz, s // tm),
            in_specs=[pl.BlockSpec(memory_space=pl.ANY),
                      pl.BlockSpec((1, tm, LANES), lambda b, i, dest: (b, i, 0)),
                      pl.BlockSpec((1, tm, d), lambda b, i, dest: (b, i, 0)),
                      pl.BlockSpec((1, 6, d), lambda b, i, dest: (b, 0, 0)),
                      pl.BlockSpec((1, d), lambda b, i, dest: (0, 0)),
                      pl.BlockSpec((1, d), lambda b, i, dest: (0, 0))],
            out_specs=pl.BlockSpec((1, tm, d), lambda b, i, dest: (b, i, 0)),
            scratch_shapes=[pltpu.VMEM((2, 2, tm, d // 2), jnp.uint32),
                            pltpu.SemaphoreType.DMA((2,))]),
        out_shape=jax.ShapeDtypeStruct((bsz, s, d), F32),
        compiler_params=_params(("arbitrary", "arbitrary")),
        name="combine",
    )(dest_flat, ys, gates3, x1, mod3, ln_g, ln_b)


def kernel(x, c, w_in, w_out, sinks, rel_bias, norm_a, norm_b, w_ada, b_ada, ln1_g, ln1_b,
           ln2_g, ln2_b, w_grp, b_grp, w_rtr, b_rtr, w_gate, w_up, w_down):
    bsz, s, d = x.shape
    t = bsz * s
    d_a = norm_a.shape[-1]
    d_b = norm_b.shape[-1]

    mod3 = _adaln(c, w_ada, b_ada).reshape(bsz, 6, d)

    kv_w = 2 * KV_A * HEAD_DIM
    w0 = w_in[0]
    w_in_bf = jnp.concatenate(
        [w0[:, d_a + kv_w:d_a + kv_w + d_b] * (ATTN_SCALE * math.log2(math.e)),
         w0[:, d_a + kv_w + d_b:],
         w0[:, :d_a] * ATTN_SCALE,
         w0[:, d_a:d_a + kv_w]], axis=1).astype(BF16)
    qkv = _qkv(x, mod3, w_in_bf)

    o_a = _swa(qkv, sinks[0], _swa_bias(rel_bias), d_a, d_b)
    o_b = _sb(qkv, d_b)

    w_r = jnp.concatenate([w_grp[0], w_rtr[0]], axis=1)
    w_r = jnp.pad(w_r, ((0, 0), (0, LANES - w_r.shape[1])))
    b_r = jnp.pad(jnp.concatenate([b_grp[0], b_rtr[0]]), (0, LANES - N_GROUPS - N_EXPERTS))[None, :]
    wr_hi = w_r.astype(BF16)
    wr_lo = (w_r - wr_hi.astype(F32)).astype(BF16)
    x1, h2, logits = _mix_ln1(o_a, o_b, x, mod3, norm_a, norm_b, w_out[0].astype(BF16),
                              ln1_g, ln1_b, wr_hi, wr_lo, b_r)

    sel, gates, counts = _route(logits.reshape(t, LANES))
    m_pad = 2 * t + N_EXPERTS * MOE_TM
    n_blk = m_pad // MOE_TM
    dest, blk = _dest(sel, counts, n_blk)
    dest_flat = dest[:, :2].reshape(2 * t)
    blk_expert = blk[:n_blk, 0]

    xs = _dispatch(dest_flat, h2.reshape(t, d // 2), m_pad)
    ys = _experts(blk_expert, xs, w_gate, w_up, w_down)
    return _combine(dest_flat, ys, gates.reshape(bsz, s, LANES), x1, mod3, ln2_g, ln2_b)
```

```python
import functools
import math

import jax
import jax.numpy as jnp
import numpy as np
from jax import lax
from jax.experimental import pallas as pl
from jax.experimental.pallas import tpu as pltpu

F32 = jnp.float32
BF16 = jnp.bfloat16
I32 = jnp.int32

HEAD_DIM = 64
KV_A = 2
NUM_BUCKETS = 32
MAX_DISTANCE = 128
WINDOW = 128
Q_BLOCK = 128
N_GROUPS = 4
EXPERTS_PER_GROUP = 8
N_EXPERTS = N_GROUPS * EXPERTS_PER_GROUP
DEPTH = 1
ALPHA = (2.0 * DEPTH) ** 0.25
ATTN_SCALE = 1.0 / math.sqrt(HEAD_DIM)
EPS = 1e-5
NEG_INF = -1e30
LOG2E = math.log2(math.e)

LANES = 128
ROW_SUB = 8
MOE_TM = 256
ROW_UNROLL = 8
SWA_PAIRS = 4
SB_GROUP = 8
SB_SKIP_BITS = 160.0
VMEM_LIMIT = 48 * 1024 * 1024


def _params(sem, vmem=VMEM_LIMIT):
    return pltpu.CompilerParams(dimension_semantics=sem, vmem_limit_bytes=vmem)


def _store_row_tiles(ref_2d, y):
    n = y.shape[0]
    for s in range(ROW_SUB):
        lo = pltpu.bitcast(y[:, 2 * s * LANES:(2 * s + 1) * LANES].astype(BF16).astype(F32), jnp.uint32)
        hi = pltpu.bitcast(y[:, (2 * s + 1) * LANES:(2 * s + 2) * LANES].astype(BF16).astype(F32), jnp.uint32)
        ref_2d[pl.ds(s, n, stride=ROW_SUB), :] = hi | (lo >> 16)


def _load_row_tiles(ref_2d, n):
    chunks = []
    for s in range(ROW_SUB):
        p = ref_2d[pl.ds(s, n, stride=ROW_SUB), :]
        chunks.append(pltpu.bitcast(p << 16, F32))
        chunks.append(pltpu.bitcast(p & jnp.uint32(0xFFFF0000), F32))
    return chunks


def _adaln_kernel(c_ref, w_ref, b_ref, o_ref):
    c = c_ref[...]
    ca = (c * jax.nn.sigmoid(c)).astype(BF16)
    o_ref[...] = jnp.dot(ca, w_ref[0].astype(BF16), preferred_element_type=F32) + b_ref[...]


def _adaln(c, w_ada, b_ada, tn=1024):
    bsz, d = c.shape
    n = w_ada.shape[-1]
    return pl.pallas_call(
        _adaln_kernel,
        grid=(n // tn,),
        in_specs=[pl.BlockSpec((bsz, d), lambda j: (0, 0)),
                  pl.BlockSpec((1, d, tn), lambda j: (0, 0, j)),
                  pl.BlockSpec((1, tn), lambda j: (0, j))],
        out_specs=pl.BlockSpec((bsz, tn), lambda j: (0, j)),
        out_shape=jax.ShapeDtypeStruct((bsz, n), F32),
        compiler_params=_params(("arbitrary",)),
        name="adaln",
    )(c, w_ada, b_ada)


def _qkv_kernel(x_ref, mod_ref, w_ref, o_ref):
    shift = mod_ref[0, 0:1, :]
    scale = mod_ref[0, 1:2, :]
    h = (x_ref[0] * (1.0 + scale) + shift).astype(BF16)
    o_ref[0] = jnp.dot(h, w_ref[...], preferred_element_type=F32).astype(BF16)


def _qkv(x, mod3, w_in_bf, tm=512, nj=2):
    bsz, s, d = x.shape
    n = w_in_bf.shape[1]
    tn = n // nj
    tm = min(tm, s)
    return pl.pallas_call(
        _qkv_kernel,
        grid=(nj, bsz, s // tm),
        in_specs=[pl.BlockSpec((1, tm, d), lambda j, b, i: (b, i, 0)),
                  pl.BlockSpec((1, 6, d), lambda j, b, i: (b, 0, 0)),
                  pl.BlockSpec((d, tn), lambda j, b, i: (0, j))],
        out_specs=pl.BlockSpec((1, tm, tn), lambda j, b, i: (b, i, j)),
        out_shape=jax.ShapeDtypeStruct((bsz, s, n), BF16),
        compiler_params=_params(("arbitrary", "arbitrary", "arbitrary")),
        name="qkv",
    )(x, mod3, w_in_bf)


def _bucket_map():
    qi = np.arange(WINDOW)[:, None]
    kj = np.arange(2 * WINDOW)[None, :]
    dist = qi + WINDOW - kj
    n = np.maximum(dist, 0)
    max_exact = NUM_BUCKETS // 2
    ratio = np.maximum(n, max_exact).astype(np.float32) / np.float32(max_exact)
    large = max_exact + (np.log(ratio) / np.float32(math.log(MAX_DISTANCE / max_exact))
                         * np.float32(NUM_BUCKETS - max_exact)).astype(np.int32)
    large = np.minimum(large, NUM_BUCKETS - 1)
    bucket = np.where(n < max_exact, n, large)
    band = (dist >= 0) & (dist < WINDOW)
    return np.where(band, bucket, -1).astype(np.int32)


def _swa_bias_kernel(rb_ref, bucket_ref, o_ref):
    first = pl.program_id(0) == 0
    h = pl.program_id(1)
    bucket = bucket_ref[...]
    col = lax.broadcasted_iota(I32, bucket.shape, 1)
    acc = jnp.full(bucket.shape, NEG_INF, F32)
    for b in range(NUM_BUCKETS):
        acc = jnp.where(bucket == b, rb_ref[b, h] * LOG2E, acc)
    o_ref[0, 0] = jnp.where(jnp.logical_and(first, col < WINDOW), NEG_INF, acc)


def _swa_bias(rel_bias):
    nh = rel_bias.shape[1]
    bucket = jnp.asarray(_bucket_map())
    return pl.pallas_call(
        _swa_bias_kernel,
        grid=(2, nh),
        in_specs=[pl.BlockSpec(memory_space=pltpu.SMEM),
                  pl.BlockSpec((WINDOW, 2 * WINDOW), lambda v, h: (0, 0))],
        out_specs=pl.BlockSpec((1, 1, WINDOW, 2 * WINDOW), lambda v, h: (v, h, 0, 0)),
        out_shape=jax.ShapeDtypeStruct((2, nh, WINDOW, 2 * WINDOW), F32),
        compiler_params=_params(("arbitrary", "arbitrary")),
        name="swa_bias",
    )(rel_bias, bucket)


def _swa_kernel(sink_ref, q_ref, kvc_ref, kvp_ref, bias_ref, o_ref, *, n_heads):
    group = n_heads // KV_A
    kv = jnp.concatenate([kvp_ref[0], kvc_ref[0]], axis=0)
    lane = lax.broadcasted_iota(I32, (2 * WINDOW, LANES), 1)
    low = lane < HEAD_DIM

    def halves(pair):
        zero = jnp.zeros_like(pair)
        return [jnp.where(low, pair, zero), jnp.where(low, zero, pair)]

    kz = halves(kv[:, 0:LANES])
    vz = halves(kv[:, LANES:2 * LANES])

    n_pairs = n_heads // KV_A
    for p0 in range(0, n_pairs, SWA_PAIRS):
        pairs = range(p0, min(p0 + SWA_PAIRS, n_pairs))
        heads = [(p, g) for p in pairs for g in range(KV_A)]
        logits, e, den, o = {}, {}, {}, {}
        for p, g in heads:
            qp = q_ref[0, :, p * LANES:(p + 1) * LANES]
            s = lax.dot_general(qp, kz[g], (((1,), (1,)), ((), ())), preferred_element_type=F32)
            logits[p, g] = s + bias_ref[0, g * group + p]
        for p, g in heads:
            sink = sink_ref[g * group + p]
            m = jnp.maximum(jnp.max(logits[p, g], axis=-1, keepdims=True), sink)
            e[p, g] = jnp.exp2(logits[p, g] - m)
            den[p, g] = jnp.sum(e[p, g], axis=-1, keepdims=True) + jnp.exp2(sink - m)
        for p, g in heads:
            o[p, g] = jnp.dot(e[p, g].astype(BF16), vz[g], preferred_element_type=F32)
        for p in pairs:
            acc = o[p, 0] * (1.0 / den[p, 0])
            for g in range(1, KV_A):
                acc = acc + o[p, g] * (1.0 / den[p, g])
            o_ref[0, :, p * LANES:(p + 1) * LANES] = acc.astype(BF16)


def _swa(qkv, sinks, bias, d_a, d_b):
    bsz, s, _ = qkv.shape
    n_heads = d_a // HEAD_DIM
    q_blk = 3 * d_b // d_a
    kv_blk = (3 * d_b + d_a) // (2 * LANES)
    return pl.pallas_call(
        functools.partial(_swa_kernel, n_heads=n_heads),
        grid=(bsz, s // WINDOW),
        in_specs=[pl.BlockSpec(memory_space=pltpu.SMEM),
                  pl.BlockSpec((1, WINDOW, d_a), lambda b, i: (b, i, q_blk)),
                  pl.BlockSpec((1, WINDOW, 2 * LANES), lambda b, i: (b, i, kv_blk)),
                  pl.BlockSpec((1, WINDOW, 2 * LANES),
                               lambda b, i: (b, jnp.maximum(i - 1, 0), kv_blk)),
                  pl.BlockSpec((1, n_heads, WINDOW, 2 * WINDOW),
                               lambda b, i: (jnp.minimum(i, 1), 0, 0, 0))],
        out_specs=pl.BlockSpec((1, WINDOW, d_a), lambda b, i: (b, i, 0)),
        out_shape=jax.ShapeDtypeStruct((bsz, s, d_a), BF16),
        compiler_params=_params(("arbitrary", "arbitrary")),
        name="swa",
    )(sinks, qkv, qkv, qkv, bias)


def _suffix_matrix():
    j = np.arange(Q_BLOCK)[:, None]
    s = np.arange(Q_BLOCK)[None, :]
    return np.concatenate([(j > s), np.ones((Q_BLOCK, Q_BLOCK), bool)], axis=1).astype(np.float32)


def _sb_kernel(q_ref, k_ref, v_ref, lt_ref, o_ref, acc_ref, carry_ref, *, group):
    i = pl.program_id(2)
    lane = lax.broadcasted_iota(I32, (Q_BLOCK, LANES), 1)
    low = lane < HEAD_DIM
    row = lax.broadcasted_iota(I32, (2 * Q_BLOCK, Q_BLOCK), 0)
    col = lax.broadcasted_iota(I32, (2 * Q_BLOCK, Q_BLOCK), 1)
    strict = col < jnp.where(row >= Q_BLOCK, row - Q_BLOCK, row)
    sign = jnp.uint32(0x80000000)

    qz = []
    for g in range(group):
        q = q_ref[0, :, g * LANES:(g + 1) * LANES]
        zero = jnp.zeros_like(q)
        qz.append(jnp.concatenate([jnp.where(low, q, zero), jnp.where(low, zero, q)], axis=0))

    acc_ref[...] = jnp.zeros_like(acc_ref)
    carry_ref[...] = jnp.zeros_like(carry_ref)

    def block(j, diagonal):
        start = pl.multiple_of(j * Q_BLOCK, Q_BLOCK)
        gs = range(group)
        z = [lax.dot_general(qz[g], k_ref[0, pl.ds(start, Q_BLOCK), g * LANES:(g + 1) * LANES],
                             (((1,), (1,)), ((), ())), preferred_element_type=F32) for g in gs]
        sp = []
        for g in gs:
            neg_abs = pltpu.bitcast(pltpu.bitcast(z[g], jnp.uint32) | sign, F32)
            s = jnp.maximum(z[g], 0.0) + jnp.log2(1.0 + jnp.exp2(neg_abs))
            sp.append(jnp.where(strict, s, 0.0) if diagonal else s)
        cs = [jnp.dot(sp[g].astype(BF16), lt_ref[...], preferred_element_type=F32) for g in gs]
        a2 = []
        for g in gs:
            a = jnp.exp2(z[g] - sp[g] - cs[g][:, :Q_BLOCK] - carry_ref[g])
            if diagonal:
                a = jnp.where(strict, a, 0.0)
            a = a.astype(BF16)
            a2.append(jnp.concatenate([a[:Q_BLOCK], a[Q_BLOCK:]], axis=1))
        carry_min = None
        for g in gs:
            carry = carry_ref[g] + cs[g][:, Q_BLOCK:]
            carry_ref[g] = carry
            carry_min = carry if carry_min is None else jnp.minimum(carry_min, carry)
        for g in gs:
            vj = v_ref[0, pl.ds(start, Q_BLOCK), g * LANES:(g + 1) * LANES]
            vzero = jnp.zeros_like(vj)
            vz = jnp.concatenate([jnp.where(low, vj, vzero), jnp.where(low, vzero, vj)], axis=0)
            acc_ref[g] += jnp.dot(a2[g], vz, preferred_element_type=F32)
        return jnp.min(carry_min)

    block(i, True)

    def more(state):
        jj, smallest_carry = state
        return jnp.logical_and(jj < i, smallest_carry < SB_SKIP_BITS)

    def body(state):
        jj, _ = state
        return jj + 1, block(i - 1 - jj, False)

    lax.while_loop(more, body, (jnp.int32(0), jnp.float32(0.0)))
    for g in range(group):
        o_ref[0, :, g * LANES:(g + 1) * LANES] = acc_ref[g].astype(BF16)


def _sb(qkv, d_b, group=SB_GROUP):
    bsz, s, _ = qkv.shape
    pairs = d_b // LANES
    ng = pairs // group
    w = group * LANES
    lt = jnp.asarray(_suffix_matrix(), BF16)
    return pl.pallas_call(
        functools.partial(_sb_kernel, group=group),
        grid=(bsz, ng, s // Q_BLOCK),
        in_specs=[pl.BlockSpec((1, Q_BLOCK, w), lambda b, p, i: (b, i, p)),
                  pl.BlockSpec((1, s, w), lambda b, p, i: (b, 0, ng + p)),
                  pl.BlockSpec((1, s, w), lambda b, p, i: (b, 0, 2 * ng + p)),
                  pl.BlockSpec((Q_BLOCK, 2 * Q_BLOCK), lambda b, p, i: (0, 0))],
        out_specs=pl.BlockSpec((1, Q_BLOCK, w), lambda b, p, i: (b, i, p)),
        out_shape=jax.ShapeDtypeStruct((bsz, s, d_b), BF16),
        scratch_shapes=[pltpu.VMEM((group, Q_BLOCK, LANES), F32),
                        pltpu.VMEM((group, 2 * Q_BLOCK, LANES), F32)],
        compiler_params=_params(("arbitrary", "arbitrary", "arbitrary")),
        name="sb",
    )(qkv, qkv, qkv, lt)


def _layer_norm(y, g, b):
    mu = jnp.mean(y, axis=-1, keepdims=True)
    yc = y - mu
    var = jnp.mean(yc * yc, axis=-1, keepdims=True)
    return yc * lax.rsqrt(var + EPS) * g + b


def _rms(o, g):
    return o * lax.rsqrt(jnp.mean(o * o, axis=-1, keepdims=True) + EPS) * g


def _mix_ln1_kernel(oa_ref, ob_ref, x_ref, mod_ref, na_ref, nb_ref, wo_ref, g_ref, b_ref,
                    wrc_ref, br_ref, x1_ref, h2_ref, lg_ref, *, d_a, parts):
    hm = x_ref.shape[1] // parts
    rows = [pl.ds(p * hm, hm) for p in range(parts)]
    gate1 = mod_ref[0, 2:3, :]
    shift2 = mod_ref[0, 3:4, :]
    scale2 = mod_ref[0, 4:5, :]
    ra = [_rms(oa_ref[0, r, :].astype(F32), na_ref[...]).astype(BF16) for r in rows]
    rb = [_rms(ob_ref[0, r, :].astype(F32), nb_ref[...]).astype(BF16) for r in rows]
    mix = [jnp.dot(ra[p], wo_ref[:d_a, :], preferred_element_type=F32)
           + jnp.dot(rb[p], wo_ref[d_a:, :], preferred_element_type=F32) for p in range(parts)]
    hi, lo = [], []
    for p, r in enumerate(rows):
        x1 = _layer_norm(ALPHA * x_ref[0, r, :] + (1.0 + gate1) * mix[p], g_ref[...], b_ref[...])
        x1_ref[0, r, :] = x1
        h2 = x1 * (1.0 + scale2) + shift2
        _store_row_tiles(h2_ref.at[0, pl.ds(p * hm * ROW_SUB, hm * ROW_SUB)], h2)
        hi.append(h2.astype(BF16))
        lo.append((h2 - hi[p].astype(F32)).astype(BF16))
    for p, r in enumerate(rows):
        both = jnp.dot(hi[p], wrc_ref[...], preferred_element_type=F32)
        lg_ref[0, r, :] = (both[:, :LANES] + both[:, LANES:]
                           + jnp.dot(lo[p], wrc_ref[:, :LANES], preferred_element_type=F32)
                           + br_ref[...])


def _mix_ln1(o_a, o_b, x, mod3, norm_a, norm_b, w_out_bf, ln_g, ln_b, wr_cat, b_r, tm=512, parts=2):
    bsz, s, d = x.shape
    d_a = o_a.shape[-1]
    d_b = o_b.shape[-1]
    tm = min(tm, s)
    row = lambda b, i: (b, i, 0)
    const2 = lambda b, i: (0, 0)
    once = pl.Buffered(1)
    return pl.pallas_call(
        functools.partial(_mix_ln1_kernel, d_a=d_a, parts=parts),
        grid=(bsz, s // tm),
        in_specs=[pl.BlockSpec((1, tm, d_a), row),
                  pl.BlockSpec((1, tm, d_b), row),
                  pl.BlockSpec((1, tm, d), row),
                  pl.BlockSpec((1, 6, d), lambda b, i: (b, 0, 0)),
                  pl.BlockSpec((1, d_a), const2),
                  pl.BlockSpec((1, d_b), const2),
                  pl.BlockSpec((d_a + d_b, d), const2, pipeline_mode=once),
                  pl.BlockSpec((1, d), const2),
                  pl.BlockSpec((1, d), const2),
                  pl.BlockSpec((d, 2 * LANES), const2, pipeline_mode=once),
                  pl.BlockSpec((1, LANES), const2)],
        out_specs=[pl.BlockSpec((1, tm, d), row),
                   pl.BlockSpec((1, tm * ROW_SUB, LANES), row),
                   pl.BlockSpec((1, tm, LANES), row)],
        out_shape=[jax.ShapeDtypeStruct((bsz, s, d), F32),
                   jax.ShapeDtypeStruct((bsz, s * ROW_SUB, LANES), jnp.uint32),
                   jax.ShapeDtypeStruct((bsz, s, LANES), F32)],
        compiler_params=_params(("arbitrary", "arbitrary")),
        name="mix_ln1",
    )(o_a, o_b, x, mod3, norm_a, norm_b, w_out_bf, ln_g, ln_b, wr_cat, b_r)


def _route_kernel(lg_ref, tri_ref, sel_ref, gate_ref, cnt_ref, base_ref):
    step = pl.program_id(0)

    @pl.when(step == 0)
    def _():
        base_ref[...] = jnp.zeros_like(base_ref)

    lg = lg_ref[...]
    tm = lg.shape[0]
    lane = lax.broadcasted_iota(I32, (tm, LANES), 1)
    big = jnp.int32(2 * LANES)
    glog = jnp.where(lane < N_GROUPS, lg, -jnp.inf)
    gmax = jnp.max(glog, axis=-1, keepdims=True)
    g_sel = jnp.min(jnp.where(glog == gmax, lane, big), axis=-1, keepdims=True)
    p_g = 1.0 / jnp.sum(jnp.exp(glog - gmax), axis=-1, keepdims=True)
    lo = N_GROUPS + g_sel * EXPERTS_PER_GROUP
    in_grp = jnp.logical_and(lane >= lo, lane < lo + EXPERTS_PER_GROUP)
    el = jnp.where(in_grp, lg, -jnp.inf)
    v1 = jnp.max(el, axis=-1, keepdims=True)
    i1 = jnp.min(jnp.where(el == v1, lane, big), axis=-1, keepdims=True)
    el2 = jnp.where(lane == i1, -jnp.inf, el)
    v2 = jnp.max(el2, axis=-1, keepdims=True)
    i2 = jnp.min(jnp.where(el2 == v2, lane, big), axis=-1, keepdims=True)
    r = jnp.exp(v2 - v1)
    w1 = 1.0 / (1.0 + r)
    g1 = p_g * w1
    g2 = p_g * (r * w1)
    e1 = i1 - N_GROUPS
    e2 = i2 - N_GROUPS
    oh1 = (lane == e1)
    oh2 = (lane == e2)
    occ = oh1.astype(F32) + oh2.astype(F32)
    before = jnp.dot(tri_ref[...], occ.astype(BF16), preferred_element_type=F32) + base_ref[...]
    r1 = jnp.sum(jnp.where(oh1, before, 0.0), axis=-1, keepdims=True)
    r2 = jnp.sum(jnp.where(oh2, before, 0.0), axis=-1, keepdims=True)
    base_ref[...] += jnp.sum(occ, axis=0, keepdims=True)
    cnt_ref[...] = base_ref[...]
    sel = jnp.where(lane == 0, e1, jnp.where(lane == 1, e2, 0))
    sel = jnp.where(lane == 2, r1.astype(I32), jnp.where(lane == 3, r2.astype(I32), sel))
    sel_ref[...] = sel
    gate_ref[...] = jnp.where(lane == 0, g1, jnp.where(lane == 1, g2, 0.0))


def _route(logits, tm=256):
    t = logits.shape[0]
    tm = min(tm, t)
    tri = jnp.asarray(np.tril(np.ones((tm, tm), np.float32), -1), BF16)
    return pl.pallas_call(
        _route_kernel,
        grid=(t // tm,),
        in_specs=[pl.BlockSpec((tm, LANES), lambda i: (i, 0)),
                  pl.BlockSpec((tm, tm), lambda i: (0, 0))],
        out_specs=[pl.BlockSpec((tm, LANES), lambda i: (i, 0)),
                   pl.BlockSpec((tm, LANES), lambda i: (i, 0)),
                   pl.BlockSpec((1, LANES), lambda i: (0, 0))],
        out_shape=[jax.ShapeDtypeStruct((t, LANES), I32),
                   jax.ShapeDtypeStruct((t, LANES), F32),
                   jax.ShapeDtypeStruct((1, LANES), F32)],
        scratch_shapes=[pltpu.VMEM((1, LANES), F32)],
        compiler_params=_params(("arbitrary",)),
        name="route",
    )(logits, tri)


def _lane_prefix(x, lane):
    shift = 1
    while shift < LANES:
        x = x + jnp.where(lane >= shift, pltpu.roll(x, shift, axis=1), 0)
        shift *= 2
    return x


def _dest_kernel(sel_ref, cnt_ref, dest_ref, blk_ref, *, n_blk_pad):
    tm = sel_ref.shape[0]
    lane1 = lax.broadcasted_iota(I32, (8, LANES), 1)
    cnt = jnp.broadcast_to(cnt_ref[...].astype(I32), (8, LANES))
    cnt = jnp.where(lane1 < N_EXPERTS, cnt, 0)
    padded = jnp.bitwise_and(cnt + (MOE_TM - 1), -MOE_TM)
    pend = _lane_prefix(padded, lane1)
    pstart = (pend - padded)[0:1, :]
    sel = sel_ref[...]
    lane = lax.broadcasted_iota(I32, (tm, LANES), 1)
    e1 = sel[:, 0:1]
    e2 = sel[:, 1:2]
    d1 = jnp.sum(jnp.where(lane == e1, pstart, 0), axis=-1, keepdims=True) + sel[:, 2:3]
    d2 = jnp.sum(jnp.where(lane == e2, pstart, 0), axis=-1, keepdims=True) + sel[:, 3:4]
    dest_ref[...] = jnp.where(lane == 0, d1, jnp.where(lane == 1, d2, 0)) * ROW_SUB

    @pl.when(pl.program_id(0) == 0)
    def _():
        brow = lax.broadcasted_iota(I32, (n_blk_pad, LANES), 0) * MOE_TM
        blane = lax.broadcasted_iota(I32, (n_blk_pad, LANES), 1)
        ended = jnp.logical_and(blane < N_EXPERTS, pend[0:1, :] <= brow)
        be = jnp.minimum(jnp.sum(ended.astype(I32), axis=-1, keepdims=True), N_EXPERTS - 1)
        blk_ref[...] = jnp.broadcast_to(be, (n_blk_pad, LANES))


def _dest(sel, counts, n_blk, tm=256):
    t = sel.shape[0]
    tm = min(tm, t)
    n_blk_pad = -(-n_blk // 8) * 8
    return pl.pallas_call(
        functools.partial(_dest_kernel, n_blk_pad=n_blk_pad),
        grid=(t // tm,),
        in_specs=[pl.BlockSpec((tm, LANES), lambda i: (i, 0)),
                  pl.BlockSpec((1, LANES), lambda i: (0, 0))],
        out_specs=[pl.BlockSpec((tm, LANES), lambda i: (i, 0)),
                   pl.BlockSpec((n_blk_pad, LANES), lambda i: (0, 0))],
        out_shape=[jax.ShapeDtypeStruct((t, LANES), I32),
                   jax.ShapeDtypeStruct((n_blk_pad, LANES), I32)],
        compiler_params=_params(("arbitrary",)),
        name="dest",
    )(sel, counts)


def _dispatch_kernel(dest_ref, h_ref, xs_in_ref, xs_ref, stage, sems):
    del xs_in_ref
    tm = h_ref.shape[0] // ROW_SUB
    step = pl.program_id(0)
    last = pl.num_programs(0) - 1
    slot = lax.rem(step, 2)
    base = step * (2 * tm)

    def wait_tile(sl):
        for _ in range(2):
            pltpu.make_async_copy(stage.at[sl], xs_ref.at[pl.ds(0, tm * ROW_SUB)], sems.at[sl]).wait()

    @pl.when(step >= 2)
    def _():
        wait_tile(slot)

    stage[slot] = h_ref[...]

    def start(c, carry):
        for u in range(ROW_UNROLL):
            r = c * ROW_UNROLL + u
            src = stage.at[slot, pl.ds(pl.multiple_of(r * ROW_SUB, ROW_SUB), ROW_SUB)]
            for k in range(2):
                dst = pl.multiple_of(dest_ref[base + 2 * r + k], ROW_SUB)
                pltpu.make_async_copy(src, xs_ref.at[pl.ds(dst, ROW_SUB)], sems.at[slot]).start()
        return carry

    lax.fori_loop(0, tm // ROW_UNROLL, start, 0)

    @pl.when(step == last)
    def _():
        wait_tile(slot)

        @pl.when(step >= 1)
        def _():
            wait_tile(1 - slot)


def _dispatch(dest_flat, h2p, m_pad, tm=256):
    t = h2p.shape[0] // ROW_SUB
    tm = min(tm, t)
    xs0 = jnp.zeros((m_pad * ROW_SUB, LANES), h2p.dtype)
    return pl.pallas_call(
        _dispatch_kernel,
        grid_spec=pltpu.PrefetchScalarGridSpec(
            num_scalar_prefetch=1,
            grid=(t // tm,),
            in_specs=[pl.BlockSpec((tm * ROW_SUB, LANES), lambda i, dest: (i, 0)),
                      pl.BlockSpec(memory_space=pl.ANY)],
            out_specs=pl.BlockSpec(memory_space=pl.ANY),
            scratch_shapes=[pltpu.VMEM((2, tm * ROW_SUB, LANES), h2p.dtype),
                            pltpu.SemaphoreType.DMA((2,))]),
        out_shape=jax.ShapeDtypeStruct((m_pad * ROW_SUB, LANES), h2p.dtype),
        input_output_aliases={2: 0},
        compiler_params=_params(("arbitrary",)),
        name="dispatch",
    )(dest_flat, h2p, xs0)


def _experts_kernel(blk_ref, xs_ref, wg_ref, wu_ref, wd_ref, ys_ref, wg_bf, wu_bf, wd_bf):
    i = pl.program_id(0)
    changed = jnp.logical_or(i == 0, blk_ref[i] != blk_ref[jnp.maximum(i - 1, 0)])

    @pl.when(changed)
    def _():
        wg_bf[...] = wg_ref[0, 0].astype(BF16)
        wu_bf[...] = wu_ref[0, 0].astype(BF16)
        wd_bf[...] = wd_ref[0, 0].astype(BF16)

    xb = jnp.concatenate([c.astype(BF16) for c in _load_row_tiles(xs_ref, MOE_TM)], axis=1)
    g = jnp.dot(xb, wg_bf[...], preferred_element_type=F32)
    u = jnp.dot(xb, wu_bf[...], preferred_element_type=F32)
    hmid = (g * jax.nn.sigmoid(g) * u).astype(BF16)
    _store_row_tiles(ys_ref, jnp.dot(hmid, wd_bf[...], preferred_element_type=F32))


def _experts(blk_expert, xs, w_gate, w_up, w_down):
    d, de = w_gate.shape[-2:]
    n_blk = xs.shape[0] // (MOE_TM * ROW_SUB)
    return pl.pallas_call(
        _experts_kernel,
        grid_spec=pltpu.PrefetchScalarGridSpec(
            num_scalar_prefetch=1,
            grid=(n_blk,),
            in_specs=[pl.BlockSpec((MOE_TM * ROW_SUB, LANES), lambda i, blk: (i, 0)),
                      pl.BlockSpec((1, 1, d, de), lambda i, blk: (0, blk[i], 0, 0)),
                      pl.BlockSpec((1, 1, d, de), lambda i, blk: (0, blk[i], 0, 0)),
                      pl.BlockSpec((1, 1, de, d), lambda i, blk: (0, blk[i], 0, 0))],
            out_specs=pl.BlockSpec((MOE_TM * ROW_SUB, LANES), lambda i, blk: (i, 0)),
            scratch_shapes=[pltpu.VMEM((d, de), BF16),
                            pltpu.VMEM((d, de), BF16),
                            pltpu.VMEM((de, d), BF16)]),
        out_shape=jax.ShapeDtypeStruct(xs.shape, jnp.uint32),
        compiler_params=_params(("arbitrary",)),
        name="experts",
    )(blk_expert, xs, w_gate, w_up, w_down)


def _combine_kernel(dest_ref, ys_ref, gate_ref, x1_ref, mod_ref, g_ref, b_ref, o_ref, ybuf, sems):
    tm = x1_ref.shape[1]
    n_steps = pl.num_programs(0) * pl.num_programs(1)
    step = pl.program_id(0) * pl.num_programs(1) + pl.program_id(1)
    slot = lax.rem(step, 2)

    def issue(st, sl):
        base = st * (2 * tm)

        def start(c, carry):
            for u in range(ROW_UNROLL):
                r = c * ROW_UNROLL + u
                row = pl.ds(pl.multiple_of(r * ROW_SUB, ROW_SUB), ROW_SUB)
                for k in range(2):
                    src = pl.multiple_of(dest_ref[base + 2 * r + k], ROW_SUB)
                    pltpu.make_async_copy(ys_ref.at[pl.ds(src, ROW_SUB)],
                                          ybuf.at[sl, k, row], sems.at[sl]).start()
            return carry

        lax.fori_loop(0, tm // ROW_UNROLL, start, 0)

    @pl.when(step == 0)
    def _():
        issue(0, 0)

    @pl.when(step + 1 < n_steps)
    def _():
        issue(step + 1, 1 - slot)

    for k in range(2):
        pltpu.make_async_copy(ys_ref.at[pl.ds(0, tm * ROW_SUB)], ybuf.at[slot, k], sems.at[slot]).wait()

    gates = gate_ref[0]
    g0 = gates[:, 0:1]
    g1 = gates[:, 1:2]
    y0 = _load_row_tiles(ybuf.at[slot, 0], tm)
    y1 = _load_row_tiles(ybuf.at[slot, 1], tm)
    ffn = jnp.concatenate([g0 * a + g1 * b for a, b in zip(y0, y1)], axis=1)
    gate2 = mod_ref[0, 5:6, :]
    o_ref[0] = _layer_norm(ALPHA * x1_ref[0] + (1.0 + gate2) * ffn, g_ref[...], b_ref[...])


def _combine(dest_flat, ys, gates3, x1, mod3, ln_g, ln_b, tm=256):
    bsz, s, d = x1.shape
    tm = min(tm, s)
    return pl.pallas_call(
        _combine_kernel,
        grid_spec=pltpu.PrefetchScalarGridSpec(
            num_scalar_prefetch=1,
            grid=(bsz, s // tm),
            in_specs=[pl.BlockSpec(memory_space=pl.ANY),
                      pl.BlockSpec((1, tm, LANES), lambda b, i, dest: (b, i, 0)),
                      pl.BlockSpec((1, tm, d), lambda b, i, dest: (b, i, 0)),
                      pl.BlockSpec((1, 6, d), lambda b, i, dest: (b, 0, 0)),
                      pl.BlockSpec((1, d), lambda b, i, dest: (0, 0)),
                      pl.BlockSpec((1, d), lambda b, i, dest: (0, 0))],
            out_specs=pl.BlockSpec((1, tm, d), lambda b, i, dest: (b, i, 0)),
            scratch_shapes=[pltpu.VMEM((2, 2, tm * ROW_SUB, LANES), jnp.uint32),
                            pltpu.SemaphoreType.DMA((2,))]),
        out_shape=jax.ShapeDtypeStruct((bsz, s, d), F32),
        compiler_params=_params(("arbitrary", "arbitrary")),
        name="combine",
    )(dest_flat, ys, gates3, x1, mod3, ln_g, ln_b)


def kernel(x, c, w_in, w_out, sinks, rel_bias, norm_a, norm_b, w_ada, b_ada, ln1_g, ln1_b,
           ln2_g, ln2_b, w_grp, b_grp, w_rtr, b_rtr, w_gate, w_up, w_down):
    bsz, s, d = x.shape
    t = bsz * s
    d_a = norm_a.shape[-1]
    d_b = norm_b.shape[-1]

    mod3 = _adaln(c, w_ada, b_ada).reshape(bsz, 6, d)

    assert KV_A * HEAD_DIM == LANES
    kv_w = 2 * KV_A * HEAD_DIM
    group_a = d_a // HEAD_DIM // KV_A
    head_order = [g * group_a + p for p in range(group_a) for g in range(KV_A)]
    perm_a = np.concatenate([np.arange(HEAD_DIM) + HEAD_DIM * h for h in head_order])
    w0 = w_in[0]
    w_in_bf = jnp.concatenate(
        [w0[:, d_a + kv_w:d_a + kv_w + d_b] * (ATTN_SCALE * LOG2E),
         w0[:, d_a + kv_w + d_b:],
         w0[:, :d_a][:, perm_a] * (ATTN_SCALE * LOG2E),
         w0[:, d_a:d_a + kv_w]], axis=1).astype(BF16)
    qkv = _qkv(x, mod3, w_in_bf)
    norm_a = norm_a[:, perm_a]
    w_out_bf = jnp.concatenate([w_out[0][:d_a][perm_a], w_out[0][d_a:]], axis=0).astype(BF16)

    o_a = _swa(qkv, sinks[0] * LOG2E, _swa_bias(rel_bias), d_a, d_b)
    o_b = _sb(qkv, d_b)

    w_r = jnp.concatenate([w_grp[0], w_rtr[0]], axis=1)
    w_r = jnp.pad(w_r, ((0, 0), (0, LANES - w_r.shape[1])))
    b_r = jnp.pad(jnp.concatenate([b_grp[0], b_rtr[0]]), (0, LANES - N_GROUPS - N_EXPERTS))[None, :]
    wr_hi = w_r.astype(BF16)
    wr_lo = (w_r - wr_hi.astype(F32)).astype(BF16)
    assert d == 2 * LANES * ROW_SUB, "row tiles hold 256 * ROW_SUB features"
    x1, h2, logits = _mix_ln1(o_a, o_b, x, mod3, norm_a, norm_b, w_out_bf,
                              ln1_g, ln1_b, jnp.concatenate([wr_hi, wr_lo], axis=1), b_r)

    sel, gates, counts = _route(logits.reshape(t, LANES))
    m_pad = 2 * t + N_EXPERTS * MOE_TM
    n_blk = m_pad // MOE_TM
    dest, blk = _dest(sel, counts, n_blk)
    dest_flat = dest[:, :2].reshape(2 * t)
    blk_expert = blk[:n_blk, 0]

    xs = _dispatch(dest_flat, h2.reshape(t * ROW_SUB, LANES), m_pad)
    ys = _experts(blk_expert, xs, w_gate, w_up, w_down)
    return _combine(dest_flat, ys, gates.reshape(bsz, s, LANES), x1, mod3, ln2_g, ln2_b)
```

```python
import functools
import math

import jax
import jax.numpy as jnp
import numpy as np
from jax import lax
from jax.experimental import pallas as pl
from jax.experimental.pallas import tpu as pltpu

F32 = jnp.float32
BF16 = jnp.bfloat16
I32 = jnp.int32

HEAD_DIM = 64
KV_A = 2
NUM_BUCKETS = 32
MAX_DISTANCE = 128
WINDOW = 128
Q_BLOCK = 128
N_GROUPS = 4
EXPERTS_PER_GROUP = 8
N_EXPERTS = N_GROUPS * EXPERTS_PER_GROUP
DEPTH = 1
ALPHA = (2.0 * DEPTH) ** 0.25
ATTN_SCALE = 1.0 / math.sqrt(HEAD_DIM)
EPS = 1e-5
NEG_INF = -1e30
LOG2E = math.log2(math.e)

LANES = 128
ROW_SUB = 8
MOE_TM = 256
ROW_UNROLL = 8
SWA_PAIRS = 4
SB_GROUP = 8
SB_TAIL_ROWS = 32
SB_SKIP_BITS = 160.0
VMEM_LIMIT = 48 * 1024 * 1024


def _params(sem, vmem=VMEM_LIMIT):
    return pltpu.CompilerParams(dimension_semantics=sem, vmem_limit_bytes=vmem)


def _store_row_tiles(ref_2d, y):
    n = y.shape[0]
    for s in range(ROW_SUB):
        lo = pltpu.bitcast(y[:, 2 * s * LANES:(2 * s + 1) * LANES].astype(BF16).astype(F32), jnp.uint32)
        hi = pltpu.bitcast(y[:, (2 * s + 1) * LANES:(2 * s + 2) * LANES].astype(BF16).astype(F32), jnp.uint32)
        ref_2d[pl.ds(s, n, stride=ROW_SUB), :] = hi | (lo >> 16)


def _load_row_tiles(ref_2d, n):
    chunks = []
    for s in range(ROW_SUB):
        p = ref_2d[pl.ds(s, n, stride=ROW_SUB), :]
        chunks.append(pltpu.bitcast(p << 16, F32))
        chunks.append(pltpu.bitcast(p & jnp.uint32(0xFFFF0000), F32))
    return chunks


def _adaln_kernel(c_ref, w_ref, b_ref, o_ref):
    c = c_ref[...]
    ca = (c * jax.nn.sigmoid(c)).astype(BF16)
    o_ref[...] = jnp.dot(ca, w_ref[0].astype(BF16), preferred_element_type=F32) + b_ref[...]


def _adaln(c, w_ada, b_ada, tn=1024):
    bsz, d = c.shape
    n = w_ada.shape[-1]
    return pl.pallas_call(
        _adaln_kernel,
        grid=(n // tn,),
        in_specs=[pl.BlockSpec((bsz, d), lambda j: (0, 0)),
                  pl.BlockSpec((1, d, tn), lambda j: (0, 0, j)),
                  pl.BlockSpec((1, tn), lambda j: (0, j))],
        out_specs=pl.BlockSpec((bsz, tn), lambda j: (0, j)),
        out_shape=jax.ShapeDtypeStruct((bsz, n), F32),
        compiler_params=_params(("arbitrary",)),
        name="adaln",
    )(c, w_ada, b_ada)


def _qkv_kernel(x_ref, mod_ref, w_ref, o_ref):
    shift = mod_ref[0, 0:1, :]
    scale = mod_ref[0, 1:2, :]
    h = (x_ref[0] * (1.0 + scale) + shift).astype(BF16)
    o_ref[0] = jnp.dot(h, w_ref[...], preferred_element_type=F32).astype(BF16)


def _qkv(x, mod3, w_in_bf, tm=512, nj=2):
    bsz, s, d = x.shape
    n = w_in_bf.shape[1]
    tn = n // nj
    tm = min(tm, s)
    return pl.pallas_call(
        _qkv_kernel,
        grid=(nj, bsz, s // tm),
        in_specs=[pl.BlockSpec((1, tm, d), lambda j, b, i: (b, i, 0)),
                  pl.BlockSpec((1, 6, d), lambda j, b, i: (b, 0, 0)),
                  pl.BlockSpec((d, tn), lambda j, b, i: (0, j))],
        out_specs=pl.BlockSpec((1, tm, tn), lambda j, b, i: (b, i, j)),
        out_shape=jax.ShapeDtypeStruct((bsz, s, n), BF16),
        compiler_params=_params(("arbitrary", "arbitrary", "arbitrary")),
        name="qkv",
    )(x, mod3, w_in_bf)


def _bucket_map():
    qi = np.arange(WINDOW)[:, None]
    kj = np.arange(2 * WINDOW)[None, :]
    dist = qi + WINDOW - kj
    n = np.maximum(dist, 0)
    max_exact = NUM_BUCKETS // 2
    ratio = np.maximum(n, max_exact).astype(np.float32) / np.float32(max_exact)
    large = max_exact + (np.log(ratio) / np.float32(math.log(MAX_DISTANCE / max_exact))
                         * np.float32(NUM_BUCKETS - max_exact)).astype(np.int32)
    large = np.minimum(large, NUM_BUCKETS - 1)
    bucket = np.where(n < max_exact, n, large)
    band = (dist >= 0) & (dist < WINDOW)
    return np.where(band, bucket, -1).astype(np.int32)


def _swa_bias_kernel(rb_ref, bucket_ref, o_ref):
    first = pl.program_id(0) == 0
    h = pl.program_id(1)
    bucket = bucket_ref[...]
    col = lax.broadcasted_iota(I32, bucket.shape, 1)
    acc = jnp.full(bucket.shape, NEG_INF, F32)
    for b in range(NUM_BUCKETS):
        acc = jnp.where(bucket == b, rb_ref[b, h] * LOG2E, acc)
    o_ref[0, 0] = jnp.where(jnp.logical_and(first, col < WINDOW), NEG_INF, acc)


def _swa_bias(rel_bias):
    nh = rel_bias.shape[1]
    bucket = jnp.asarray(_bucket_map())
    return pl.pallas_call(
        _swa_bias_kernel,
        grid=(2, nh),
        in_specs=[pl.BlockSpec(memory_space=pltpu.SMEM),
                  pl.BlockSpec((WINDOW, 2 * WINDOW), lambda v, h: (0, 0))],
        out_specs=pl.BlockSpec((1, 1, WINDOW, 2 * WINDOW), lambda v, h: (v, h, 0, 0)),
        out_shape=jax.ShapeDtypeStruct((2, nh, WINDOW, 2 * WINDOW), F32),
        compiler_params=_params(("arbitrary", "arbitrary")),
        name="swa_bias",
    )(rel_bias, bucket)


def _swa_kernel(sink_ref, q_ref, kvc_ref, kvp_ref, bias_ref, o_ref, *, n_heads):
    group = n_heads // KV_A
    kv = jnp.concatenate([kvp_ref[0], kvc_ref[0]], axis=0)
    lane = lax.broadcasted_iota(I32, (2 * WINDOW, LANES), 1)
    low = lane < HEAD_DIM

    def halves(pair):
        zero = jnp.zeros_like(pair)
        return [jnp.where(low, pair, zero), jnp.where(low, zero, pair)]

    kz = halves(kv[:, 0:LANES])
    vz = halves(kv[:, LANES:2 * LANES])

    n_pairs = n_heads // KV_A
    for p0 in range(0, n_pairs, SWA_PAIRS):
        pairs = range(p0, min(p0 + SWA_PAIRS, n_pairs))
        heads = [(p, g) for p in pairs for g in range(KV_A)]
        logits, e, den, o = {}, {}, {}, {}
        for p, g in heads:
            qp = q_ref[0, :, p * LANES:(p + 1) * LANES]
            s = lax.dot_general(qp, kz[g], (((1,), (1,)), ((), ())), preferred_element_type=F32)
            logits[p, g] = s + bias_ref[0, g * group + p]
        for p, g in heads:
            sink = sink_ref[g * group + p]
            m = jnp.maximum(jnp.max(logits[p, g], axis=-1, keepdims=True), sink)
            e[p, g] = jnp.exp2(logits[p, g] - m)
            den[p, g] = jnp.sum(e[p, g], axis=-1, keepdims=True) + jnp.exp2(sink - m)
        for p, g in heads:
            o[p, g] = jnp.dot(e[p, g].astype(BF16), vz[g], preferred_element_type=F32)
        for p in pairs:
            acc = o[p, 0] * (1.0 / den[p, 0])
            for g in range(1, KV_A):
                acc = acc + o[p, g] * (1.0 / den[p, g])
            o_ref[0, :, p * LANES:(p + 1) * LANES] = acc.astype(BF16)


def _swa(qkv, sinks, bias, d_a, d_b):
    bsz, s, _ = qkv.shape
    n_heads = d_a // HEAD_DIM
    q_blk = 3 * d_b // d_a
    kv_blk = (3 * d_b + d_a) // (2 * LANES)
    return pl.pallas_call(
        functools.partial(_swa_kernel, n_heads=n_heads),
        grid=(bsz, s // WINDOW),
        in_specs=[pl.BlockSpec(memory_space=pltpu.SMEM),
                  pl.BlockSpec((1, WINDOW, d_a), lambda b, i: (b, i, q_blk)),
                  pl.BlockSpec((1, WINDOW, 2 * LANES), lambda b, i: (b, i, kv_blk)),
                  pl.BlockSpec((1, WINDOW, 2 * LANES),
                               lambda b, i: (b, jnp.maximum(i - 1, 0), kv_blk)),
                  pl.BlockSpec((1, n_heads, WINDOW, 2 * WINDOW),
                               lambda b, i: (jnp.minimum(i, 1), 0, 0, 0))],
        out_specs=pl.BlockSpec((1, WINDOW, d_a), lambda b, i: (b, i, 0)),
        out_shape=jax.ShapeDtypeStruct((bsz, s, d_a), BF16),
        compiler_params=_params(("arbitrary", "arbitrary")),
        name="swa",
    )(sinks, qkv, qkv, qkv, bias)


def _suffix_matrix():
    j = np.arange(Q_BLOCK)[:, None]
    s = np.arange(Q_BLOCK)[None, :]
    return np.concatenate([(j > s), np.ones((Q_BLOCK, Q_BLOCK), bool)], axis=1).astype(np.float32)


def _sb_kernel(q_ref, k_ref, v_ref, lt_ref, o_ref, acc_ref, carry_ref, *, group):
    i = pl.program_id(2)
    lane = lax.broadcasted_iota(I32, (Q_BLOCK, LANES), 1)
    low = lane < HEAD_DIM
    row = lax.broadcasted_iota(I32, (2 * Q_BLOCK, Q_BLOCK), 0)
    col = lax.broadcasted_iota(I32, (2 * Q_BLOCK, Q_BLOCK), 1)
    strict = col < jnp.where(row >= Q_BLOCK, row - Q_BLOCK, row)
    sign = jnp.uint32(0x80000000)

    qh = []
    for g in range(group):
        q = q_ref[0, :, g * LANES:(g + 1) * LANES]
        zero = jnp.zeros_like(q)
        qh.append([jnp.where(low, q, zero), jnp.where(low, zero, q)])

    acc_ref[...] = jnp.zeros_like(acc_ref)
    carry_ref[...] = jnp.zeros_like(carry_ref)

    def block(j, diagonal, rows):
        start = pl.multiple_of(j * Q_BLOCK, Q_BLOCK)
        gs = range(group)
        z = [lax.dot_general(jnp.concatenate([qh[g][0][:rows], qh[g][1][:rows]], axis=0),
                             k_ref[0, pl.ds(start, Q_BLOCK), g * LANES:(g + 1) * LANES],
                             (((1,), (1,)), ((), ())), preferred_element_type=F32) for g in gs]
        sp = []
        for g in gs:
            neg_abs = pltpu.bitcast(pltpu.bitcast(z[g], jnp.uint32) | sign, F32)
            s = jnp.maximum(z[g], 0.0) + jnp.log2(1.0 + jnp.exp2(neg_abs))
            sp.append(jnp.where(strict, s, 0.0) if diagonal else s)
        cs = [jnp.dot(sp[g].astype(BF16), lt_ref[...], preferred_element_type=F32) for g in gs]
        a2, carry_min = [], None
        for g in gs:
            carry = jnp.concatenate([carry_ref[g, 0, :rows], carry_ref[g, 1, :rows]], axis=0)
            a = jnp.exp2(z[g] - sp[g] - cs[g][:, :Q_BLOCK] - carry)
            if diagonal:
                a = jnp.where(strict, a, 0.0)
            a = a.astype(BF16)
            a2.append(jnp.concatenate([a[:rows], a[rows:]], axis=1))
            carry = carry + cs[g][:, Q_BLOCK:]
            carry_ref[g, 0, :rows] = carry[:rows]
            carry_ref[g, 1, :rows] = carry[rows:]
            carry_min = carry if carry_min is None else jnp.minimum(carry_min, carry)
        for g in gs:
            vj = v_ref[0, pl.ds(start, Q_BLOCK), g * LANES:(g + 1) * LANES]
            vzero = jnp.zeros_like(vj)
            vz = jnp.concatenate([jnp.where(low, vj, vzero), jnp.where(low, vzero, vj)], axis=0)
            acc_ref[g, :rows] += jnp.dot(a2[g], vz, preferred_element_type=F32)
        t = SB_TAIL_ROWS
        top = jnp.min(jnp.minimum(carry_min[:t], carry_min[rows:rows + t]))
        rest = None if rows == t else jnp.min(jnp.minimum(carry_min[t:rows], carry_min[rows + t:]))
        return top, rest

    block(i, True, Q_BLOCK)

    def more(state):
        jj, top, rest = state
        return jnp.logical_and(jj < i, jnp.minimum(top, rest) < SB_SKIP_BITS)

    def body(state):
        jj, _, rest = state
        j = i - 1 - jj

        def tail_rows():
            return block(j, False, SB_TAIL_ROWS)[0], rest

        def all_rows():
            return block(j, False, Q_BLOCK)

        top, rest = lax.cond(rest >= SB_SKIP_BITS, tail_rows, all_rows)
        return jj + 1, top, rest

    lax.while_loop(more, body, (jnp.int32(0), jnp.float32(0.0), jnp.float32(0.0)))
    for g in range(group):
        o_ref[0, :, g * LANES:(g + 1) * LANES] = acc_ref[g].astype(BF16)


def _sb(qkv, d_b, group=SB_GROUP):
    bsz, s, _ = qkv.shape
    pairs = d_b // LANES
    ng = pairs // group
    w = group * LANES
    lt = jnp.asarray(_suffix_matrix(), BF16)
    return pl.pallas_call(
        functools.partial(_sb_kernel, group=group),
        grid=(bsz, ng, s // Q_BLOCK),
        in_specs=[pl.BlockSpec((1, Q_BLOCK, w), lambda b, p, i: (b, i, p)),
                  pl.BlockSpec((1, s, w), lambda b, p, i: (b, 0, ng + p)),
                  pl.BlockSpec((1, s, w), lambda b, p, i: (b, 0, 2 * ng + p)),
                  pl.BlockSpec((Q_BLOCK, 2 * Q_BLOCK), lambda b, p, i: (0, 0))],
        out_specs=pl.BlockSpec((1, Q_BLOCK, w), lambda b, p, i: (b, i, p)),
        out_shape=jax.ShapeDtypeStruct((bsz, s, d_b), BF16),
        scratch_shapes=[pltpu.VMEM((group, Q_BLOCK, LANES), F32),
                        pltpu.VMEM((group, 2, Q_BLOCK, LANES), F32)],
        compiler_params=_params(("arbitrary", "arbitrary", "arbitrary")),
        name="sb",
    )(qkv, qkv, qkv, lt)


def _layer_norm(y, g, b):
    mu = jnp.mean(y, axis=-1, keepdims=True)
    yc = y - mu
    var = jnp.mean(yc * yc, axis=-1, keepdims=True)
    return yc * lax.rsqrt(var + EPS) * g + b


def _rms(o, g):
    return o * lax.rsqrt(jnp.mean(o * o, axis=-1, keepdims=True) + EPS) * g


def _mix_ln1_kernel(oa_ref, ob_ref, x_ref, mod_ref, na_ref, nb_ref, wo_ref, g_ref, b_ref,
                    wrc_ref, br_ref, x1_ref, h2_ref, lg_ref, *, d_a, parts):
    hm = x_ref.shape[1] // parts
    rows = [pl.ds(p * hm, hm) for p in range(parts)]
    gate1 = mod_ref[0, 2:3, :]
    shift2 = mod_ref[0, 3:4, :]
    scale2 = mod_ref[0, 4:5, :]
    ra = [_rms(oa_ref[0, r, :].astype(F32), na_ref[...]).astype(BF16) for r in rows]
    rb = [_rms(ob_ref[0, r, :].astype(F32), nb_ref[...]).astype(BF16) for r in rows]
    mix = [jnp.dot(ra[p], wo_ref[:d_a, :], preferred_element_type=F32)
           + jnp.dot(rb[p], wo_ref[d_a:, :], preferred_element_type=F32) for p in range(parts)]
    hi, lo = [], []
    for p, r in enumerate(rows):
        x1 = _layer_norm(ALPHA * x_ref[0, r, :] + (1.0 + gate1) * mix[p], g_ref[...], b_ref[...])
        x1_ref[0, r, :] = x1
        h2 = x1 * (1.0 + scale2) + shift2
        _store_row_tiles(h2_ref.at[0, pl.ds(p * hm * ROW_SUB, hm * ROW_SUB)], h2)
        hi.append(h2.astype(BF16))
        lo.append((h2 - hi[p].astype(F32)).astype(BF16))
    for p, r in enumerate(rows):
        both = jnp.dot(hi[p], wrc_ref[...], preferred_element_type=F32)
        lg_ref[0, r, :] = (both[:, :LANES] + both[:, LANES:]
                           + jnp.dot(lo[p], wrc_ref[:, :LANES], preferred_element_type=F32)
                           + br_ref[...])


def _mix_ln1(o_a, o_b, x, mod3, norm_a, norm_b, w_out_bf, ln_g, ln_b, wr_cat, b_r, tm=512, parts=2):
    bsz, s, d = x.shape
    d_a = o_a.shape[-1]
    d_b = o_b.shape[-1]
    tm = min(tm, s)
    row = lambda b, i: (b, i, 0)
    const2 = lambda b, i: (0, 0)
    once = pl.Buffered(1)
    return pl.pallas_call(
        functools.partial(_mix_ln1_kernel, d_a=d_a, parts=parts),
        grid=(bsz, s // tm),
        in_specs=[pl.BlockSpec((1, tm, d_a), row),
                  pl.BlockSpec((1, tm, d_b), row),
                  pl.BlockSpec((1, tm, d), row),
                  pl.BlockSpec((1, 6, d), lambda b, i: (b, 0, 0)),
                  pl.BlockSpec((1, d_a), const2),
                  pl.BlockSpec((1, d_b), const2),
                  pl.BlockSpec((d_a + d_b, d), const2, pipeline_mode=once),
                  pl.BlockSpec((1, d), const2),
                  pl.BlockSpec((1, d), const2),
                  pl.BlockSpec((d, 2 * LANES), const2, pipeline_mode=once),
                  pl.BlockSpec((1, LANES), const2)],
        out_specs=[pl.BlockSpec((1, tm, d), row),
                   pl.BlockSpec((1, tm * ROW_SUB, LANES), row),
                   pl.BlockSpec((1, tm, LANES), row)],
        out_shape=[jax.ShapeDtypeStruct((bsz, s, d), F32),
                   jax.ShapeDtypeStruct((bsz, s * ROW_SUB, LANES), jnp.uint32),
                   jax.ShapeDtypeStruct((bsz, s, LANES), F32)],
        compiler_params=_params(("arbitrary", "arbitrary")),
        name="mix_ln1",
    )(o_a, o_b, x, mod3, norm_a, norm_b, w_out_bf, ln_g, ln_b, wr_cat, b_r)


def _route_kernel(lg_ref, tri_ref, sel_ref, gate_ref, cnt_ref, base_ref):
    step = pl.program_id(0)

    @pl.when(step == 0)
    def _():
        base_ref[...] = jnp.zeros_like(base_ref)

    lg = lg_ref[...]
    tm = lg.shape[0]
    lane = lax.broadcasted_iota(I32, (tm, LANES), 1)
    big = jnp.int32(2 * LANES)
    glog = jnp.where(lane < N_GROUPS, lg, -jnp.inf)
    gmax = jnp.max(glog, axis=-1, keepdims=True)
    g_sel = jnp.min(jnp.where(glog == gmax, lane, big), axis=-1, keepdims=True)
    p_g = 1.0 / jnp.sum(jnp.exp(glog - gmax), axis=-1, keepdims=True)
    lo = N_GROUPS + g_sel * EXPERTS_PER_GROUP
    in_grp = jnp.logical_and(lane >= lo, lane < lo + EXPERTS_PER_GROUP)
    el = jnp.where(in_grp, lg, -jnp.inf)
    v1 = jnp.max(el, axis=-1, keepdims=True)
    i1 = jnp.min(jnp.where(el == v1, lane, big), axis=-1, keepdims=True)
    el2 = jnp.where(lane == i1, -jnp.inf, el)
    v2 = jnp.max(el2, axis=-1, keepdims=True)
    i2 = jnp.min(jnp.where(el2 == v2, lane, big), axis=-1, keepdims=True)
    r = jnp.exp(v2 - v1)
    w1 = 1.0 / (1.0 + r)
    g1 = p_g * w1
    g2 = p_g * (r * w1)
    e1 = i1 - N_GROUPS
    e2 = i2 - N_GROUPS
    oh1 = (lane == e1)
    oh2 = (lane == e2)
    occ = oh1.astype(F32) + oh2.astype(F32)
    before = jnp.dot(tri_ref[...], occ.astype(BF16), preferred_element_type=F32) + base_ref[...]
    r1 = jnp.sum(jnp.where(oh1, before, 0.0), axis=-1, keepdims=True)
    r2 = jnp.sum(jnp.where(oh2, before, 0.0), axis=-1, keepdims=True)
    base_ref[...] += jnp.sum(occ, axis=0, keepdims=True)
    cnt_ref[...] = base_ref[...]
    sel = jnp.where(lane == 0, e1, jnp.where(lane == 1, e2, 0))
    sel = jnp.where(lane == 2, r1.astype(I32), jnp.where(lane == 3, r2.astype(I32), sel))
    sel_ref[...] = sel
    gate_ref[...] = jnp.where(lane == 0, g1, jnp.where(lane == 1, g2, 0.0))


def _route(logits, tm=1024):
    t = logits.shape[0]
    tm = min(tm, t)
    tri = jnp.asarray(np.tril(np.ones((tm, tm), np.float32), -1), BF16)
    return pl.pallas_call(
        _route_kernel,
        grid=(t // tm,),
        in_specs=[pl.BlockSpec((tm, LANES), lambda i: (i, 0)),
                  pl.BlockSpec((tm, tm), lambda i: (0, 0))],
        out_specs=[pl.BlockSpec((tm, LANES), lambda i: (i, 0)),
                   pl.BlockSpec((tm, LANES), lambda i: (i, 0)),
                   pl.BlockSpec((1, LANES), lambda i: (0, 0))],
        out_shape=[jax.ShapeDtypeStruct((t, LANES), I32),
                   jax.ShapeDtypeStruct((t, LANES), F32),
                   jax.ShapeDtypeStruct((1, LANES), F32)],
        scratch_shapes=[pltpu.VMEM((1, LANES), F32)],
        compiler_params=_params(("arbitrary",)),
        name="route",
    )(logits, tri)


def _lane_prefix(x, lane):
    shift = 1
    while shift < LANES:
        x = x + jnp.where(lane >= shift, pltpu.roll(x, shift, axis=1), 0)
        shift *= 2
    return x


def _dest_kernel(sel_ref, cnt_ref, dest_ref, blk_ref, plan_ref, *, n_blk_pad):
    tm = sel_ref.shape[0]
    lane1 = lax.broadcasted_iota(I32, (8, LANES), 1)
    cnt = jnp.broadcast_to(cnt_ref[...].astype(I32), (8, LANES))
    cnt = jnp.where(lane1 < N_EXPERTS, cnt, 0)
    padded = jnp.bitwise_and(cnt + (MOE_TM - 1), -MOE_TM)
    pend = _lane_prefix(padded, lane1)
    pstart = (pend - padded)[0:1, :]
    sel = sel_ref[...]
    lane = lax.broadcasted_iota(I32, (tm, LANES), 1)
    e1 = sel[:, 0:1]
    e2 = sel[:, 1:2]
    d1 = jnp.sum(jnp.where(lane == e1, pstart, 0), axis=-1, keepdims=True) + sel[:, 2:3]
    d2 = jnp.sum(jnp.where(lane == e2, pstart, 0), axis=-1, keepdims=True) + sel[:, 3:4]
    dest_ref[...] = jnp.where(lane == 0, d1, jnp.where(lane == 1, d2, 0)) * ROW_SUB

    @pl.when(pl.program_id(0) == 0)
    def _():
        brow = lax.broadcasted_iota(I32, (n_blk_pad, LANES), 0) * MOE_TM
        blane = lax.broadcasted_iota(I32, (n_blk_pad, LANES), 1)
        ended = jnp.logical_and(blane < N_EXPERTS, pend[0:1, :] <= brow)
        be = jnp.minimum(jnp.sum(ended.astype(I32), axis=-1, keepdims=True), N_EXPERTS - 1)
        blk_ref[...] = jnp.broadcast_to(be, (n_blk_pad, LANES))
        sub = lax.broadcasted_iota(I32, (8, LANES), 0)
        used = jnp.max(pend, axis=-1, keepdims=True) >> (MOE_TM.bit_length() - 1)
        plan_ref[...] = jnp.where(sub == 0, (pend - padded + cnt) * ROW_SUB,
                                  jnp.where(sub == 1, padded - cnt, used))


def _dest(sel, counts, n_blk, tm=1024):
    t = sel.shape[0]
    tm = min(tm, t)
    n_blk_pad = -(-n_blk // 8) * 8
    return pl.pallas_call(
        functools.partial(_dest_kernel, n_blk_pad=n_blk_pad),
        grid=(t // tm,),
        in_specs=[pl.BlockSpec((tm, LANES), lambda i: (i, 0)),
                  pl.BlockSpec((1, LANES), lambda i: (0, 0))],
        out_specs=[pl.BlockSpec((tm, LANES), lambda i: (i, 0)),
                   pl.BlockSpec((n_blk_pad, LANES), lambda i: (0, 0)),
                   pl.BlockSpec((8, LANES), lambda i: (0, 0))],
        out_shape=[jax.ShapeDtypeStruct((t, LANES), I32),
                   jax.ShapeDtypeStruct((n_blk_pad, LANES), I32),
                   jax.ShapeDtypeStruct((8, LANES), I32)],
        compiler_params=_params(("arbitrary",)),
        name="dest",
    )(sel, counts)


def _dispatch_kernel(dest_ref, plan_ref, h_ref, xs_ref, stage, zeros, sems, zsem, *, n_blk):
    tm = h_ref.shape[0] // ROW_SUB
    step = pl.program_id(0)
    last = pl.num_programs(0) - 1
    slot = lax.rem(step, 2)
    base = step * (2 * tm)
    blk_sub = MOE_TM * ROW_SUB

    def fill(wait):
        def run(copy):
            copy.wait() if wait else copy.start()

        def pads(e, carry):
            first = plan_ref[e]
            n = plan_ref[N_EXPERTS + e]
            bit = MOE_TM // 2
            while bit >= 1:
                @pl.when(jnp.bitwise_and(n, bit) != 0)
                def _(bit=bit):
                    done = jnp.bitwise_and(n, -2 * bit)
                    dst = pl.multiple_of(first + done * ROW_SUB, ROW_SUB)
                    run(pltpu.make_async_copy(zeros.at[pl.ds(0, bit * ROW_SUB)],
                                              xs_ref.at[pl.ds(dst, bit * ROW_SUB)], zsem))
                bit //= 2
            return carry

        lax.fori_loop(0, N_EXPERTS, pads, 0)

        def unused(b, carry):
            dst = pl.multiple_of(b * blk_sub, blk_sub)
            run(pltpu.make_async_copy(zeros, xs_ref.at[pl.ds(dst, blk_sub)], zsem))
            return carry

        lax.fori_loop(plan_ref[2 * N_EXPERTS], n_blk, unused, 0)

    @pl.when(step == 0)
    def _():
        zeros[...] = jnp.zeros_like(zeros)
        fill(wait=False)
        fill(wait=True)

    def wait_tile(sl):
        for _ in range(2):
            pltpu.make_async_copy(stage.at[sl], xs_ref.at[pl.ds(0, tm * ROW_SUB)], sems.at[sl]).wait()

    @pl.when(step >= 2)
    def _():
        wait_tile(slot)

    stage[slot] = h_ref[...]

    def start(c, carry):
        for u in range(ROW_UNROLL):
            r = c * ROW_UNROLL + u
            src = stage.at[slot, pl.ds(pl.multiple_of(r * ROW_SUB, ROW_SUB), ROW_SUB)]
            for k in range(2):
                dst = pl.multiple_of(dest_ref[base + 2 * r + k], ROW_SUB)
                pltpu.make_async_copy(src, xs_ref.at[pl.ds(dst, ROW_SUB)], sems.at[slot]).start()
        return carry

    lax.fori_loop(0, tm // ROW_UNROLL, start, 0)

    @pl.when(step == last)
    def _():
        wait_tile(slot)

        @pl.when(step >= 1)
        def _():
            wait_tile(1 - slot)


def _dispatch(dest_flat, plan_flat, h2p, m_pad, tm=256):
    t = h2p.shape[0] // ROW_SUB
    tm = min(tm, t)
    return pl.pallas_call(
        functools.partial(_dispatch_kernel, n_blk=m_pad // MOE_TM),
        grid_spec=pltpu.PrefetchScalarGridSpec(
            num_scalar_prefetch=2,
            grid=(t // tm,),
            in_specs=[pl.BlockSpec((tm * ROW_SUB, LANES), lambda i, dest, plan: (i, 0))],
            out_specs=pl.BlockSpec(memory_space=pl.ANY),
            scratch_shapes=[pltpu.VMEM((2, tm * ROW_SUB, LANES), h2p.dtype),
                            pltpu.VMEM((MOE_TM * ROW_SUB, LANES), h2p.dtype),
                            pltpu.SemaphoreType.DMA((2,)),
                            pltpu.SemaphoreType.DMA(())]),
        out_shape=jax.ShapeDtypeStruct((m_pad * ROW_SUB, LANES), h2p.dtype),
        compiler_params=_params(("arbitrary",)),
        name="dispatch",
    )(dest_flat, plan_flat, h2p)


def _experts_kernel(blk_ref, xs_ref, wg_ref, wu_ref, wd_ref, ys_ref, wg_bf, wu_bf, wd_bf):
    i = pl.program_id(0)
    changed = jnp.logical_or(i == 0, blk_ref[i] != blk_ref[jnp.maximum(i - 1, 0)])

    @pl.when(changed)
    def _():
        wg_bf[...] = wg_ref[0, 0].astype(BF16)
        wu_bf[...] = wu_ref[0, 0].astype(BF16)
        wd_bf[...] = wd_ref[0, 0].astype(BF16)

    in_use = i < blk_ref[pl.num_programs(0)]

    @pl.when(in_use)
    def _():
        xb = jnp.concatenate([c.astype(BF16) for c in _load_row_tiles(xs_ref, MOE_TM)], axis=1)
        g = jnp.dot(xb, wg_bf[...], preferred_element_type=F32)
        u = jnp.dot(xb, wu_bf[...], preferred_element_type=F32)
        hmid = (g * jax.nn.sigmoid(g) * u).astype(BF16)
        _store_row_tiles(ys_ref, jnp.dot(hmid, wd_bf[...], preferred_element_type=F32))

    @pl.when(jnp.logical_not(in_use))
    def _():
        ys_ref[...] = jnp.zeros_like(ys_ref)


def _experts(blk_expert, xs, w_gate, w_up, w_down):
    d, de = w_gate.shape[-2:]
    n_blk = xs.shape[0] // (MOE_TM * ROW_SUB)
    return pl.pallas_call(
        _experts_kernel,
        grid_spec=pltpu.PrefetchScalarGridSpec(
            num_scalar_prefetch=1,
            grid=(n_blk,),
            in_specs=[pl.BlockSpec((MOE_TM * ROW_SUB, LANES), lambda i, blk: (i, 0)),
                      pl.BlockSpec((1, 1, d, de), lambda i, blk: (0, blk[i], 0, 0)),
                      pl.BlockSpec((1, 1, d, de), lambda i, blk: (0, blk[i], 0, 0)),
                      pl.BlockSpec((1, 1, de, d), lambda i, blk: (0, blk[i], 0, 0))],
            out_specs=pl.BlockSpec((MOE_TM * ROW_SUB, LANES), lambda i, blk: (i, 0)),
            scratch_shapes=[pltpu.VMEM((d, de), BF16),
                            pltpu.VMEM((d, de), BF16),
                            pltpu.VMEM((de, d), BF16)]),
        out_shape=jax.ShapeDtypeStruct(xs.shape, jnp.uint32),
        compiler_params=_params(("arbitrary",)),
        name="experts",
    )(blk_expert, xs, w_gate, w_up, w_down)


def _combine_kernel(dest_ref, ys_ref, gate_ref, x1_ref, mod_ref, g_ref, b_ref, o_ref, ybuf, sems):
    tm = x1_ref.shape[1]
    n_steps = pl.num_programs(0) * pl.num_programs(1)
    step = pl.program_id(0) * pl.num_programs(1) + pl.program_id(1)
    slot = lax.rem(step, 2)

    def issue(st, sl):
        base = st * (2 * tm)

        def start(c, carry):
            for u in range(ROW_UNROLL):
                r = c * ROW_UNROLL + u
                row = pl.ds(pl.multiple_of(r * ROW_SUB, ROW_SUB), ROW_SUB)
                for k in range(2):
                    src = pl.multiple_of(dest_ref[base + 2 * r + k], ROW_SUB)
                    pltpu.make_async_copy(ys_ref.at[pl.ds(src, ROW_SUB)],
                                          ybuf.at[sl, k, row], sems.at[sl]).start()
            return carry

        lax.fori_loop(0, tm // ROW_UNROLL, start, 0)

    @pl.when(step == 0)
    def _():
        issue(0, 0)

    @pl.when(step + 1 < n_steps)
    def _():
        issue(step + 1, 1 - slot)

    for k in range(2):
        pltpu.make_async_copy(ys_ref.at[pl.ds(0, tm * ROW_SUB)], ybuf.at[slot, k], sems.at[slot]).wait()

    gates = gate_ref[0]
    g0 = gates[:, 0:1]
    g1 = gates[:, 1:2]
    y0 = _load_row_tiles(ybuf.at[slot, 0], tm)
    y1 = _load_row_tiles(ybuf.at[slot, 1], tm)
    ffn = jnp.concatenate([g0 * a + g1 * b for a, b in zip(y0, y1)], axis=1)
    gate2 = mod_ref[0, 5:6, :]
    o_ref[0] = _layer_norm(ALPHA * x1_ref[0] + (1.0 + gate2) * ffn, g_ref[...], b_ref[...])


def _combine(dest_flat, ys, gates3, x1, mod3, ln_g, ln_b, tm=256):
    bsz, s, d = x1.shape
    tm = min(tm, s)
    return pl.pallas_call(
        _combine_kernel,
        grid_spec=pltpu.PrefetchScalarGridSpec(
            num_scalar_prefetch=1,
            grid=(bsz, s // tm),
            in_specs=[pl.BlockSpec(memory_space=pl.ANY),
                      pl.BlockSpec((1, tm, LANES), lambda b, i, dest: (b, i, 0)),
                      pl.BlockSpec((1, tm, d), lambda b, i, dest: (b, i, 0)),
                      pl.BlockSpec((1, 6, d), lambda b, i, dest: (b, 0, 0)),
                      pl.BlockSpec((1, d), lambda b, i, dest: (0, 0)),
                      pl.BlockSpec((1, d), lambda b, i, dest: (0, 0))],
            out_specs=pl.BlockSpec((1, tm, d), lambda b, i, dest: (b, i, 0)),
            scratch_shapes=[pltpu.VMEM((2, 2, tm * ROW_SUB, LANES), jnp.uint32),
                            pltpu.SemaphoreType.DMA((2,))]),
        out_shape=jax.ShapeDtypeStruct((bsz, s, d), F32),
        compiler_params=_params(("arbitrary", "arbitrary")),
        name="combine",
    )(dest_flat, ys, gates3, x1, mod3, ln_g, ln_b)


def kernel(x, c, w_in, w_out, sinks, rel_bias, norm_a, norm_b, w_ada, b_ada, ln1_g, ln1_b,
           ln2_g, ln2_b, w_grp, b_grp, w_rtr, b_rtr, w_gate, w_up, w_down):
    bsz, s, d = x.shape
    t = bsz * s
    d_a = norm_a.shape[-1]
    d_b = norm_b.shape[-1]

    mod3 = _adaln(c, w_ada, b_ada).reshape(bsz, 6, d)

    assert KV_A * HEAD_DIM == LANES
    kv_w = 2 * KV_A * HEAD_DIM
    group_a = d_a // HEAD_DIM // KV_A
    head_order = [g * group_a + p for p in range(group_a) for g in range(KV_A)]
    perm_a = np.concatenate([np.arange(HEAD_DIM) + HEAD_DIM * h for h in head_order])
    w0 = w_in[0]
    w_in_bf = jnp.concatenate(
        [w0[:, d_a + kv_w:d_a + kv_w + d_b] * (ATTN_SCALE * LOG2E),
         w0[:, d_a + kv_w + d_b:],
         w0[:, :d_a][:, perm_a] * (ATTN_SCALE * LOG2E),
         w0[:, d_a:d_a + kv_w]], axis=1).astype(BF16)
    qkv = _qkv(x, mod3, w_in_bf)
    norm_a = norm_a[:, perm_a]
    w_out_bf = jnp.concatenate([w_out[0][:d_a][perm_a], w_out[0][d_a:]], axis=0).astype(BF16)

    o_a = _swa(qkv, sinks[0] * LOG2E, _swa_bias(rel_bias), d_a, d_b)
    o_b = _sb(qkv, d_b)

    w_r = jnp.concatenate([w_grp[0], w_rtr[0]], axis=1)
    w_r = jnp.pad(w_r, ((0, 0), (0, LANES - w_r.shape[1])))
    b_r = jnp.pad(jnp.concatenate([b_grp[0], b_rtr[0]]), (0, LANES - N_GROUPS - N_EXPERTS))[None, :]
    wr_hi = w_r.astype(BF16)
    wr_lo = (w_r - wr_hi.astype(F32)).astype(BF16)
    assert d == 2 * LANES * ROW_SUB, "row tiles hold 256 * ROW_SUB features"
    x1, h2, logits = _mix_ln1(o_a, o_b, x, mod3, norm_a, norm_b, w_out_bf,
                              ln1_g, ln1_b, jnp.concatenate([wr_hi, wr_lo], axis=1), b_r)

    sel, gates, counts = _route(logits.reshape(t, LANES))
    m_pad = 2 * t + N_EXPERTS * MOE_TM
    n_blk = m_pad // MOE_TM
    dest, blk, plan = _dest(sel, counts, n_blk)
    dest_flat = dest[:, :2].reshape(2 * t)
    blk_expert = jnp.concatenate([blk[:n_blk, 0], plan[2, :1]])
    plan_flat = jnp.concatenate([plan[0, :N_EXPERTS], plan[1, :N_EXPERTS], plan[2, :1]])

    xs = _dispatch(dest_flat, plan_flat, h2.reshape(t * ROW_SUB, LANES), m_pad)
    ys = _experts(blk_expert, xs, w_gate, w_up, w_down)
    return _combine(dest_flat, ys, gates.reshape(bsz, s, LANES), x1, mod3, ln2_g, ln2_b)
```

```python
import functools
import math

import jax
import jax.numpy as jnp
import numpy as np
from jax import lax
from jax.experimental import pallas as pl
from jax.experimental.pallas import tpu as pltpu

F32 = jnp.float32
BF16 = jnp.bfloat16
I32 = jnp.int32

HEAD_DIM = 64
KV_A = 2
NUM_BUCKETS = 32
MAX_DISTANCE = 128
WINDOW = 128
Q_BLOCK = 128
N_GROUPS = 4
EXPERTS_PER_GROUP = 8
N_EXPERTS = N_GROUPS * EXPERTS_PER_GROUP
DEPTH = 1
ALPHA = (2.0 * DEPTH) ** 0.25
ATTN_SCALE = 1.0 / math.sqrt(HEAD_DIM)
EPS = 1e-5
NEG_INF = -1e30
LOG2E = math.log2(math.e)

LANES = 128
ROW_SUB = 8
MOE_TM = 256
ROW_UNROLL = 8
SWA_PAIRS = 4
SB_GROUP = 8
SB_TAIL_ROWS = 32
SB_SKIP_BITS = 160.0
VMEM_LIMIT = 48 * 1024 * 1024


def _params(sem, vmem=VMEM_LIMIT):
    return pltpu.CompilerParams(dimension_semantics=sem, vmem_limit_bytes=vmem)


def _store_row_tiles(ref_2d, y):
    n = y.shape[0]
    for s in range(ROW_SUB):
        lo = pltpu.bitcast(y[:, 2 * s * LANES:(2 * s + 1) * LANES].astype(BF16).astype(F32), jnp.uint32)
        hi = pltpu.bitcast(y[:, (2 * s + 1) * LANES:(2 * s + 2) * LANES].astype(BF16).astype(F32), jnp.uint32)
        ref_2d[pl.ds(s, n, stride=ROW_SUB), :] = hi | (lo >> 16)


def _load_row_tiles(ref_2d, n):
    chunks = []
    for s in range(ROW_SUB):
        p = ref_2d[pl.ds(s, n, stride=ROW_SUB), :]
        chunks.append(pltpu.bitcast(p << 16, F32))
        chunks.append(pltpu.bitcast(p & jnp.uint32(0xFFFF0000), F32))
    return chunks


def _adaln_kernel(c_ref, w_ref, b_ref, o_ref):
    c = c_ref[...]
    ca = (c * jax.nn.sigmoid(c)).astype(BF16)
    o_ref[...] = jnp.dot(ca, w_ref[0].astype(BF16), preferred_element_type=F32) + b_ref[...]


def _adaln(c, w_ada, b_ada, tn=1024):
    bsz, d = c.shape
    n = w_ada.shape[-1]
    return pl.pallas_call(
        _adaln_kernel,
        grid=(n // tn,),
        in_specs=[pl.BlockSpec((bsz, d), lambda j: (0, 0)),
                  pl.BlockSpec((1, d, tn), lambda j: (0, 0, j)),
                  pl.BlockSpec((1, tn), lambda j: (0, j))],
        out_specs=pl.BlockSpec((bsz, tn), lambda j: (0, j)),
        out_shape=jax.ShapeDtypeStruct((bsz, n), F32),
        compiler_params=_params(("arbitrary",)),
        name="adaln",
    )(c, w_ada, b_ada)


def _qkv_kernel(x_ref, mod_ref, w_ref, o_ref):
    shift = mod_ref[0, 0:1, :]
    scale = mod_ref[0, 1:2, :]
    h = (x_ref[0] * (1.0 + scale) + shift).astype(BF16)
    o_ref[0] = jnp.dot(h, w_ref[...], preferred_element_type=F32).astype(BF16)


def _qkv(x, mod3, w_in_bf, tm=512, nj=2):
    bsz, s, d = x.shape
    n = w_in_bf.shape[1]
    tn = n // nj
    tm = min(tm, s)
    return pl.pallas_call(
        _qkv_kernel,
        grid=(nj, bsz, s // tm),
        in_specs=[pl.BlockSpec((1, tm, d), lambda j, b, i: (b, i, 0)),
                  pl.BlockSpec((1, 6, d), lambda j, b, i: (b, 0, 0)),
                  pl.BlockSpec((d, tn), lambda j, b, i: (0, j))],
        out_specs=pl.BlockSpec((1, tm, tn), lambda j, b, i: (b, i, j)),
        out_shape=jax.ShapeDtypeStruct((bsz, s, n), BF16),
        compiler_params=_params(("arbitrary", "arbitrary", "arbitrary")),
        name="qkv",
    )(x, mod3, w_in_bf)


def _bucket_map():
    qi = np.arange(WINDOW)[:, None]
    kj = np.arange(2 * WINDOW)[None, :]
    dist = qi + WINDOW - kj
    n = np.maximum(dist, 0)
    max_exact = NUM_BUCKETS // 2
    ratio = np.maximum(n, max_exact).astype(np.float32) / np.float32(max_exact)
    large = max_exact + (np.log(ratio) / np.float32(math.log(MAX_DISTANCE / max_exact))
                         * np.float32(NUM_BUCKETS - max_exact)).astype(np.int32)
    large = np.minimum(large, NUM_BUCKETS - 1)
    bucket = np.where(n < max_exact, n, large)
    band = (dist >= 0) & (dist < WINDOW)
    return np.where(band, bucket, -1).astype(np.int32)


def _swa_bias_kernel(rb_ref, bucket_ref, o_ref):
    first = pl.program_id(0) == 0
    h = pl.program_id(1)
    bucket = bucket_ref[...]
    col = lax.broadcasted_iota(I32, bucket.shape, 1)
    acc = jnp.full(bucket.shape, NEG_INF, F32)
    for b in range(NUM_BUCKETS):
        acc = jnp.where(bucket == b, rb_ref[b, h] * LOG2E, acc)
    o_ref[0, 0] = jnp.where(jnp.logical_and(first, col < WINDOW), NEG_INF, acc)


def _swa_bias(rel_bias):
    nh = rel_bias.shape[1]
    bucket = jnp.asarray(_bucket_map())
    return pl.pallas_call(
        _swa_bias_kernel,
        grid=(2, nh),
        in_specs=[pl.BlockSpec(memory_space=pltpu.SMEM),
                  pl.BlockSpec((WINDOW, 2 * WINDOW), lambda v, h: (0, 0))],
        out_specs=pl.BlockSpec((1, 1, WINDOW, 2 * WINDOW), lambda v, h: (v, h, 0, 0)),
        out_shape=jax.ShapeDtypeStruct((2, nh, WINDOW, 2 * WINDOW), F32),
        compiler_params=_params(("arbitrary", "arbitrary")),
        name="swa_bias",
    )(rel_bias, bucket)


def _swa_kernel(sink_ref, q_ref, kvc_ref, kvp_ref, bias_ref, o_ref, *, n_heads):
    group = n_heads // KV_A
    kv = jnp.concatenate([kvp_ref[0], kvc_ref[0]], axis=0)
    lane = lax.broadcasted_iota(I32, (2 * WINDOW, LANES), 1)
    low = lane < HEAD_DIM

    def halves(pair):
        zero = jnp.zeros_like(pair)
        return [jnp.where(low, pair, zero), jnp.where(low, zero, pair)]

    kz = halves(kv[:, 0:LANES])
    vz = halves(kv[:, LANES:2 * LANES])

    n_pairs = n_heads // KV_A
    for p0 in range(0, n_pairs, SWA_PAIRS):
        pairs = range(p0, min(p0 + SWA_PAIRS, n_pairs))
        heads = [(p, g) for p in pairs for g in range(KV_A)]
        logits, e, den, o = {}, {}, {}, {}
        for p, g in heads:
            qp = q_ref[0, :, p * LANES:(p + 1) * LANES]
            s = lax.dot_general(qp, kz[g], (((1,), (1,)), ((), ())), preferred_element_type=F32)
            logits[p, g] = s + bias_ref[0, g * group + p]
        for p, g in heads:
            sink = sink_ref[g * group + p]
            m = jnp.maximum(jnp.max(logits[p, g], axis=-1, keepdims=True), sink)
            e[p, g] = jnp.exp2(logits[p, g] - m)
            den[p, g] = jnp.sum(e[p, g], axis=-1, keepdims=True) + jnp.exp2(sink - m)
        for p, g in heads:
            o[p, g] = jnp.dot(e[p, g].astype(BF16), vz[g], preferred_element_type=F32)
        for p in pairs:
            acc = o[p, 0] * (1.0 / den[p, 0])
            for g in range(1, KV_A):
                acc = acc + o[p, g] * (1.0 / den[p, g])
            o_ref[0, :, p * LANES:(p + 1) * LANES] = acc.astype(BF16)


def _swa(qkv, sinks, bias, d_a, d_b):
    bsz, s, _ = qkv.shape
    n_heads = d_a // HEAD_DIM
    q_blk = 3 * d_b // d_a
    kv_blk = (3 * d_b + d_a) // (2 * LANES)
    return pl.pallas_call(
        functools.partial(_swa_kernel, n_heads=n_heads),
        grid=(bsz, s // WINDOW),
        in_specs=[pl.BlockSpec(memory_space=pltpu.SMEM),
                  pl.BlockSpec((1, WINDOW, d_a), lambda b, i: (b, i, q_blk)),
                  pl.BlockSpec((1, WINDOW, 2 * LANES), lambda b, i: (b, i, kv_blk)),
                  pl.BlockSpec((1, WINDOW, 2 * LANES),
                               lambda b, i: (b, jnp.maximum(i - 1, 0), kv_blk)),
                  pl.BlockSpec((1, n_heads, WINDOW, 2 * WINDOW),
                               lambda b, i: (jnp.minimum(i, 1), 0, 0, 0))],
        out_specs=pl.BlockSpec((1, WINDOW, d_a), lambda b, i: (b, i, 0)),
        out_shape=jax.ShapeDtypeStruct((bsz, s, d_a), BF16),
        compiler_params=_params(("arbitrary", "arbitrary")),
        name="swa",
    )(sinks, qkv, qkv, qkv, bias)


def _suffix_matrix():
    j = np.arange(Q_BLOCK)[:, None]
    s = np.arange(Q_BLOCK)[None, :]
    return np.concatenate([(j > s), np.ones((Q_BLOCK, Q_BLOCK), bool)], axis=1).astype(np.float32)


def _sb_kernel(q_ref, k_ref, v_ref, lt_ref, o_ref, acc_ref, carry_ref, *, group):
    i = pl.program_id(2)
    lane = lax.broadcasted_iota(I32, (Q_BLOCK, LANES), 1)
    low = lane < HEAD_DIM
    row = lax.broadcasted_iota(I32, (2 * Q_BLOCK, Q_BLOCK), 0)
    col = lax.broadcasted_iota(I32, (2 * Q_BLOCK, Q_BLOCK), 1)
    strict = col < jnp.where(row >= Q_BLOCK, row - Q_BLOCK, row)
    sign = jnp.uint32(0x80000000)

    qh = []
    for g in range(group):
        q = q_ref[0, :, g * LANES:(g + 1) * LANES]
        zero = jnp.zeros_like(q)
        qh.append([jnp.where(low, q, zero), jnp.where(low, zero, q)])

    gs = range(group)

    def scores(j, g, rows):
        return lax.dot_general(jnp.concatenate([qh[g][0][:rows], qh[g][1][:rows]], axis=0),
                               k_ref[0, pl.ds(pl.multiple_of(j * Q_BLOCK, Q_BLOCK), Q_BLOCK),
                                     g * LANES:(g + 1) * LANES],
                               (((1,), (1,)), ((), ())), preferred_element_type=F32)

    def softplus2(z):
        neg_abs = pltpu.bitcast(pltpu.bitcast(z, jnp.uint32) | sign, F32)
        return jnp.maximum(z, 0.0) + jnp.log2(1.0 + jnp.exp2(neg_abs))

    def suffix(sp):
        return jnp.dot(sp.astype(BF16), lt_ref[...], preferred_element_type=F32)

    def weighted_values(a, j, g, rows):
        a = a.astype(BF16)
        a2 = jnp.concatenate([a[:rows], a[rows:]], axis=1)
        vj = v_ref[0, pl.ds(pl.multiple_of(j * Q_BLOCK, Q_BLOCK), Q_BLOCK), g * LANES:(g + 1) * LANES]
        vzero = jnp.zeros_like(vj)
        vz = jnp.concatenate([jnp.where(low, vj, vzero), jnp.where(low, vzero, vj)], axis=0)
        return jnp.dot(a2, vz, preferred_element_type=F32)

    def carry_mins(carry_min, rows):
        t = SB_TAIL_ROWS
        top = jnp.min(jnp.minimum(carry_min[:t], carry_min[rows:rows + t]))
        rest = None if rows == t else jnp.min(jnp.minimum(carry_min[t:rows], carry_min[rows + t:]))
        return top, rest

    def first_blocks(with_previous):
        js = [i, i - 1] if with_previous else [i]
        z = {(b, g): scores(j, g, Q_BLOCK) for b, j in enumerate(js) for g in gs}
        sp = {}
        for (b, g), zz in z.items():
            s = softplus2(zz)
            sp[b, g] = jnp.where(strict, s, 0.0) if b == 0 else s
        cs = {bg: suffix(s) for bg, s in sp.items()}
        a, carry_min = {}, None
        for g in gs:
            a[0, g] = jnp.where(strict, jnp.exp2(z[0, g] - sp[0, g] - cs[0, g][:, :Q_BLOCK]), 0.0)
            carry = cs[0, g][:, Q_BLOCK:]
            if with_previous:
                a[1, g] = jnp.exp2(z[1, g] - sp[1, g] - cs[1, g][:, :Q_BLOCK] - carry)
                carry = carry + cs[1, g][:, Q_BLOCK:]
            carry_ref[g, 0] = carry[:Q_BLOCK]
            carry_ref[g, 1] = carry[Q_BLOCK:]
            carry_min = carry if carry_min is None else jnp.minimum(carry_min, carry)
        for g in gs:
            acc = weighted_values(a[0, g], js[0], g, Q_BLOCK)
            if with_previous:
                acc = acc + weighted_values(a[1, g], js[1], g, Q_BLOCK)
            acc_ref[g] = acc
        return carry_mins(carry_min, Q_BLOCK)

    def block(j, rows):
        z = [scores(j, g, rows) for g in gs]
        sp = [softplus2(zz) for zz in z]
        cs = [suffix(s) for s in sp]
        a, carry_min = [], None
        for g in gs:
            carry = jnp.concatenate([carry_ref[g, 0, :rows], carry_ref[g, 1, :rows]], axis=0)
            a.append(jnp.exp2(z[g] - sp[g] - cs[g][:, :Q_BLOCK] - carry))
            carry = carry + cs[g][:, Q_BLOCK:]
            carry_ref[g, 0, :rows] = carry[:rows]
            carry_ref[g, 1, :rows] = carry[rows:]
            carry_min = carry if carry_min is None else jnp.minimum(carry_min, carry)
        for g in gs:
            acc_ref[g, :rows] += weighted_values(a[g], j, g, rows)
        return carry_mins(carry_min, rows)

    top0, rest0 = lax.cond(i > 0, lambda: first_blocks(True), lambda: first_blocks(False))

    def more(state):
        jj, top, rest = state
        return jnp.logical_and(jj < i - 1, jnp.minimum(top, rest) < SB_SKIP_BITS)

    def body(state):
        jj, _, rest = state
        j = i - 2 - jj

        def tail_rows():
            return block(j, SB_TAIL_ROWS)[0], rest

        def all_rows():
            return block(j, Q_BLOCK)

        top, rest = lax.cond(rest >= SB_SKIP_BITS, tail_rows, all_rows)
        return jj + 1, top, rest

    lax.while_loop(more, body, (jnp.int32(0), top0, rest0))
    for g in range(group):
        o_ref[0, :, g * LANES:(g + 1) * LANES] = acc_ref[g].astype(BF16)


def _sb(qkv, d_b, group=SB_GROUP):
    bsz, s, _ = qkv.shape
    pairs = d_b // LANES
    ng = pairs // group
    w = group * LANES
    lt = jnp.asarray(_suffix_matrix(), BF16)
    return pl.pallas_call(
        functools.partial(_sb_kernel, group=group),
        grid=(bsz, ng, s // Q_BLOCK),
        in_specs=[pl.BlockSpec((1, Q_BLOCK, w), lambda b, p, i: (b, i, p)),
                  pl.BlockSpec((1, s, w), lambda b, p, i: (b, 0, ng + p)),
                  pl.BlockSpec((1, s, w), lambda b, p, i: (b, 0, 2 * ng + p)),
                  pl.BlockSpec((Q_BLOCK, 2 * Q_BLOCK), lambda b, p, i: (0, 0))],
        out_specs=pl.BlockSpec((1, Q_BLOCK, w), lambda b, p, i: (b, i, p)),
        out_shape=jax.ShapeDtypeStruct((bsz, s, d_b), BF16),
        scratch_shapes=[pltpu.VMEM((group, Q_BLOCK, LANES), F32),
                        pltpu.VMEM((group, 2, Q_BLOCK, LANES), F32)],
        compiler_params=_params(("arbitrary", "arbitrary", "arbitrary")),
        name="sb",
    )(qkv, qkv, qkv, lt)


def _layer_norm(y, g, b):
    mu = jnp.mean(y, axis=-1, keepdims=True)
    yc = y - mu
    var = jnp.mean(yc * yc, axis=-1, keepdims=True)
    return yc * lax.rsqrt(var + EPS) * g + b


def _rms(o, g):
    return o * lax.rsqrt(jnp.mean(o * o, axis=-1, keepdims=True) + EPS) * g


def _mix_ln1_kernel(oa_ref, ob_ref, x_ref, mod_ref, na_ref, nb_ref, wo_ref, g_ref, b_ref,
                    wrc_ref, br_ref, x1_ref, h2_ref, lg_ref, *, d_a, parts):
    hm = x_ref.shape[1] // parts
    rows = [pl.ds(p * hm, hm) for p in range(parts)]
    gate1 = mod_ref[0, 2:3, :]
    shift2 = mod_ref[0, 3:4, :]
    scale2 = mod_ref[0, 4:5, :]
    ra = [_rms(oa_ref[0, r, :].astype(F32), na_ref[...]).astype(BF16) for r in rows]
    rb = [_rms(ob_ref[0, r, :].astype(F32), nb_ref[...]).astype(BF16) for r in rows]
    mix = [jnp.dot(ra[p], wo_ref[:d_a, :], preferred_element_type=F32)
           + jnp.dot(rb[p], wo_ref[d_a:, :], preferred_element_type=F32) for p in range(parts)]
    hi, lo = [], []
    for p, r in enumerate(rows):
        x1 = _layer_norm(ALPHA * x_ref[0, r, :] + (1.0 + gate1) * mix[p], g_ref[...], b_ref[...])
        x1_ref[0, r, :] = x1
        h2 = x1 * (1.0 + scale2) + shift2
        _store_row_tiles(h2_ref.at[0, pl.ds(p * hm * ROW_SUB, hm * ROW_SUB)], h2)
        hi.append(h2.astype(BF16))
        lo.append((h2 - hi[p].astype(F32)).astype(BF16))
    for p, r in enumerate(rows):
        both = jnp.dot(hi[p], wrc_ref[...], preferred_element_type=F32)
        lg_ref[0, r, :] = (both[:, :LANES] + both[:, LANES:]
                           + jnp.dot(lo[p], wrc_ref[:, :LANES], preferred_element_type=F32)
                           + br_ref[...])


def _mix_ln1(o_a, o_b, x, mod3, norm_a, norm_b, w_out_bf, ln_g, ln_b, wr_cat, b_r, tm=512, parts=2):
    bsz, s, d = x.shape
    d_a = o_a.shape[-1]
    d_b = o_b.shape[-1]
    tm = min(tm, s)
    row = lambda b, i: (b, i, 0)
    const2 = lambda b, i: (0, 0)
    once = pl.Buffered(1)
    return pl.pallas_call(
        functools.partial(_mix_ln1_kernel, d_a=d_a, parts=parts),
        grid=(bsz, s // tm),
        in_specs=[pl.BlockSpec((1, tm, d_a), row),
                  pl.BlockSpec((1, tm, d_b), row),
                  pl.BlockSpec((1, tm, d), row),
                  pl.BlockSpec((1, 6, d), lambda b, i: (b, 0, 0)),
                  pl.BlockSpec((1, d_a), const2),
                  pl.BlockSpec((1, d_b), const2),
                  pl.BlockSpec((d_a + d_b, d), const2, pipeline_mode=once),
                  pl.BlockSpec((1, d), const2),
                  pl.BlockSpec((1, d), const2),
                  pl.BlockSpec((d, 2 * LANES), const2, pipeline_mode=once),
                  pl.BlockSpec((1, LANES), const2)],
        out_specs=[pl.BlockSpec((1, tm, d), row),
                   pl.BlockSpec((1, tm * ROW_SUB, LANES), row),
                   pl.BlockSpec((1, tm, LANES), row)],
        out_shape=[jax.ShapeDtypeStruct((bsz, s, d), F32),
                   jax.ShapeDtypeStruct((bsz, s * ROW_SUB, LANES), jnp.uint32),
                   jax.ShapeDtypeStruct((bsz, s, LANES), F32)],
        compiler_params=_params(("arbitrary", "arbitrary")),
        name="mix_ln1",
    )(o_a, o_b, x, mod3, norm_a, norm_b, w_out_bf, ln_g, ln_b, wr_cat, b_r)


def _route_kernel(lg_ref, tri_ref, sel_ref, gate_ref, cnt_ref, base_ref):
    step = pl.program_id(0)

    @pl.when(step == 0)
    def _():
        base_ref[...] = jnp.zeros_like(base_ref)

    lg = lg_ref[...]
    tm = lg.shape[0]
    lane = lax.broadcasted_iota(I32, (tm, LANES), 1)
    big = jnp.int32(2 * LANES)
    glog = jnp.where(lane < N_GROUPS, lg, -jnp.inf)
    gmax = jnp.max(glog, axis=-1, keepdims=True)
    g_sel = jnp.min(jnp.where(glog == gmax, lane, big), axis=-1, keepdims=True)
    p_g = 1.0 / jnp.sum(jnp.exp(glog - gmax), axis=-1, keepdims=True)
    lo = N_GROUPS + g_sel * EXPERTS_PER_GROUP
    in_grp = jnp.logical_and(lane >= lo, lane < lo + EXPERTS_PER_GROUP)
    el = jnp.where(in_grp, lg, -jnp.inf)
    v1 = jnp.max(el, axis=-1, keepdims=True)
    i1 = jnp.min(jnp.where(el == v1, lane, big), axis=-1, keepdims=True)
    el2 = jnp.where(lane == i1, -jnp.inf, el)
    v2 = jnp.max(el2, axis=-1, keepdims=True)
    i2 = jnp.min(jnp.where(el2 == v2, lane, big), axis=-1, keepdims=True)
    r = jnp.exp(v2 - v1)
    w1 = 1.0 / (1.0 + r)
    g1 = p_g * w1
    g2 = p_g * (r * w1)
    e1 = i1 - N_GROUPS
    e2 = i2 - N_GROUPS
    oh1 = (lane == e1)
    oh2 = (lane == e2)
    occ = oh1.astype(F32) + oh2.astype(F32)
    before = jnp.dot(tri_ref[...], occ.astype(BF16), preferred_element_type=F32) + base_ref[...]
    r1 = jnp.sum(jnp.where(oh1, before, 0.0), axis=-1, keepdims=True)
    r2 = jnp.sum(jnp.where(oh2, before, 0.0), axis=-1, keepdims=True)
    base_ref[...] += jnp.sum(occ, axis=0, keepdims=True)
    cnt_ref[...] = base_ref[...]
    sel = jnp.where(lane == 0, e1, jnp.where(lane == 1, e2, 0))
    sel = jnp.where(lane == 2, r1.astype(I32), jnp.where(lane == 3, r2.astype(I32), sel))
    sel_ref[...] = sel
    gate_ref[...] = jnp.where(lane == 0, g1, jnp.where(lane == 1, g2, 0.0))


def _route(logits, tm=1024):
    t = logits.shape[0]
    tm = min(tm, t)
    tri = jnp.asarray(np.tril(np.ones((tm, tm), np.float32), -1), BF16)
    return pl.pallas_call(
        _route_kernel,
        grid=(t // tm,),
        in_specs=[pl.BlockSpec((tm, LANES), lambda i: (i, 0)),
                  pl.BlockSpec((tm, tm), lambda i: (0, 0))],
        out_specs=[pl.BlockSpec((tm, LANES), lambda i: (i, 0)),
                   pl.BlockSpec((tm, LANES), lambda i: (i, 0)),
                   pl.BlockSpec((1, LANES), lambda i: (0, 0))],
        out_shape=[jax.ShapeDtypeStruct((t, LANES), I32),
                   jax.ShapeDtypeStruct((t, LANES), F32),
                   jax.ShapeDtypeStruct((1, LANES), F32)],
        scratch_shapes=[pltpu.VMEM((1, LANES), F32)],
        compiler_params=_params(("arbitrary",)),
        name="route",
    )(logits, tri)


def _lane_prefix(x, lane):
    shift = 1
    while shift < LANES:
        x = x + jnp.where(lane >= shift, pltpu.roll(x, shift, axis=1), 0)
        shift *= 2
    return x


def _dest_kernel(sel_ref, cnt_ref, dest_ref, blk_ref, plan_ref, *, n_blk_pad):
    tm = sel_ref.shape[0]
    lane1 = lax.broadcasted_iota(I32, (8, LANES), 1)
    cnt = jnp.broadcast_to(cnt_ref[...].astype(I32), (8, LANES))
    cnt = jnp.where(lane1 < N_EXPERTS, cnt, 0)
    padded = jnp.bitwise_and(cnt + (MOE_TM - 1), -MOE_TM)
    pend = _lane_prefix(padded, lane1)
    pstart = (pend - padded)[0:1, :]
    sel = sel_ref[...]
    lane = lax.broadcasted_iota(I32, (tm, LANES), 1)
    e1 = sel[:, 0:1]
    e2 = sel[:, 1:2]
    d1 = jnp.sum(jnp.where(lane == e1, pstart, 0), axis=-1, keepdims=True) + sel[:, 2:3]
    d2 = jnp.sum(jnp.where(lane == e2, pstart, 0), axis=-1, keepdims=True) + sel[:, 3:4]
    dest_ref[...] = jnp.where(lane == 0, d1, jnp.where(lane == 1, d2, 0)) * ROW_SUB

    @pl.when(pl.program_id(0) == 0)
    def _():
        brow = lax.broadcasted_iota(I32, (n_blk_pad, LANES), 0) * MOE_TM
        blane = lax.broadcasted_iota(I32, (n_blk_pad, LANES), 1)
        ended = jnp.logical_and(blane < N_EXPERTS, pend[0:1, :] <= brow)
        be = jnp.minimum(jnp.sum(ended.astype(I32), axis=-1, keepdims=True), N_EXPERTS - 1)
        blk_ref[...] = jnp.broadcast_to(be, (n_blk_pad, LANES))
        sub = lax.broadcasted_iota(I32, (8, LANES), 0)
        used = jnp.max(pend, axis=-1, keepdims=True) >> (MOE_TM.bit_length() - 1)
        plan_ref[...] = jnp.where(sub == 0, (pend - padded + cnt) * ROW_SUB,
                                  jnp.where(sub == 1, padded - cnt, used))


def _dest(sel, counts, n_blk, tm=1024):
    t = sel.shape[0]
    tm = min(tm, t)
    n_blk_pad = -(-n_blk // 8) * 8
    return pl.pallas_call(
        functools.partial(_dest_kernel, n_blk_pad=n_blk_pad),
        grid=(t // tm,),
        in_specs=[pl.BlockSpec((tm, LANES), lambda i: (i, 0)),
                  pl.BlockSpec((1, LANES), lambda i: (0, 0))],
        out_specs=[pl.BlockSpec((tm, LANES), lambda i: (i, 0)),
                   pl.BlockSpec((n_blk_pad, LANES), lambda i: (0, 0)),
                   pl.BlockSpec((8, LANES), lambda i: (0, 0))],
        out_shape=[jax.ShapeDtypeStruct((t, LANES), I32),
                   jax.ShapeDtypeStruct((n_blk_pad, LANES), I32),
                   jax.ShapeDtypeStruct((8, LANES), I32)],
        compiler_params=_params(("arbitrary",)),
        name="dest",
    )(sel, counts)


def _dispatch_kernel(dest_ref, plan_ref, h_ref, xs_ref, stage, zeros, sems, zsem, *, n_blk):
    tm = h_ref.shape[0] // ROW_SUB
    step = pl.program_id(0)
    last = pl.num_programs(0) - 1
    slot = lax.rem(step, 2)
    base = step * (2 * tm)
    blk_sub = MOE_TM * ROW_SUB

    def fill(wait):
        def run(copy):
            copy.wait() if wait else copy.start()

        def pads(e, carry):
            first = plan_ref[e]
            n = plan_ref[N_EXPERTS + e]
            bit = MOE_TM // 2
            while bit >= 1:
                @pl.when(jnp.bitwise_and(n, bit) != 0)
                def _(bit=bit):
                    done = jnp.bitwise_and(n, -2 * bit)
                    dst = pl.multiple_of(first + done * ROW_SUB, ROW_SUB)
                    run(pltpu.make_async_copy(zeros.at[pl.ds(0, bit * ROW_SUB)],
                                              xs_ref.at[pl.ds(dst, bit * ROW_SUB)], zsem))
                bit //= 2
            return carry

        lax.fori_loop(0, N_EXPERTS, pads, 0)

        def unused(b, carry):
            dst = pl.multiple_of(b * blk_sub, blk_sub)
            run(pltpu.make_async_copy(zeros, xs_ref.at[pl.ds(dst, blk_sub)], zsem))
            return carry

        lax.fori_loop(plan_ref[2 * N_EXPERTS], n_blk, unused, 0)

    @pl.when(step == 0)
    def _():
        zeros[...] = jnp.zeros_like(zeros)
        fill(wait=False)
        fill(wait=True)

    def wait_tile(sl):
        for _ in range(2):
            pltpu.make_async_copy(stage.at[sl], xs_ref.at[pl.ds(0, tm * ROW_SUB)], sems.at[sl]).wait()

    @pl.when(step >= 2)
    def _():
        wait_tile(slot)

    stage[slot] = h_ref[...]

    def start(c, carry):
        for u in range(ROW_UNROLL):
            r = c * ROW_UNROLL + u
            src = stage.at[slot, pl.ds(pl.multiple_of(r * ROW_SUB, ROW_SUB), ROW_SUB)]
            for k in range(2):
                dst = pl.multiple_of(dest_ref[base + 2 * r + k], ROW_SUB)
                pltpu.make_async_copy(src, xs_ref.at[pl.ds(dst, ROW_SUB)], sems.at[slot]).start()
        return carry

    lax.fori_loop(0, tm // ROW_UNROLL, start, 0)

    @pl.when(step == last)
    def _():
        wait_tile(slot)

        @pl.when(step >= 1)
        def _():
            wait_tile(1 - slot)


def _dispatch(dest_flat, plan_flat, h2p, m_pad, tm=256):
    t = h2p.shape[0] // ROW_SUB
    tm = min(tm, t)
    return pl.pallas_call(
        functools.partial(_dispatch_kernel, n_blk=m_pad // MOE_TM),
        grid_spec=pltpu.PrefetchScalarGridSpec(
            num_scalar_prefetch=2,
            grid=(t // tm,),
            in_specs=[pl.BlockSpec((tm * ROW_SUB, LANES), lambda i, dest, plan: (i, 0))],
            out_specs=pl.BlockSpec(memory_space=pl.ANY),
            scratch_shapes=[pltpu.VMEM((2, tm * ROW_SUB, LANES), h2p.dtype),
                            pltpu.VMEM((MOE_TM * ROW_SUB, LANES), h2p.dtype),
                            pltpu.SemaphoreType.DMA((2,)),
                            pltpu.SemaphoreType.DMA(())]),
        out_shape=jax.ShapeDtypeStruct((m_pad * ROW_SUB, LANES), h2p.dtype),
        compiler_params=_params(("arbitrary",)),
        name="dispatch",
    )(dest_flat, plan_flat, h2p)


def _experts_kernel(blk_ref, xs_ref, wg_hbm, wu_hbm, wd_hbm, ys_ref,
                    wg_f32, wu_f32, wd_f32, wg_bf, wu_bf, wd_bf, slot_ref, sems):
    i = pl.program_id(0)
    n_blk = pl.num_programs(0)
    n_used = blk_ref[n_blk]
    e = blk_ref[i]
    in_use = i < n_used
    first_of_run = jnp.logical_and(
        in_use, jnp.logical_or(i == 0, blk_ref[jnp.maximum(i - 1, 0)] != e))

    def weight_copies(expert, sl):
        return [pltpu.make_async_copy(wg_hbm.at[0, expert], wg_f32.at[sl], sems.at[sl]),
                pltpu.make_async_copy(wu_hbm.at[0, expert], wu_f32.at[sl], sems.at[sl]),
                pltpu.make_async_copy(wd_hbm.at[0, expert], wd_f32.at[sl], sems.at[sl])]

    @pl.when(i == 0)
    def _():
        slot_ref[0] = 0
        for c in weight_copies(e, 0):
            c.start()

    @pl.when(first_of_run)
    def _():
        sl = slot_ref[0]
        for c in weight_copies(e, sl):
            c.wait()
        wg_bf[...] = wg_f32[sl].astype(BF16)
        wu_bf[...] = wu_f32[sl].astype(BF16)
        wd_bf[...] = wd_f32[sl].astype(BF16)
        nxt = lax.while_loop(
            lambda j: jnp.logical_and(j < n_used, blk_ref[jnp.minimum(j, n_blk - 1)] == e),
            lambda j: j + 1, i + 1)

        @pl.when(nxt < n_used)
        def _():
            for c in weight_copies(blk_ref[nxt], 1 - sl):
                c.start()

        slot_ref[0] = 1 - sl

    @pl.when(in_use)
    def _():
        xb = jnp.concatenate([c.astype(BF16) for c in _load_row_tiles(xs_ref, MOE_TM)], axis=1)
        g = jnp.dot(xb, wg_bf[...], preferred_element_type=F32)
        u = jnp.dot(xb, wu_bf[...], preferred_element_type=F32)
        hmid = (g * jax.nn.sigmoid(g) * u).astype(BF16)
        _store_row_tiles(ys_ref, jnp.dot(hmid, wd_bf[...], preferred_element_type=F32))

    @pl.when(jnp.logical_not(in_use))
    def _():
        ys_ref[...] = jnp.zeros_like(ys_ref)


def _experts(blk_expert, xs, w_gate, w_up, w_down):
    d, de = w_gate.shape[-2:]
    n_blk = xs.shape[0] // (MOE_TM * ROW_SUB)
    return pl.pallas_call(
        _experts_kernel,
        grid_spec=pltpu.PrefetchScalarGridSpec(
            num_scalar_prefetch=1,
            grid=(n_blk,),
            in_specs=[pl.BlockSpec((MOE_TM * ROW_SUB, LANES), lambda i, blk: (i, 0)),
                      pl.BlockSpec(memory_space=pl.ANY),
                      pl.BlockSpec(memory_space=pl.ANY),
                      pl.BlockSpec(memory_space=pl.ANY)],
            out_specs=pl.BlockSpec((MOE_TM * ROW_SUB, LANES), lambda i, blk: (i, 0)),
            scratch_shapes=[pltpu.VMEM((2, d, de), F32),
                            pltpu.VMEM((2, d, de), F32),
                            pltpu.VMEM((2, de, d), F32),
                            pltpu.VMEM((d, de), BF16),
                            pltpu.VMEM((d, de), BF16),
                            pltpu.VMEM((de, d), BF16),
                            pltpu.SMEM((1,), I32),
                            pltpu.SemaphoreType.DMA((2,))]),
        out_shape=jax.ShapeDtypeStruct(xs.shape, jnp.uint32),
        compiler_params=_params(("arbitrary",)),
        name="experts",
    )(blk_expert, xs, w_gate, w_up, w_down)


def _combine_kernel(dest_ref, ys_ref, gate_ref, x1_ref, mod_ref, g_ref, b_ref, o_ref, ybuf, sems):
    tm = x1_ref.shape[1]
    n_steps = pl.num_programs(0) * pl.num_programs(1)
    step = pl.program_id(0) * pl.num_programs(1) + pl.program_id(1)
    slot = lax.rem(step, 2)

    def issue(st, sl):
        base = st * (2 * tm)

        def start(c, carry):
            for u in range(ROW_UNROLL):
                r = c * ROW_UNROLL + u
                row = pl.ds(pl.multiple_of(r * ROW_SUB, ROW_SUB), ROW_SUB)
                for k in range(2):
                    src = pl.multiple_of(dest_ref[base + 2 * r + k], ROW_SUB)
                    pltpu.make_async_copy(ys_ref.at[pl.ds(src, ROW_SUB)],
                                          ybuf.at[sl, k, row], sems.at[sl]).start()
            return carry

        lax.fori_loop(0, tm // ROW_UNROLL, start, 0)

    @pl.when(step == 0)
    def _():
        issue(0, 0)

    @pl.when(step + 1 < n_steps)
    def _():
        issue(step + 1, 1 - slot)

    for k in range(2):
        pltpu.make_async_copy(ys_ref.at[pl.ds(0, tm * ROW_SUB)], ybuf.at[slot, k], sems.at[slot]).wait()

    gates = gate_ref[0]
    g0 = gates[:, 0:1]
    g1 = gates[:, 1:2]
    y0 = _load_row_tiles(ybuf.at[slot, 0], tm)
    y1 = _load_row_tiles(ybuf.at[slot, 1], tm)
    ffn = jnp.concatenate([g0 * a + g1 * b for a, b in zip(y0, y1)], axis=1)
    gate2 = mod_ref[0, 5:6, :]
    o_ref[0] = _layer_norm(ALPHA * x1_ref[0] + (1.0 + gate2) * ffn, g_ref[...], b_ref[...])


def _combine(dest_flat, ys, gates3, x1, mod3, ln_g, ln_b, tm=256):
    bsz, s, d = x1.shape
    tm = min(tm, s)
    return pl.pallas_call(
        _combine_kernel,
        grid_spec=pltpu.PrefetchScalarGridSpec(
            num_scalar_prefetch=1,
            grid=(bsz, s // tm),
            in_specs=[pl.BlockSpec(memory_space=pl.ANY),
                      pl.BlockSpec((1, tm, LANES), lambda b, i, dest: (b, i, 0)),
                      pl.BlockSpec((1, tm, d), lambda b, i, dest: (b, i, 0)),
                      pl.BlockSpec((1, 6, d), lambda b, i, dest: (b, 0, 0)),
                      pl.BlockSpec((1, d), lambda b, i, dest: (0, 0)),
                      pl.BlockSpec((1, d), lambda b, i, dest: (0, 0))],
            out_specs=pl.BlockSpec((1, tm, d), lambda b, i, dest: (b, i, 0)),
            scratch_shapes=[pltpu.VMEM((2, 2, tm * ROW_SUB, LANES), jnp.uint32),
                            pltpu.SemaphoreType.DMA((2,))]),
        out_shape=jax.ShapeDtypeStruct((bsz, s, d), F32),
        compiler_params=_params(("arbitrary", "arbitrary")),
        name="combine",
    )(dest_flat, ys, gates3, x1, mod3, ln_g, ln_b)


def kernel(x, c, w_in, w_out, sinks, rel_bias, norm_a, norm_b, w_ada, b_ada, ln1_g, ln1_b,
           ln2_g, ln2_b, w_grp, b_grp, w_rtr, b_rtr, w_gate, w_up, w_down):
    bsz, s, d = x.shape
    t = bsz * s
    d_a = norm_a.shape[-1]
    d_b = norm_b.shape[-1]

    mod3 = _adaln(c, w_ada, b_ada).reshape(bsz, 6, d)

    assert KV_A * HEAD_DIM == LANES
    kv_w = 2 * KV_A * HEAD_DIM
    group_a = d_a // HEAD_DIM // KV_A
    head_order = [g * group_a + p for p in range(group_a) for g in range(KV_A)]
    perm_a = np.concatenate([np.arange(HEAD_DIM) + HEAD_DIM * h for h in head_order])
    w0 = w_in[0]
    w_in_bf = jnp.concatenate(
        [w0[:, d_a + kv_w:d_a + kv_w + d_b] * (ATTN_SCALE * LOG2E),
         w0[:, d_a + kv_w + d_b:],
         w0[:, :d_a][:, perm_a] * (ATTN_SCALE * LOG2E),
         w0[:, d_a:d_a + kv_w]], axis=1).astype(BF16)
    qkv = _qkv(x, mod3, w_in_bf)
    norm_a = norm_a[:, perm_a]
    w_out_bf = jnp.concatenate([w_out[0][:d_a][perm_a], w_out[0][d_a:]], axis=0).astype(BF16)

    o_a = _swa(qkv, sinks[0] * LOG2E, _swa_bias(rel_bias), d_a, d_b)
    o_b = _sb(qkv, d_b)

    w_r = jnp.concatenate([w_grp[0], w_rtr[0]], axis=1)
    w_r = jnp.pad(w_r, ((0, 0), (0, LANES - w_r.shape[1])))
    b_r = jnp.pad(jnp.concatenate([b_grp[0], b_rtr[0]]), (0, LANES - N_GROUPS - N_EXPERTS))[None, :]
    wr_hi = w_r.astype(BF16)
    wr_lo = (w_r - wr_hi.astype(F32)).astype(BF16)
    assert d == 2 * LANES * ROW_SUB, "row tiles hold 256 * ROW_SUB features"
    x1, h2, logits = _mix_ln1(o_a, o_b, x, mod3, norm_a, norm_b, w_out_bf,
                              ln1_g, ln1_b, jnp.concatenate([wr_hi, wr_lo], axis=1), b_r)

    sel, gates, counts = _route(logits.reshape(t, LANES))
    m_pad = 2 * t + N_EXPERTS * MOE_TM
    n_blk = m_pad // MOE_TM
    dest, blk, plan = _dest(sel, counts, n_blk)
    dest_flat = dest[:, :2].reshape(2 * t)
    blk_expert = jnp.concatenate([blk[:n_blk, 0], plan[2, :1]])
    plan_flat = jnp.concatenate([plan[0, :N_EXPERTS], plan[1, :N_EXPERTS], plan[2, :1]])

    xs = _dispatch(dest_flat, plan_flat, h2.reshape(t * ROW_SUB, LANES), m_pad)
    ys = _experts(blk_expert, xs, w_gate, w_up, w_down)
    return _combine(dest_flat, ys, gates.reshape(bsz, s, LANES), x1, mod3, ln2_g, ln2_b)
```

```python
import functools
import math

import jax
import jax.numpy as jnp
import numpy as np
from jax import lax
from jax.experimental import pallas as pl
from jax.experimental.pallas import tpu as pltpu

F32 = jnp.float32
BF16 = jnp.bfloat16
I32 = jnp.int32

HEAD_DIM = 64
KV_A = 2
NUM_BUCKETS = 32
MAX_DISTANCE = 128
WINDOW = 128
Q_BLOCK = 128
N_GROUPS = 4
EXPERTS_PER_GROUP = 8
N_EXPERTS = N_GROUPS * EXPERTS_PER_GROUP
DEPTH = 1
ALPHA = (2.0 * DEPTH) ** 0.25
ATTN_SCALE = 1.0 / math.sqrt(HEAD_DIM)
EPS = 1e-5
NEG_INF = -1e30
LOG2E = math.log2(math.e)

LANES = 128
ROW_SUB = 8
MOE_TM = 256
ROW_UNROLL = 8
SWA_PAIRS = 4
SB_GROUP = 8
SB_TAIL_ROWS = 32
SB_SKIP_BITS = 160.0
VMEM_LIMIT = 48 * 1024 * 1024


def _params(sem, vmem=VMEM_LIMIT):
    return pltpu.CompilerParams(dimension_semantics=sem, vmem_limit_bytes=vmem)


def _store_row_tiles(ref_2d, y):
    n = y.shape[0]
    for s in range(ROW_SUB):
        lo = pltpu.bitcast(y[:, 2 * s * LANES:(2 * s + 1) * LANES].astype(BF16).astype(F32), jnp.uint32)
        hi = pltpu.bitcast(y[:, (2 * s + 1) * LANES:(2 * s + 2) * LANES].astype(BF16).astype(F32), jnp.uint32)
        ref_2d[pl.ds(s, n, stride=ROW_SUB), :] = hi | (lo >> 16)


def _load_row_tiles(ref_2d, n):
    chunks = []
    for s in range(ROW_SUB):
        p = ref_2d[pl.ds(s, n, stride=ROW_SUB), :]
        chunks.append(pltpu.bitcast(p << 16, F32))
        chunks.append(pltpu.bitcast(p & jnp.uint32(0xFFFF0000), F32))
    return chunks


def _adaln_kernel(c_ref, w_ref, b_ref, o_ref):
    c = c_ref[...]
    ca = (c * jax.nn.sigmoid(c)).astype(BF16)
    o_ref[...] = jnp.dot(ca, w_ref[0].astype(BF16), preferred_element_type=F32) + b_ref[...]


def _adaln(c, w_ada, b_ada, tn=1024):
    bsz, d = c.shape
    n = w_ada.shape[-1]
    return pl.pallas_call(
        _adaln_kernel,
        grid=(n // tn,),
        in_specs=[pl.BlockSpec((bsz, d), lambda j: (0, 0)),
                  pl.BlockSpec((1, d, tn), lambda j: (0, 0, j)),
                  pl.BlockSpec((1, tn), lambda j: (0, j))],
        out_specs=pl.BlockSpec((bsz, tn), lambda j: (0, j)),
        out_shape=jax.ShapeDtypeStruct((bsz, n), F32),
        compiler_params=_params(("arbitrary",)),
        name="adaln",
    )(c, w_ada, b_ada)


def _qkv_kernel(x_ref, mod_ref, w_ref, o_ref):
    shift = mod_ref[0, 0:1, :]
    scale = mod_ref[0, 1:2, :]
    h = (x_ref[0] * (1.0 + scale) + shift).astype(BF16)
    o_ref[0] = jnp.dot(h, w_ref[...], preferred_element_type=F32).astype(BF16)


def _qkv(x, mod3, w_in_bf, tm=512, nj=2):
    bsz, s, d = x.shape
    n = w_in_bf.shape[1]
    tn = n // nj
    tm = min(tm, s)
    return pl.pallas_call(
        _qkv_kernel,
        grid=(nj, bsz, s // tm),
        in_specs=[pl.BlockSpec((1, tm, d), lambda j, b, i: (b, i, 0)),
                  pl.BlockSpec((1, 6, d), lambda j, b, i: (b, 0, 0)),
                  pl.BlockSpec((d, tn), lambda j, b, i: (0, j))],
        out_specs=pl.BlockSpec((1, tm, tn), lambda j, b, i: (b, i, j)),
        out_shape=jax.ShapeDtypeStruct((bsz, s, n), BF16),
        compiler_params=_params(("arbitrary", "arbitrary", "arbitrary")),
        name="qkv",
    )(x, mod3, w_in_bf)


def _bucket_map():
    qi = np.arange(WINDOW)[:, None]
    kj = np.arange(2 * WINDOW)[None, :]
    dist = qi + WINDOW - kj
    n = np.maximum(dist, 0)
    max_exact = NUM_BUCKETS // 2
    ratio = np.maximum(n, max_exact).astype(np.float32) / np.float32(max_exact)
    large = max_exact + (np.log(ratio) / np.float32(math.log(MAX_DISTANCE / max_exact))
                         * np.float32(NUM_BUCKETS - max_exact)).astype(np.int32)
    large = np.minimum(large, NUM_BUCKETS - 1)
    bucket = np.where(n < max_exact, n, large)
    band = (dist >= 0) & (dist < WINDOW)
    return np.where(band, bucket, -1).astype(np.int32)


def _swa_bias_kernel(rb_ref, bucket_ref, o_ref):
    first = pl.program_id(0) == 0
    bucket = bucket_ref[...]
    col = lax.broadcasted_iota(I32, bucket.shape, 1)
    hidden = jnp.logical_and(first, col < WINDOW)
    for h in range(o_ref.shape[1]):
        acc = jnp.full(bucket.shape, NEG_INF, F32)
        for b in range(NUM_BUCKETS):
            acc = jnp.where(bucket == b, rb_ref[b, h] * LOG2E, acc)
        o_ref[0, h] = jnp.where(hidden, NEG_INF, acc)


def _swa_bias(rel_bias):
    nh = rel_bias.shape[1]
    bucket = jnp.asarray(_bucket_map())
    return pl.pallas_call(
        _swa_bias_kernel,
        grid=(2,),
        in_specs=[pl.BlockSpec(memory_space=pltpu.SMEM),
                  pl.BlockSpec((WINDOW, 2 * WINDOW), lambda v: (0, 0))],
        out_specs=pl.BlockSpec((1, nh, WINDOW, 2 * WINDOW), lambda v: (v, 0, 0, 0)),
        out_shape=jax.ShapeDtypeStruct((2, nh, WINDOW, 2 * WINDOW), F32),
        compiler_params=_params(("arbitrary",)),
        name="swa_bias",
    )(rel_bias, bucket)


def _swa_kernel(sink_ref, q_ref, kvc_ref, kvp_ref, bias_ref, o_ref, *, n_heads):
    group = n_heads // KV_A
    kv = jnp.concatenate([kvp_ref[0], kvc_ref[0]], axis=0)
    lane = lax.broadcasted_iota(I32, (2 * WINDOW, LANES), 1)
    low = lane < HEAD_DIM

    def halves(pair):
        zero = jnp.zeros_like(pair)
        return [jnp.where(low, pair, zero), jnp.where(low, zero, pair)]

    kz = halves(kv[:, 0:LANES])
    vz = halves(kv[:, LANES:2 * LANES])

    n_pairs = n_heads // KV_A
    for p0 in range(0, n_pairs, SWA_PAIRS):
        pairs = range(p0, min(p0 + SWA_PAIRS, n_pairs))
        heads = [(p, g) for p in pairs for g in range(KV_A)]
        logits, e, den, o = {}, {}, {}, {}
        for p, g in heads:
            qp = q_ref[0, :, p * LANES:(p + 1) * LANES]
            s = lax.dot_general(qp, kz[g], (((1,), (1,)), ((), ())), preferred_element_type=F32)
            logits[p, g] = s + bias_ref[0, g * group + p]
        for p, g in heads:
            sink = sink_ref[g * group + p]
            m = jnp.maximum(jnp.max(logits[p, g], axis=-1, keepdims=True), sink)
            e[p, g] = jnp.exp2(logits[p, g] - m)
            den[p, g] = jnp.sum(e[p, g], axis=-1, keepdims=True) + jnp.exp2(sink - m)
        for p, g in heads:
            o[p, g] = jnp.dot(e[p, g].astype(BF16), vz[g], preferred_element_type=F32)
        for p in pairs:
            acc = o[p, 0] * (1.0 / den[p, 0])
            for g in range(1, KV_A):
                acc = acc + o[p, g] * (1.0 / den[p, g])
            o_ref[0, :, p * LANES:(p + 1) * LANES] = acc.astype(BF16)


def _swa(qkv, sinks, bias, d_a, d_b):
    bsz, s, _ = qkv.shape
    n_heads = d_a // HEAD_DIM
    q_blk = 3 * d_b // d_a
    kv_blk = (3 * d_b + d_a) // (2 * LANES)
    return pl.pallas_call(
        functools.partial(_swa_kernel, n_heads=n_heads),
        grid=(bsz, s // WINDOW),
        in_specs=[pl.BlockSpec(memory_space=pltpu.SMEM),
                  pl.BlockSpec((1, WINDOW, d_a), lambda b, i: (b, i, q_blk)),
                  pl.BlockSpec((1, WINDOW, 2 * LANES), lambda b, i: (b, i, kv_blk)),
                  pl.BlockSpec((1, WINDOW, 2 * LANES),
                               lambda b, i: (b, jnp.maximum(i - 1, 0), kv_blk)),
                  pl.BlockSpec((1, n_heads, WINDOW, 2 * WINDOW),
                               lambda b, i: (jnp.minimum(i, 1), 0, 0, 0))],
        out_specs=pl.BlockSpec((1, WINDOW, d_a), lambda b, i: (b, i, 0)),
        out_shape=jax.ShapeDtypeStruct((bsz, s, d_a), BF16),
        compiler_params=_params(("arbitrary", "arbitrary")),
        name="swa",
    )(sinks, qkv, qkv, qkv, bias)


def _suffix_matrix():
    j = np.arange(Q_BLOCK)[:, None]
    s = np.arange(Q_BLOCK)[None, :]
    return np.concatenate([(j > s), np.ones((Q_BLOCK, Q_BLOCK), bool)], axis=1).astype(np.float32)


def _sb_kernel(q_ref, k_ref, v_ref, lt_ref, o_ref, acc_ref, carry_ref, *, group):
    i = pl.program_id(2)
    lane = lax.broadcasted_iota(I32, (Q_BLOCK, LANES), 1)
    low = lane < HEAD_DIM
    row = lax.broadcasted_iota(I32, (2 * Q_BLOCK, Q_BLOCK), 0)
    col = lax.broadcasted_iota(I32, (2 * Q_BLOCK, Q_BLOCK), 1)
    strict = col < jnp.where(row >= Q_BLOCK, row - Q_BLOCK, row)
    sign = jnp.uint32(0x80000000)

    qh = []
    for g in range(group):
        q = q_ref[0, :, g * LANES:(g + 1) * LANES]
        zero = jnp.zeros_like(q)
        qh.append([jnp.where(low, q, zero), jnp.where(low, zero, q)])

    gs = range(group)

    def scores(j, g, lo, hi):
        return lax.dot_general(jnp.concatenate([qh[g][0][lo:hi], qh[g][1][lo:hi]], axis=0),
                               k_ref[0, pl.ds(pl.multiple_of(j * Q_BLOCK, Q_BLOCK), Q_BLOCK),
                                     g * LANES:(g + 1) * LANES],
                               (((1,), (1,)), ((), ())), preferred_element_type=F32)

    def softplus2(z):
        neg_abs = pltpu.bitcast(pltpu.bitcast(z, jnp.uint32) | sign, F32)
        return jnp.maximum(z, 0.0) + jnp.log2(1.0 + jnp.exp2(neg_abs))

    def suffix(sp):
        return jnp.dot(sp.astype(BF16), lt_ref[...], preferred_element_type=F32)

    def weighted_values(a, j, g, rows):
        a = a.astype(BF16)
        a2 = jnp.concatenate([a[:rows], a[rows:]], axis=1)
        vj = v_ref[0, pl.ds(pl.multiple_of(j * Q_BLOCK, Q_BLOCK), Q_BLOCK), g * LANES:(g + 1) * LANES]
        vzero = jnp.zeros_like(vj)
        vz = jnp.concatenate([jnp.where(low, vj, vzero), jnp.where(low, vzero, vj)], axis=0)
        return jnp.dot(a2, vz, preferred_element_type=F32)

    t = SB_TAIL_ROWS

    def carry_mins(carry_min, lo, hi):
        n = hi - lo
        n_top = max(min(hi, t) - lo, 0)
        top = jnp.min(jnp.minimum(carry_min[:n_top], carry_min[n:n + n_top])) if n_top else None
        rest = jnp.min(jnp.minimum(carry_min[n_top:n], carry_min[n + n_top:])) if n_top < n else None
        return top, rest

    def first_blocks(n_before):
        pieces = [(i, 0, Q_BLOCK)]
        if n_before >= 1:
            pieces.append((i - 1, 0, Q_BLOCK))
        if n_before >= 2:
            pieces.append((i - 2, 0, t))
        z = {(b, g): scores(j, g, lo, hi) for b, (j, lo, hi) in enumerate(pieces) for g in gs}
        sp = {}
        for (b, g), zz in z.items():
            s = softplus2(zz)
            sp[b, g] = jnp.where(strict, s, 0.0) if b == 0 else s
        cs = {bg: suffix(s) for bg, s in sp.items()}
        a, carry_min, head_min = {}, None, None
        for g in gs:
            a[0, g] = jnp.where(strict, jnp.exp2(z[0, g] - sp[0, g] - cs[0, g][:, :Q_BLOCK]), 0.0)
            carry = cs[0, g][:, Q_BLOCK:]
            if n_before >= 1:
                a[1, g] = jnp.exp2(z[1, g] - sp[1, g] - cs[1, g][:, :Q_BLOCK] - carry)
                carry = carry + cs[1, g][:, Q_BLOCK:]
            carry_ref[g, 0] = carry[:Q_BLOCK]
            carry_ref[g, 1] = carry[Q_BLOCK:]
            carry_min = carry if carry_min is None else jnp.minimum(carry_min, carry)
            if n_before >= 2:
                head = jnp.concatenate([carry[:t], carry[Q_BLOCK:Q_BLOCK + t]], axis=0)
                a[2, g] = jnp.exp2(z[2, g] - sp[2, g] - cs[2, g][:, :Q_BLOCK] - head)
                head = head + cs[2, g][:, Q_BLOCK:]
                carry_ref[g, 0, :t] = head[:t]
                carry_ref[g, 1, :t] = head[t:]
                head_min = head if head_min is None else jnp.minimum(head_min, head)
        for g in gs:
            acc = weighted_values(a[0, g], i, g, Q_BLOCK)
            if n_before >= 1:
                acc = acc + weighted_values(a[1, g], i - 1, g, Q_BLOCK)
            acc_ref[g] = acc
            if n_before >= 2:
                acc_ref[g, :t] += weighted_values(a[2, g], i - 2, g, t)
        top, rest = carry_mins(carry_min, 0, Q_BLOCK)
        if n_before >= 2:
            top = carry_mins(head_min, 0, t)[0]
        return top, rest

    def block(j, lo, hi):
        n = hi - lo
        z = [scores(j, g, lo, hi) for g in gs]
        sp = [softplus2(zz) for zz in z]
        cs = [suffix(s) for s in sp]
        a, carry_min = [], None
        for g in gs:
            carry = jnp.concatenate([carry_ref[g, 0, lo:hi], carry_ref[g, 1, lo:hi]], axis=0)
            a.append(jnp.exp2(z[g] - sp[g] - cs[g][:, :Q_BLOCK] - carry))
            carry = carry + cs[g][:, Q_BLOCK:]
            carry_ref[g, 0, lo:hi] = carry[:n]
            carry_ref[g, 1, lo:hi] = carry[n:]
            carry_min = carry if carry_min is None else jnp.minimum(carry_min, carry)
        for g in gs:
            acc_ref[g, lo:hi] += weighted_values(a[g], j, g, n)
        return carry_mins(carry_min, lo, hi)

    def two_before():
        top, rest = first_blocks(2)
        rest = lax.cond(rest < SB_SKIP_BITS, lambda: block(i - 2, t, Q_BLOCK)[1], lambda: rest)
        return top, rest

    top0, rest0 = lax.cond(
        i >= 2, two_before,
        lambda: lax.cond(i == 1, lambda: first_blocks(1), lambda: first_blocks(0)))

    def more(state):
        jj, top, rest = state
        return jnp.logical_and(jj < i - 2, jnp.minimum(top, rest) < SB_SKIP_BITS)

    def body(state):
        jj, _, rest = state
        j = i - 3 - jj

        def tail_rows():
            return block(j, 0, t)[0], rest

        def all_rows():
            return block(j, 0, Q_BLOCK)

        top, rest = lax.cond(rest >= SB_SKIP_BITS, tail_rows, all_rows)
        return jj + 1, top, rest

    lax.while_loop(more, body, (jnp.int32(0), top0, rest0))
    for g in range(group):
        o_ref[0, :, g * LANES:(g + 1) * LANES] = acc_ref[g].astype(BF16)


def _sb(qkv, d_b, group=SB_GROUP):
    bsz, s, _ = qkv.shape
    pairs = d_b // LANES
    ng = pairs // group
    w = group * LANES
    lt = jnp.asarray(_suffix_matrix(), BF16)
    return pl.pallas_call(
        functools.partial(_sb_kernel, group=group),
        grid=(bsz, ng, s // Q_BLOCK),
        in_specs=[pl.BlockSpec((1, Q_BLOCK, w), lambda b, p, i: (b, i, p)),
                  pl.BlockSpec((1, s, w), lambda b, p, i: (b, 0, ng + p)),
                  pl.BlockSpec((1, s, w), lambda b, p, i: (b, 0, 2 * ng + p)),
                  pl.BlockSpec((Q_BLOCK, 2 * Q_BLOCK), lambda b, p, i: (0, 0))],
        out_specs=pl.BlockSpec((1, Q_BLOCK, w), lambda b, p, i: (b, i, p)),
        out_shape=jax.ShapeDtypeStruct((bsz, s, d_b), BF16),
        scratch_shapes=[pltpu.VMEM((group, Q_BLOCK, LANES), F32),
                        pltpu.VMEM((group, 2, Q_BLOCK, LANES), F32)],
        compiler_params=_params(("arbitrary", "arbitrary", "arbitrary")),
        name="sb",
    )(qkv, qkv, qkv, lt)


def _layer_norm(y, g, b):
    mu = jnp.mean(y, axis=-1, keepdims=True)
    yc = y - mu
    var = jnp.mean(yc * yc, axis=-1, keepdims=True)
    return yc * lax.rsqrt(var + EPS) * g + b


def _rms(o, g):
    return o * lax.rsqrt(jnp.mean(o * o, axis=-1, keepdims=True) + EPS) * g


def _mix_ln1_kernel(oa_ref, ob_ref, x_ref, mod_ref, na_ref, nb_ref, wo_ref, g_ref, b_ref,
                    wrc_ref, br_ref, x1_ref, h2_ref, lg_ref, *, d_a, parts):
    hm = x_ref.shape[1] // parts
    rows = [pl.ds(p * hm, hm) for p in range(parts)]
    gate1 = mod_ref[0, 2:3, :]
    shift2 = mod_ref[0, 3:4, :]
    scale2 = mod_ref[0, 4:5, :]
    ra = [_rms(oa_ref[0, r, :].astype(F32), na_ref[...]).astype(BF16) for r in rows]
    rb = [_rms(ob_ref[0, r, :].astype(F32), nb_ref[...]).astype(BF16) for r in rows]
    mix = [jnp.dot(ra[p], wo_ref[:d_a, :], preferred_element_type=F32)
           + jnp.dot(rb[p], wo_ref[d_a:, :], preferred_element_type=F32) for p in range(parts)]
    hi, lo = [], []
    for p, r in enumerate(rows):
        x1 = _layer_norm(ALPHA * x_ref[0, r, :] + (1.0 + gate1) * mix[p], g_ref[...], b_ref[...])
        x1_ref[0, r, :] = x1
        h2 = x1 * (1.0 + scale2) + shift2
        _store_row_tiles(h2_ref.at[0, pl.ds(p * hm * ROW_SUB, hm * ROW_SUB)], h2)
        hi.append(h2.astype(BF16))
        lo.append((h2 - hi[p].astype(F32)).astype(BF16))
    for p, r in enumerate(rows):
        both = jnp.dot(hi[p], wrc_ref[...], preferred_element_type=F32)
        lg_ref[0, r, :] = (both[:, :LANES] + both[:, LANES:]
                           + jnp.dot(lo[p], wrc_ref[:, :LANES], preferred_element_type=F32)
                           + br_ref[...])


def _mix_ln1(o_a, o_b, x, mod3, norm_a, norm_b, w_out_bf, ln_g, ln_b, wr_cat, b_r, tm=512, parts=2):
    bsz, s, d = x.shape
    d_a = o_a.shape[-1]
    d_b = o_b.shape[-1]
    tm = min(tm, s)
    row = lambda b, i: (b, i, 0)
    const2 = lambda b, i: (0, 0)
    once = pl.Buffered(1)
    return pl.pallas_call(
        functools.partial(_mix_ln1_kernel, d_a=d_a, parts=parts),
        grid=(bsz, s // tm),
        in_specs=[pl.BlockSpec((1, tm, d_a), row),
                  pl.BlockSpec((1, tm, d_b), row),
                  pl.BlockSpec((1, tm, d), row),
                  pl.BlockSpec((1, 6, d), lambda b, i: (b, 0, 0)),
                  pl.BlockSpec((1, d_a), const2),
                  pl.BlockSpec((1, d_b), const2),
                  pl.BlockSpec((d_a + d_b, d), const2, pipeline_mode=once),
                  pl.BlockSpec((1, d), const2),
                  pl.BlockSpec((1, d), const2),
                  pl.BlockSpec((d, 2 * LANES), const2, pipeline_mode=once),
                  pl.BlockSpec((1, LANES), const2)],
        out_specs=[pl.BlockSpec((1, tm, d), row),
                   pl.BlockSpec((1, tm * ROW_SUB, LANES), row),
                   pl.BlockSpec((1, tm, LANES), row)],
        out_shape=[jax.ShapeDtypeStruct((bsz, s, d), F32),
                   jax.ShapeDtypeStruct((bsz, s * ROW_SUB, LANES), jnp.uint32),
                   jax.ShapeDtypeStruct((bsz, s, LANES), F32)],
        compiler_params=_params(("arbitrary", "arbitrary")),
        name="mix_ln1",
    )(o_a, o_b, x, mod3, norm_a, norm_b, w_out_bf, ln_g, ln_b, wr_cat, b_r)


def _route_kernel(lg_ref, tri_ref, sel_ref, gate_ref, cnt_ref, base_ref):
    step = pl.program_id(0)

    @pl.when(step == 0)
    def _():
        base_ref[...] = jnp.zeros_like(base_ref)

    lg = lg_ref[...]
    tm = lg.shape[0]
    lane = lax.broadcasted_iota(I32, (tm, LANES), 1)
    big = jnp.int32(2 * LANES)
    glog = jnp.where(lane < N_GROUPS, lg, -jnp.inf)
    gmax = jnp.max(glog, axis=-1, keepdims=True)
    g_sel = jnp.min(jnp.where(glog == gmax, lane, big), axis=-1, keepdims=True)
    p_g = 1.0 / jnp.sum(jnp.exp(glog - gmax), axis=-1, keepdims=True)
    lo = N_GROUPS + g_sel * EXPERTS_PER_GROUP
    in_grp = jnp.logical_and(lane >= lo, lane < lo + EXPERTS_PER_GROUP)
    el = jnp.where(in_grp, lg, -jnp.inf)
    v1 = jnp.max(el, axis=-1, keepdims=True)
    i1 = jnp.min(jnp.where(el == v1, lane, big), axis=-1, keepdims=True)
    el2 = jnp.where(lane == i1, -jnp.inf, el)
    v2 = jnp.max(el2, axis=-1, keepdims=True)
    i2 = jnp.min(jnp.where(el2 == v2, lane, big), axis=-1, keepdims=True)
    r = jnp.exp(v2 - v1)
    w1 = 1.0 / (1.0 + r)
    g1 = p_g * w1
    g2 = p_g * (r * w1)
    e1 = i1 - N_GROUPS
    e2 = i2 - N_GROUPS
    oh1 = (lane == e1)
    oh2 = (lane == e2)
    occ = oh1.astype(F32) + oh2.astype(F32)
    before = jnp.dot(tri_ref[...], occ.astype(BF16), preferred_element_type=F32) + base_ref[...]
    r1 = jnp.sum(jnp.where(oh1, before, 0.0), axis=-1, keepdims=True)
    r2 = jnp.sum(jnp.where(oh2, before, 0.0), axis=-1, keepdims=True)
    base_ref[...] += jnp.sum(occ, axis=0, keepdims=True)
    cnt_ref[...] = base_ref[...]
    sel = jnp.where(lane == 0, e1, jnp.where(lane == 1, e2, 0))
    sel = jnp.where(lane == 2, r1.astype(I32), jnp.where(lane == 3, r2.astype(I32), sel))
    sel_ref[...] = sel
    gate_ref[...] = jnp.where(lane == 0, g1, jnp.where(lane == 1, g2, 0.0))


def _route(logits, tm=1024):
    t = logits.shape[0]
    tm = min(tm, t)
    tri = jnp.asarray(np.tril(np.ones((tm, tm), np.float32), -1), BF16)
    return pl.pallas_call(
        _route_kernel,
        grid=(t // tm,),
        in_specs=[pl.BlockSpec((tm, LANES), lambda i: (i, 0)),
                  pl.BlockSpec((tm, tm), lambda i: (0, 0))],
        out_specs=[pl.BlockSpec((tm, LANES), lambda i: (i, 0)),
                   pl.BlockSpec((tm, LANES), lambda i: (i, 0)),
                   pl.BlockSpec((1, LANES), lambda i: (0, 0))],
        out_shape=[jax.ShapeDtypeStruct((t, LANES), I32),
                   jax.ShapeDtypeStruct((t, LANES), F32),
                   jax.ShapeDtypeStruct((1, LANES), F32)],
        scratch_shapes=[pltpu.VMEM((1, LANES), F32)],
        compiler_params=_params(("arbitrary",)),
        name="route",
    )(logits, tri)


def _lane_prefix(x, lane):
    shift = 1
    while shift < LANES:
        x = x + jnp.where(lane >= shift, pltpu.roll(x, shift, axis=1), 0)
        shift *= 2
    return x


def _dest_kernel(sel_ref, cnt_ref, dest_ref, blk_ref, plan_ref, *, n_blk_pad):
    tm = sel_ref.shape[0]
    lane1 = lax.broadcasted_iota(I32, (8, LANES), 1)
    cnt = jnp.broadcast_to(cnt_ref[...].astype(I32), (8, LANES))
    cnt = jnp.where(lane1 < N_EXPERTS, cnt, 0)
    padded = jnp.bitwise_and(cnt + (MOE_TM - 1), -MOE_TM)
    pend = _lane_prefix(padded, lane1)
    pstart = (pend - padded)[0:1, :]
    sel = sel_ref[...]
    lane = lax.broadcasted_iota(I32, (tm, LANES), 1)
    e1 = sel[:, 0:1]
    e2 = sel[:, 1:2]
    d1 = jnp.sum(jnp.where(lane == e1, pstart, 0), axis=-1, keepdims=True) + sel[:, 2:3]
    d2 = jnp.sum(jnp.where(lane == e2, pstart, 0), axis=-1, keepdims=True) + sel[:, 3:4]
    dest_ref[...] = jnp.where(lane == 0, d1, jnp.where(lane == 1, d2, 0)) * ROW_SUB

    @pl.when(pl.program_id(0) == 0)
    def _():
        brow = lax.broadcasted_iota(I32, (n_blk_pad, LANES), 0) * MOE_TM
        blane = lax.broadcasted_iota(I32, (n_blk_pad, LANES), 1)
        ended = jnp.logical_and(blane < N_EXPERTS, pend[0:1, :] <= brow)
        be = jnp.minimum(jnp.sum(ended.astype(I32), axis=-1, keepdims=True), N_EXPERTS - 1)
        blk_ref[...] = jnp.broadcast_to(be, (n_blk_pad, LANES))
        sub = lax.broadcasted_iota(I32, (8, LANES), 0)
        used = jnp.max(pend, axis=-1, keepdims=True) >> (MOE_TM.bit_length() - 1)
        plan_ref[...] = jnp.where(sub == 0, (pend - padded + cnt) * ROW_SUB,
                                  jnp.where(sub == 1, padded - cnt, used))


def _dest(sel, counts, n_blk, tm=1024):
    t = sel.shape[0]
    tm = min(tm, t)
    n_blk_pad = -(-n_blk // 8) * 8
    return pl.pallas_call(
        functools.partial(_dest_kernel, n_blk_pad=n_blk_pad),
        grid=(t // tm,),
        in_specs=[pl.BlockSpec((tm, LANES), lambda i: (i, 0)),
                  pl.BlockSpec((1, LANES), lambda i: (0, 0))],
        out_specs=[pl.BlockSpec((tm, LANES), lambda i: (i, 0)),
                   pl.BlockSpec((n_blk_pad, LANES), lambda i: (0, 0)),
                   pl.BlockSpec((8, LANES), lambda i: (0, 0))],
        out_shape=[jax.ShapeDtypeStruct((t, LANES), I32),
                   jax.ShapeDtypeStruct((n_blk_pad, LANES), I32),
                   jax.ShapeDtypeStruct((8, LANES), I32)],
        compiler_params=_params(("arbitrary",)),
        name="dest",
    )(sel, counts)


def _dispatch_kernel(dest_ref, plan_ref, h_ref, xs_ref, stage, zeros, sems, zsem, *, n_blk):
    tm = h_ref.shape[0] // ROW_SUB
    step = pl.program_id(0)
    last = pl.num_programs(0) - 1
    slot = lax.rem(step, 2)
    base = step * (2 * tm)
    blk_sub = MOE_TM * ROW_SUB

    def fill(wait):
        def run(copy):
            copy.wait() if wait else copy.start()

        def pads(e, carry):
            first = plan_ref[e]
            n = plan_ref[N_EXPERTS + e]
            bit = MOE_TM // 2
            while bit >= 1:
                @pl.when(jnp.bitwise_and(n, bit) != 0)
                def _(bit=bit):
                    done = jnp.bitwise_and(n, -2 * bit)
                    dst = pl.multiple_of(first + done * ROW_SUB, ROW_SUB)
                    run(pltpu.make_async_copy(zeros.at[pl.ds(0, bit * ROW_SUB)],
                                              xs_ref.at[pl.ds(dst, bit * ROW_SUB)], zsem))
                bit //= 2
            return carry

        lax.fori_loop(0, N_EXPERTS, pads, 0)

        def unused(b, carry):
            dst = pl.multiple_of(b * blk_sub, blk_sub)
            run(pltpu.make_async_copy(zeros, xs_ref.at[pl.ds(dst, blk_sub)], zsem))
            return carry

        lax.fori_loop(plan_ref[2 * N_EXPERTS], n_blk, unused, 0)

    @pl.when(step == 0)
    def _():
        zeros[...] = jnp.zeros_like(zeros)
        fill(wait=False)
        fill(wait=True)

    def wait_tile(sl):
        for _ in range(2):
            pltpu.make_async_copy(stage.at[sl], xs_ref.at[pl.ds(0, tm * ROW_SUB)], sems.at[sl]).wait()

    @pl.when(step >= 2)
    def _():
        wait_tile(slot)

    stage[slot] = h_ref[...]

    def start(c, carry):
        for u in range(ROW_UNROLL):
            r = c * ROW_UNROLL + u
            src = stage.at[slot, pl.ds(pl.multiple_of(r * ROW_SUB, ROW_SUB), ROW_SUB)]
            for k in range(2):
                dst = pl.multiple_of(dest_ref[base + 2 * r + k], ROW_SUB)
                pltpu.make_async_copy(src, xs_ref.at[pl.ds(dst, ROW_SUB)], sems.at[slot]).start()
        return carry

    lax.fori_loop(0, tm // ROW_UNROLL, start, 0)

    @pl.when(step == last)
    def _():
        wait_tile(slot)

        @pl.when(step >= 1)
        def _():
            wait_tile(1 - slot)


def _dispatch(dest_flat, plan_flat, h2p, m_pad, tm=256):
    t = h2p.shape[0] // ROW_SUB
    tm = min(tm, t)
    return pl.pallas_call(
        functools.partial(_dispatch_kernel, n_blk=m_pad // MOE_TM),
        grid_spec=pltpu.PrefetchScalarGridSpec(
            num_scalar_prefetch=2,
            grid=(t // tm,),
            in_specs=[pl.BlockSpec((tm * ROW_SUB, LANES), lambda i, dest, plan: (i, 0))],
            out_specs=pl.BlockSpec(memory_space=pl.ANY),
            scratch_shapes=[pltpu.VMEM((2, tm * ROW_SUB, LANES), h2p.dtype),
                            pltpu.VMEM((MOE_TM * ROW_SUB, LANES), h2p.dtype),
                            pltpu.SemaphoreType.DMA((2,)),
                            pltpu.SemaphoreType.DMA(())]),
        out_shape=jax.ShapeDtypeStruct((m_pad * ROW_SUB, LANES), h2p.dtype),
        compiler_params=_params(("arbitrary",)),
        name="dispatch",
    )(dest_flat, plan_flat, h2p)


def _experts_kernel(blk_ref, xs_ref, wg_hbm, wu_hbm, wd_hbm, ys_ref,
                    wg_f32, wu_f32, wd_f32, wg_bf, wu_bf, wd_bf, slot_ref, sems):
    i = pl.program_id(0)
    n_blk = pl.num_programs(0)
    n_used = blk_ref[n_blk]
    e = blk_ref[i]
    in_use = i < n_used
    first_of_run = jnp.logical_and(
        in_use, jnp.logical_or(i == 0, blk_ref[jnp.maximum(i - 1, 0)] != e))

    def weight_copies(expert, sl):
        return [pltpu.make_async_copy(wg_hbm.at[0, expert], wg_f32.at[sl], sems.at[sl]),
                pltpu.make_async_copy(wu_hbm.at[0, expert], wu_f32.at[sl], sems.at[sl]),
                pltpu.make_async_copy(wd_hbm.at[0, expert], wd_f32.at[sl], sems.at[sl])]

    @pl.when(i == 0)
    def _():
        slot_ref[0] = 0
        for c in weight_copies(e, 0):
            c.start()

    @pl.when(first_of_run)
    def _():
        sl = slot_ref[0]
        for c in weight_copies(e, sl):
            c.wait()
        wg_bf[...] = wg_f32[sl].astype(BF16)
        wu_bf[...] = wu_f32[sl].astype(BF16)
        wd_bf[...] = wd_f32[sl].astype(BF16)
        nxt = lax.while_loop(
            lambda j: jnp.logical_and(j < n_used, blk_ref[jnp.minimum(j, n_blk - 1)] == e),
            lambda j: j + 1, i + 1)

        @pl.when(nxt < n_used)
        def _():
            for c in weight_copies(blk_ref[nxt], 1 - sl):
                c.start()

        slot_ref[0] = 1 - sl

    @pl.when(in_use)
    def _():
        xb = jnp.concatenate([c.astype(BF16) for c in _load_row_tiles(xs_ref, MOE_TM)], axis=1)
        g = jnp.dot(xb, wg_bf[...], preferred_element_type=F32)
        u = jnp.dot(xb, wu_bf[...], preferred_element_type=F32)
        hmid = (g * jax.nn.sigmoid(g) * u).astype(BF16)
        _store_row_tiles(ys_ref, jnp.dot(hmid, wd_bf[...], preferred_element_type=F32))

    @pl.when(jnp.logical_not(in_use))
    def _():
        ys_ref[...] = jnp.zeros_like(ys_ref)


def _experts(blk_expert, xs, w_gate, w_up, w_down):
    d, de = w_gate.shape[-2:]
    n_blk = xs.shape[0] // (MOE_TM * ROW_SUB)
    return pl.pallas_call(
        _experts_kernel,
        grid_spec=pltpu.PrefetchScalarGridSpec(
            num_scalar_prefetch=1,
            grid=(n_blk,),
            in_specs=[pl.BlockSpec((MOE_TM * ROW_SUB, LANES), lambda i, blk: (i, 0)),
                      pl.BlockSpec(memory_space=pl.ANY),
                      pl.BlockSpec(memory_space=pl.ANY),
                      pl.BlockSpec(memory_space=pl.ANY)],
            out_specs=pl.BlockSpec((MOE_TM * ROW_SUB, LANES), lambda i, blk: (i, 0)),
            scratch_shapes=[pltpu.VMEM((2, d, de), F32),
                            pltpu.VMEM((2, d, de), F32),
                            pltpu.VMEM((2, de, d), F32),
                            pltpu.VMEM((d, de), BF16),
                            pltpu.VMEM((d, de), BF16),
                            pltpu.VMEM((de, d), BF16),
                            pltpu.SMEM((1,), I32),
                            pltpu.SemaphoreType.DMA((2,))]),
        out_shape=jax.ShapeDtypeStruct(xs.shape, jnp.uint32),
        compiler_params=_params(("arbitrary",)),
        name="experts",
    )(blk_expert, xs, w_gate, w_up, w_down)


def _combine_kernel(dest_ref, ys_ref, gate_ref, x1_ref, mod_ref, g_ref, b_ref, o_ref, ybuf, sems):
    tm = x1_ref.shape[1]
    n_steps = pl.num_programs(0) * pl.num_programs(1)
    step = pl.program_id(0) * pl.num_programs(1) + pl.program_id(1)
    slot = lax.rem(step, 2)

    def issue(st, sl):
        base = st * (2 * tm)

        def start(c, carry):
            for u in range(ROW_UNROLL):
                r = c * ROW_UNROLL + u
                row = pl.ds(pl.multiple_of(r * ROW_SUB, ROW_SUB), ROW_SUB)
                for k in range(2):
                    src = pl.multiple_of(dest_ref[base + 2 * r + k], ROW_SUB)
                    pltpu.make_async_copy(ys_ref.at[pl.ds(src, ROW_SUB)],
                                          ybuf.at[sl, k, row], sems.at[sl]).start()
            return carry

        lax.fori_loop(0, tm // ROW_UNROLL, start, 0)

    @pl.when(step == 0)
    def _():
        issue(0, 0)

    @pl.when(step + 1 < n_steps)
    def _():
        issue(step + 1, 1 - slot)

    for k in range(2):
        pltpu.make_async_copy(ys_ref.at[pl.ds(0, tm * ROW_SUB)], ybuf.at[slot, k], sems.at[slot]).wait()

    gates = gate_ref[0]
    g0 = gates[:, 0:1]
    g1 = gates[:, 1:2]
    y0 = _load_row_tiles(ybuf.at[slot, 0], tm)
    y1 = _load_row_tiles(ybuf.at[slot, 1], tm)
    ffn = jnp.concatenate([g0 * a + g1 * b for a, b in zip(y0, y1)], axis=1)
    gate2 = mod_ref[0, 5:6, :]
    o_ref[0] = _layer_norm(ALPHA * x1_ref[0] + (1.0 + gate2) * ffn, g_ref[...], b_ref[...])


def _combine(dest_flat, ys, gates3, x1, mod3, ln_g, ln_b, tm=256):
    bsz, s, d = x1.shape
    tm = min(tm, s)
    return pl.pallas_call(
        _combine_kernel,
        grid_spec=pltpu.PrefetchScalarGridSpec(
            num_scalar_prefetch=1,
            grid=(bsz, s // tm),
            in_specs=[pl.BlockSpec(memory_space=pl.ANY),
                      pl.BlockSpec((1, tm, LANES), lambda b, i, dest: (b, i, 0)),
                      pl.BlockSpec((1, tm, d), lambda b, i, dest: (b, i, 0)),
                      pl.BlockSpec((1, 6, d), lambda b, i, dest: (b, 0, 0)),
                      pl.BlockSpec((1, d), lambda b, i, dest: (0, 0)),
                      pl.BlockSpec((1, d), lambda b, i, dest: (0, 0))],
            out_specs=pl.BlockSpec((1, tm, d), lambda b, i, dest: (b, i, 0)),
            scratch_shapes=[pltpu.VMEM((2, 2, tm * ROW_SUB, LANES), jnp.uint32),
                            pltpu.SemaphoreType.DMA((2,))]),
        out_shape=jax.ShapeDtypeStruct((bsz, s, d), F32),
        compiler_params=_params(("arbitrary", "arbitrary")),
        name="combine",
    )(dest_flat, ys, gates3, x1, mod3, ln_g, ln_b)


def kernel(x, c, w_in, w_out, sinks, rel_bias, norm_a, norm_b, w_ada, b_ada, ln1_g, ln1_b,
           ln2_g, ln2_b, w_grp, b_grp, w_rtr, b_rtr, w_gate, w_up, w_down):
    bsz, s, d = x.shape
    t = bsz * s
    d_a = norm_a.shape[-1]
    d_b = norm_b.shape[-1]

    mod3 = _adaln(c, w_ada, b_ada).reshape(bsz, 6, d)

    assert KV_A * HEAD_DIM == LANES
    kv_w = 2 * KV_A * HEAD_DIM
    group_a = d_a // HEAD_DIM // KV_A
    head_order = [g * group_a + p for p in range(group_a) for g in range(KV_A)]
    perm_a = np.concatenate([np.arange(HEAD_DIM) + HEAD_DIM * h for h in head_order])
    w0 = w_in[0]
    w_in_bf = jnp.concatenate(
        [w0[:, d_a + kv_w:d_a + kv_w + d_b] * (ATTN_SCALE * LOG2E),
         w0[:, d_a + kv_w + d_b:],
         w0[:, :d_a][:, perm_a] * (ATTN_SCALE * LOG2E),
         w0[:, d_a:d_a + kv_w]], axis=1).astype(BF16)
    qkv = _qkv(x, mod3, w_in_bf)
    norm_a = norm_a[:, perm_a]
    w_out_bf = jnp.concatenate([w_out[0][:d_a][perm_a], w_out[0][d_a:]], axis=0).astype(BF16)

    o_a = _swa(qkv, sinks[0] * LOG2E, _swa_bias(rel_bias), d_a, d_b)
    o_b = _sb(qkv, d_b)

    w_r = jnp.concatenate([w_grp[0], w_rtr[0]], axis=1)
    w_r = jnp.pad(w_r, ((0, 0), (0, LANES - w_r.shape[1])))
    b_r = jnp.pad(jnp.concatenate([b_grp[0], b_rtr[0]]), (0, LANES - N_GROUPS - N_EXPERTS))[None, :]
    wr_hi = w_r.astype(BF16)
    wr_lo = (w_r - wr_hi.astype(F32)).astype(BF16)
    assert d == 2 * LANES * ROW_SUB, "row tiles hold 256 * ROW_SUB features"
    x1, h2, logits = _mix_ln1(o_a, o_b, x, mod3, norm_a, norm_b, w_out_bf,
                              ln1_g, ln1_b, jnp.concatenate([wr_hi, wr_lo], axis=1), b_r)

    sel, gates, counts = _route(logits.reshape(t, LANES))
    m_pad = 2 * t + N_EXPERTS * MOE_TM
    n_blk = m_pad // MOE_TM
    dest, blk, plan = _dest(sel, counts, n_blk)
    dest_flat = dest[:, :2].reshape(2 * t)
    blk_expert = jnp.concatenate([blk[:n_blk, 0], plan[2, :1]])
    plan_flat = jnp.concatenate([plan[0, :N_EXPERTS], plan[1, :N_EXPERTS], plan[2, :1]])

    xs = _dispatch(dest_flat, plan_flat, h2.reshape(t * ROW_SUB, LANES), m_pad)
    ys = _experts(blk_expert, xs, w_gate, w_up, w_down)
    return _combine(dest_flat, ys, gates.reshape(bsz, s, LANES), x1, mod3, ln2_g, ln2_b)
```

```python
import functools
import math

import jax
import jax.numpy as jnp
import numpy as np
from jax import lax
from jax.experimental import pallas as pl
from jax.experimental.pallas import tpu as pltpu

F32 = jnp.float32
BF16 = jnp.bfloat16
I32 = jnp.int32

HEAD_DIM = 64
KV_A = 2
NUM_BUCKETS = 32
MAX_DISTANCE = 128
WINDOW = 128
Q_BLOCK = 128
N_GROUPS = 4
EXPERTS_PER_GROUP = 8
N_EXPERTS = N_GROUPS * EXPERTS_PER_GROUP
DEPTH = 1
ALPHA = (2.0 * DEPTH) ** 0.25
ATTN_SCALE = 1.0 / math.sqrt(HEAD_DIM)
EPS = 1e-5
NEG_INF = -1e30
LOG2E = math.log2(math.e)

LANES = 128
ROW_SUB = 8
MOE_TM = 256
COMBINE_SLOTS = 3
ROW_UNROLL = 8
SWA_PAIRS = 4
SB_GROUP = 8
SB_TAIL_ROWS = 48
SB_SKIP_BITS = 150.0
VMEM_LIMIT = 48 * 1024 * 1024


def _params(sem, vmem=VMEM_LIMIT):
    return pltpu.CompilerParams(dimension_semantics=sem, vmem_limit_bytes=vmem)


def _store_row_tiles(ref_2d, y):
    n = y.shape[0]
    for s in range(ROW_SUB):
        lo = pltpu.bitcast(y[:, 2 * s * LANES:(2 * s + 1) * LANES].astype(BF16).astype(F32), jnp.uint32)
        hi = pltpu.bitcast(y[:, (2 * s + 1) * LANES:(2 * s + 2) * LANES].astype(BF16).astype(F32), jnp.uint32)
        ref_2d[pl.ds(s, n, stride=ROW_SUB), :] = hi | (lo >> 16)


def _load_row_tiles(ref_2d, n):
    chunks = []
    for s in range(ROW_SUB):
        p = ref_2d[pl.ds(s, n, stride=ROW_SUB), :]
        chunks.append(pltpu.bitcast(p << 16, F32))
        chunks.append(pltpu.bitcast(p & jnp.uint32(0xFFFF0000), F32))
    return chunks


def _adaln_kernel(c_ref, w_ref, b_ref, o_ref):
    c = c_ref[...]
    ca = (c * jax.nn.sigmoid(c)).astype(BF16)
    o_ref[...] = jnp.dot(ca, w_ref[0].astype(BF16), preferred_element_type=F32) + b_ref[...]


def _adaln(c, w_ada, b_ada, tn=1024):
    bsz, d = c.shape
    n = w_ada.shape[-1]
    return pl.pallas_call(
        _adaln_kernel,
        grid=(n // tn,),
        in_specs=[pl.BlockSpec((bsz, d), lambda j: (0, 0)),
                  pl.BlockSpec((1, d, tn), lambda j: (0, 0, j)),
                  pl.BlockSpec((1, tn), lambda j: (0, j))],
        out_specs=pl.BlockSpec((bsz, tn), lambda j: (0, j)),
        out_shape=jax.ShapeDtypeStruct((bsz, n), F32),
        compiler_params=_params(("arbitrary",)),
        name="adaln",
    )(c, w_ada, b_ada)


def _qkv_kernel(x_ref, mod_ref, w_ref, o_ref):
    shift = mod_ref[0, 0:1, :]
    scale = mod_ref[0, 1:2, :]
    h = (x_ref[0] * (1.0 + scale) + shift).astype(BF16)
    o_ref[0] = jnp.dot(h, w_ref[...], preferred_element_type=F32).astype(BF16)


def _qkv(x, mod3, w_in_bf, tm=512, nj=2):
    bsz, s, d = x.shape
    n = w_in_bf.shape[1]
    tn = n // nj
    tm = min(tm, s)
    return pl.pallas_call(
        _qkv_kernel,
        grid=(nj, bsz, s // tm),
        in_specs=[pl.BlockSpec((1, tm, d), lambda j, b, i: (b, i, 0)),
                  pl.BlockSpec((1, 6, d), lambda j, b, i: (b, 0, 0)),
                  pl.BlockSpec((d, tn), lambda j, b, i: (0, j))],
        out_specs=pl.BlockSpec((1, tm, tn), lambda j, b, i: (b, i, j)),
        out_shape=jax.ShapeDtypeStruct((bsz, s, n), BF16),
        compiler_params=_params(("arbitrary", "arbitrary", "arbitrary")),
        name="qkv",
    )(x, mod3, w_in_bf)


def _bucket_map():
    qi = np.arange(WINDOW)[:, None]
    kj = np.arange(2 * WINDOW)[None, :]
    dist = qi + WINDOW - kj
    n = np.maximum(dist, 0)
    max_exact = NUM_BUCKETS // 2
    ratio = np.maximum(n, max_exact).astype(np.float32) / np.float32(max_exact)
    large = max_exact + (np.log(ratio) / np.float32(math.log(MAX_DISTANCE / max_exact))
                         * np.float32(NUM_BUCKETS - max_exact)).astype(np.int32)
    large = np.minimum(large, NUM_BUCKETS - 1)
    bucket = np.where(n < max_exact, n, large)
    band = (dist >= 0) & (dist < WINDOW)
    return np.where(band, bucket, -1).astype(np.int32)


def _swa_bias_kernel(rb_ref, bucket_ref, o_ref):
    first = pl.program_id(0) == 0
    bucket = bucket_ref[...]
    col = lax.broadcasted_iota(I32, bucket.shape, 1)
    hidden = jnp.logical_and(first, col < WINDOW)
    for h in range(o_ref.shape[1]):
        acc = jnp.full(bucket.shape, NEG_INF, F32)
        for b in range(NUM_BUCKETS):
            acc = jnp.where(bucket == b, rb_ref[b, h] * LOG2E, acc)
        o_ref[0, h] = jnp.where(hidden, NEG_INF, acc)


def _swa_bias(rel_bias):
    nh = rel_bias.shape[1]
    bucket = jnp.asarray(_bucket_map())
    return pl.pallas_call(
        _swa_bias_kernel,
        grid=(2,),
        in_specs=[pl.BlockSpec(memory_space=pltpu.SMEM),
                  pl.BlockSpec((WINDOW, 2 * WINDOW), lambda v: (0, 0))],
        out_specs=pl.BlockSpec((1, nh, WINDOW, 2 * WINDOW), lambda v: (v, 0, 0, 0)),
        out_shape=jax.ShapeDtypeStruct((2, nh, WINDOW, 2 * WINDOW), F32),
        compiler_params=_params(("arbitrary",)),
        name="swa_bias",
    )(rel_bias, bucket)


def _swa_kernel(sink_ref, q_ref, kvc_ref, kvp_ref, bias_ref, o_ref, *, n_heads):
    group = n_heads // KV_A
    kv = jnp.concatenate([kvp_ref[0], kvc_ref[0]], axis=0)
    lane = lax.broadcasted_iota(I32, (2 * WINDOW, LANES), 1)
    low = lane < HEAD_DIM

    def halves(pair):
        zero = jnp.zeros_like(pair)
        return [jnp.where(low, pair, zero), jnp.where(low, zero, pair)]

    kz = halves(kv[:, 0:LANES])
    vz = halves(kv[:, LANES:2 * LANES])

    n_pairs = n_heads // KV_A
    for p0 in range(0, n_pairs, SWA_PAIRS):
        pairs = range(p0, min(p0 + SWA_PAIRS, n_pairs))
        heads = [(p, g) for p in pairs for g in range(KV_A)]
        logits, e, den, o = {}, {}, {}, {}
        for p, g in heads:
            qp = q_ref[0, :, p * LANES:(p + 1) * LANES]
            s = lax.dot_general(qp, kz[g], (((1,), (1,)), ((), ())), preferred_element_type=F32)
            logits[p, g] = s + bias_ref[0, g * group + p]
        for p, g in heads:
            sink = sink_ref[g * group + p]
            m = jnp.maximum(jnp.max(logits[p, g], axis=-1, keepdims=True), sink)
            e[p, g] = jnp.exp2(logits[p, g] - m)
            den[p, g] = jnp.sum(e[p, g], axis=-1, keepdims=True) + jnp.exp2(sink - m)
        for p, g in heads:
            o[p, g] = jnp.dot(e[p, g].astype(BF16), vz[g], preferred_element_type=F32)
        for p in pairs:
            acc = o[p, 0] * (1.0 / den[p, 0])
            for g in range(1, KV_A):
                acc = acc + o[p, g] * (1.0 / den[p, g])
            o_ref[0, :, p * LANES:(p + 1) * LANES] = acc.astype(BF16)


def _swa(qkv, sinks, bias, d_a, d_b):
    bsz, s, _ = qkv.shape
    n_heads = d_a // HEAD_DIM
    q_blk = 3 * d_b // d_a
    kv_blk = (3 * d_b + d_a) // (2 * LANES)
    return pl.pallas_call(
        functools.partial(_swa_kernel, n_heads=n_heads),
        grid=(bsz, s // WINDOW),
        in_specs=[pl.BlockSpec(memory_space=pltpu.SMEM),
                  pl.BlockSpec((1, WINDOW, d_a), lambda b, i: (b, i, q_blk)),
                  pl.BlockSpec((1, WINDOW, 2 * LANES), lambda b, i: (b, i, kv_blk)),
                  pl.BlockSpec((1, WINDOW, 2 * LANES),
                               lambda b, i: (b, jnp.maximum(i - 1, 0), kv_blk)),
                  pl.BlockSpec((1, n_heads, WINDOW, 2 * WINDOW),
                               lambda b, i: (jnp.minimum(i, 1), 0, 0, 0))],
        out_specs=pl.BlockSpec((1, WINDOW, d_a), lambda b, i: (b, i, 0)),
        out_shape=jax.ShapeDtypeStruct((bsz, s, d_a), BF16),
        compiler_params=_params(("arbitrary", "arbitrary")),
        name="swa",
    )(sinks, qkv, qkv, qkv, bias)


def _suffix_matrix():
    j = np.arange(Q_BLOCK)[:, None]
    s = np.arange(Q_BLOCK)[None, :]
    return np.concatenate([(j > s), np.ones((Q_BLOCK, Q_BLOCK), bool)], axis=1).astype(np.float32)


def _sb_kernel(q_ref, k_ref, v_ref, lt_ref, o_ref, acc_ref, carry_ref, *, group):
    i = pl.program_id(2)
    lane = lax.broadcasted_iota(I32, (Q_BLOCK, LANES), 1)
    low = lane < HEAD_DIM
    row = lax.broadcasted_iota(I32, (2 * Q_BLOCK, Q_BLOCK), 0)
    col = lax.broadcasted_iota(I32, (2 * Q_BLOCK, Q_BLOCK), 1)
    strict = col < jnp.where(row >= Q_BLOCK, row - Q_BLOCK, row)
    sign = jnp.uint32(0x80000000)

    qh = []
    for g in range(group):
        q = q_ref[0, :, g * LANES:(g + 1) * LANES]
        zero = jnp.zeros_like(q)
        qh.append([jnp.where(low, q, zero), jnp.where(low, zero, q)])

    gs = range(group)

    def scores(j, g, lo, hi):
        return lax.dot_general(jnp.concatenate([qh[g][0][lo:hi], qh[g][1][lo:hi]], axis=0),
                               k_ref[0, pl.ds(pl.multiple_of(j * Q_BLOCK, Q_BLOCK), Q_BLOCK),
                                     g * LANES:(g + 1) * LANES],
                               (((1,), (1,)), ((), ())), preferred_element_type=F32)

    def softplus2(z):
        neg_abs = pltpu.bitcast(pltpu.bitcast(z, jnp.uint32) | sign, F32)
        return jnp.maximum(z, 0.0) + jnp.log2(1.0 + jnp.exp2(neg_abs))

    def suffix(sp):
        return jnp.dot(sp.astype(BF16), lt_ref[...], preferred_element_type=F32)

    def weighted_values(a, j, g, rows):
        a = a.astype(BF16)
        a2 = jnp.concatenate([a[:rows], a[rows:]], axis=1)
        vj = v_ref[0, pl.ds(pl.multiple_of(j * Q_BLOCK, Q_BLOCK), Q_BLOCK), g * LANES:(g + 1) * LANES]
        vzero = jnp.zeros_like(vj)
        vz = jnp.concatenate([jnp.where(low, vj, vzero), jnp.where(low, vzero, vj)], axis=0)
        return jnp.dot(a2, vz, preferred_element_type=F32)

    t = SB_TAIL_ROWS

    def carry_mins(carry_min, lo, hi):
        n = hi - lo
        n_top = max(min(hi, t) - lo, 0)
        top = jnp.min(jnp.minimum(carry_min[:n_top], carry_min[n:n + n_top])) if n_top else None
        rest = jnp.min(jnp.minimum(carry_min[n_top:n], carry_min[n + n_top:])) if n_top < n else None
        return top, rest

    def first_blocks(n_before):
        pieces = [(i, 0, Q_BLOCK)]
        if n_before >= 1:
            pieces.append((i - 1, 0, Q_BLOCK))
        if n_before >= 2:
            pieces.append((i - 2, 0, t))
        z = {(b, g): scores(j, g, lo, hi) for b, (j, lo, hi) in enumerate(pieces) for g in gs}
        sp = {}
        for (b, g), zz in z.items():
            s = softplus2(zz)
            sp[b, g] = jnp.where(strict, s, 0.0) if b == 0 else s
        cs = {bg: suffix(s) for bg, s in sp.items()}
        a, carry_min, head_min = {}, None, None
        for g in gs:
            a[0, g] = jnp.where(strict, jnp.exp2(z[0, g] - sp[0, g] - cs[0, g][:, :Q_BLOCK]), 0.0)
            carry = cs[0, g][:, Q_BLOCK:]
            if n_before >= 1:
                a[1, g] = jnp.exp2(z[1, g] - sp[1, g] - cs[1, g][:, :Q_BLOCK] - carry)
                carry = carry + cs[1, g][:, Q_BLOCK:]
            carry_ref[g, 0] = carry[:Q_BLOCK]
            carry_ref[g, 1] = carry[Q_BLOCK:]
            carry_min = carry if carry_min is None else jnp.minimum(carry_min, carry)
            if n_before >= 2:
                head = jnp.concatenate([carry[:t], carry[Q_BLOCK:Q_BLOCK + t]], axis=0)
                a[2, g] = jnp.exp2(z[2, g] - sp[2, g] - cs[2, g][:, :Q_BLOCK] - head)
                head = head + cs[2, g][:, Q_BLOCK:]
                carry_ref[g, 0, :t] = head[:t]
                carry_ref[g, 1, :t] = head[t:]
                head_min = head if head_min is None else jnp.minimum(head_min, head)
        for g in gs:
            acc = weighted_values(a[0, g], i, g, Q_BLOCK)
            if n_before >= 1:
                acc = acc + weighted_values(a[1, g], i - 1, g, Q_BLOCK)
            acc_ref[g] = acc
            if n_before >= 2:
                acc_ref[g, :t] += weighted_values(a[2, g], i - 2, g, t)
        top, rest = carry_mins(carry_min, 0, Q_BLOCK)
        if n_before >= 2:
            top = carry_mins(head_min, 0, t)[0]
        return top, rest

    def block(j, lo, hi):
        n = hi - lo
        z = [scores(j, g, lo, hi) for g in gs]
        sp = [softplus2(zz) for zz in z]
        cs = [suffix(s) for s in sp]
        a, carry_min = [], None
        for g in gs:
            carry = jnp.concatenate([carry_ref[g, 0, lo:hi], carry_ref[g, 1, lo:hi]], axis=0)
            a.append(jnp.exp2(z[g] - sp[g] - cs[g][:, :Q_BLOCK] - carry))
            carry = carry + cs[g][:, Q_BLOCK:]
            carry_ref[g, 0, lo:hi] = carry[:n]
            carry_ref[g, 1, lo:hi] = carry[n:]
            carry_min = carry if carry_min is None else jnp.minimum(carry_min, carry)
        for g in gs:
            acc_ref[g, lo:hi] += weighted_values(a[g], j, g, n)
        return carry_mins(carry_min, lo, hi)

    def two_before():
        top, rest = first_blocks(2)
        rest = lax.cond(rest < SB_SKIP_BITS, lambda: block(i - 2, t, Q_BLOCK)[1], lambda: rest)
        return top, rest

    top0, rest0 = lax.cond(
        i >= 2, two_before,
        lambda: lax.cond(i == 1, lambda: first_blocks(1), lambda: first_blocks(0)))

    def more(state):
        jj, top, rest = state
        return jnp.logical_and(jj < i - 2, jnp.minimum(top, rest) < SB_SKIP_BITS)

    def body(state):
        jj, _, rest = state
        j = i - 3 - jj

        def tail_rows():
            return block(j, 0, t)[0], rest

        def all_rows():
            return block(j, 0, Q_BLOCK)

        top, rest = lax.cond(rest >= SB_SKIP_BITS, tail_rows, all_rows)
        return jj + 1, top, rest

    lax.while_loop(more, body, (jnp.int32(0), top0, rest0))
    for g in range(group):
        o_ref[0, :, g * LANES:(g + 1) * LANES] = acc_ref[g].astype(BF16)


def _sb(qkv, d_b, group=SB_GROUP):
    bsz, s, _ = qkv.shape
    pairs = d_b // LANES
    ng = pairs // group
    w = group * LANES
    lt = jnp.asarray(_suffix_matrix(), BF16)
    return pl.pallas_call(
        functools.partial(_sb_kernel, group=group),
        grid=(bsz, ng, s // Q_BLOCK),
        in_specs=[pl.BlockSpec((1, Q_BLOCK, w), lambda b, p, i: (b, i, p)),
                  pl.BlockSpec((1, s, w), lambda b, p, i: (b, 0, ng + p)),
                  pl.BlockSpec((1, s, w), lambda b, p, i: (b, 0, 2 * ng + p)),
                  pl.BlockSpec((Q_BLOCK, 2 * Q_BLOCK), lambda b, p, i: (0, 0))],
        out_specs=pl.BlockSpec((1, Q_BLOCK, w), lambda b, p, i: (b, i, p)),
        out_shape=jax.ShapeDtypeStruct((bsz, s, d_b), BF16),
        scratch_shapes=[pltpu.VMEM((group, Q_BLOCK, LANES), F32),
                        pltpu.VMEM((group, 2, Q_BLOCK, LANES), F32)],
        compiler_params=_params(("arbitrary", "arbitrary", "arbitrary")),
        name="sb",
    )(qkv, qkv, qkv, lt)


def _layer_norm(y, g, b):
    mu = jnp.mean(y, axis=-1, keepdims=True)
    yc = y - mu
    var = jnp.mean(yc * yc, axis=-1, keepdims=True)
    return yc * lax.rsqrt(var + EPS) * g + b


def _rms(o, g):
    return o * lax.rsqrt(jnp.mean(o * o, axis=-1, keepdims=True) + EPS) * g


def _mix_ln1_kernel(oa_ref, ob_ref, x_ref, mod_ref, na_ref, nb_ref, wo_ref, g_ref, b_ref,
                    wrc_ref, br_ref, x1_ref, h2_ref, lg_ref, *, d_a, parts):
    hm = x_ref.shape[1] // parts
    rows = [pl.ds(p * hm, hm) for p in range(parts)]
    gate1 = mod_ref[0, 2:3, :]
    shift2 = mod_ref[0, 3:4, :]
    scale2 = mod_ref[0, 4:5, :]
    ra = [_rms(oa_ref[0, r, :].astype(F32), na_ref[...]).astype(BF16) for r in rows]
    rb = [_rms(ob_ref[0, r, :].astype(F32), nb_ref[...]).astype(BF16) for r in rows]
    mix = [jnp.dot(ra[p], wo_ref[:d_a, :], preferred_element_type=F32)
           + jnp.dot(rb[p], wo_ref[d_a:, :], preferred_element_type=F32) for p in range(parts)]
    hi, lo = [], []
    for p, r in enumerate(rows):
        x1 = _layer_norm(ALPHA * x_ref[0, r, :] + (1.0 + gate1) * mix[p], g_ref[...], b_ref[...])
        x1_ref[0, r, :] = x1
        h2 = x1 * (1.0 + scale2) + shift2
        _store_row_tiles(h2_ref.at[0, pl.ds(p * hm * ROW_SUB, hm * ROW_SUB)], h2)
        hi.append(h2.astype(BF16))
        lo.append((h2 - hi[p].astype(F32)).astype(BF16))
    for p, r in enumerate(rows):
        both = jnp.dot(hi[p], wrc_ref[...], preferred_element_type=F32)
        lg_ref[0, r, :] = (both[:, :LANES] + both[:, LANES:]
                           + jnp.dot(lo[p], wrc_ref[:, :LANES], preferred_element_type=F32)
                           + br_ref[...])


def _mix_ln1(o_a, o_b, x, mod3, norm_a, norm_b, w_out_bf, ln_g, ln_b, wr_cat, b_r, tm=512, parts=2):
    bsz, s, d = x.shape
    d_a = o_a.shape[-1]
    d_b = o_b.shape[-1]
    tm = min(tm, s)
    row = lambda b, i: (b, i, 0)
    const2 = lambda b, i: (0, 0)
    once = pl.Buffered(1)
    return pl.pallas_call(
        functools.partial(_mix_ln1_kernel, d_a=d_a, parts=parts),
        grid=(bsz, s // tm),
        in_specs=[pl.BlockSpec((1, tm, d_a), row),
                  pl.BlockSpec((1, tm, d_b), row),
                  pl.BlockSpec((1, tm, d), row),
                  pl.BlockSpec((1, 6, d), lambda b, i: (b, 0, 0)),
                  pl.BlockSpec((1, d_a), const2),
                  pl.BlockSpec((1, d_b), const2),
                  pl.BlockSpec((d_a + d_b, d), const2, pipeline_mode=once),
                  pl.BlockSpec((1, d), const2),
                  pl.BlockSpec((1, d), const2),
                  pl.BlockSpec((d, 2 * LANES), const2, pipeline_mode=once),
                  pl.BlockSpec((1, LANES), const2)],
        out_specs=[pl.BlockSpec((1, tm, d), row),
                   pl.BlockSpec((1, tm * ROW_SUB, LANES), row),
                   pl.BlockSpec((1, tm, LANES), row)],
        out_shape=[jax.ShapeDtypeStruct((bsz, s, d), F32),
                   jax.ShapeDtypeStruct((bsz, s * ROW_SUB, LANES), jnp.uint32),
                   jax.ShapeDtypeStruct((bsz, s, LANES), F32)],
        compiler_params=_params(("arbitrary", "arbitrary")),
        name="mix_ln1",
    )(o_a, o_b, x, mod3, norm_a, norm_b, w_out_bf, ln_g, ln_b, wr_cat, b_r)


def _route_kernel(lg_ref, tri_ref, sel_ref, gate_ref, cnt_ref, base_ref):
    step = pl.program_id(0)

    @pl.when(step == 0)
    def _():
        base_ref[...] = jnp.zeros_like(base_ref)

    lg = lg_ref[...]
    tm = lg.shape[0]
    lane = lax.broadcasted_iota(I32, (tm, LANES), 1)
    big = jnp.int32(2 * LANES)
    glog = jnp.where(lane < N_GROUPS, lg, -jnp.inf)
    gmax = jnp.max(glog, axis=-1, keepdims=True)
    g_sel = jnp.min(jnp.where(glog == gmax, lane, big), axis=-1, keepdims=True)
    p_g = 1.0 / jnp.sum(jnp.exp(glog - gmax), axis=-1, keepdims=True)
    lo = N_GROUPS + g_sel * EXPERTS_PER_GROUP
    in_grp = jnp.logical_and(lane >= lo, lane < lo + EXPERTS_PER_GROUP)
    el = jnp.where(in_grp, lg, -jnp.inf)
    v1 = jnp.max(el, axis=-1, keepdims=True)
    i1 = jnp.min(jnp.where(el == v1, lane, big), axis=-1, keepdims=True)
    el2 = jnp.where(lane == i1, -jnp.inf, el)
    v2 = jnp.max(el2, axis=-1, keepdims=True)
    i2 = jnp.min(jnp.where(el2 == v2, lane, big), axis=-1, keepdims=True)
    r = jnp.exp(v2 - v1)
    w1 = 1.0 / (1.0 + r)
    g1 = p_g * w1
    g2 = p_g * (r * w1)
    e1 = i1 - N_GROUPS
    e2 = i2 - N_GROUPS
    oh1 = (lane == e1)
    oh2 = (lane == e2)
    occ = oh1.astype(F32) + oh2.astype(F32)
    before = jnp.dot(tri_ref[...], occ.astype(BF16), preferred_element_type=F32) + base_ref[...]
    r1 = jnp.sum(jnp.where(oh1, before, 0.0), axis=-1, keepdims=True)
    r2 = jnp.sum(jnp.where(oh2, before, 0.0), axis=-1, keepdims=True)
    base_ref[...] += jnp.sum(occ, axis=0, keepdims=True)
    cnt_ref[...] = base_ref[...]
    sel = jnp.where(lane == 0, e1, jnp.where(lane == 1, e2, 0))
    sel = jnp.where(lane == 2, r1.astype(I32), jnp.where(lane == 3, r2.astype(I32), sel))
    sel_ref[...] = sel
    gate_ref[...] = jnp.where(lane == 0, g1, jnp.where(lane == 1, g2, 0.0))


def _route(logits, tm=1024):
    t = logits.shape[0]
    tm = min(tm, t)
    tri = jnp.asarray(np.tril(np.ones((tm, tm), np.float32), -1), BF16)
    return pl.pallas_call(
        _route_kernel,
        grid=(t // tm,),
        in_specs=[pl.BlockSpec((tm, LANES), lambda i: (i, 0)),
                  pl.BlockSpec((tm, tm), lambda i: (0, 0))],
        out_specs=[pl.BlockSpec((tm, LANES), lambda i: (i, 0)),
                   pl.BlockSpec((tm, LANES), lambda i: (i, 0)),
                   pl.BlockSpec((1, LANES), lambda i: (0, 0))],
        out_shape=[jax.ShapeDtypeStruct((t, LANES), I32),
                   jax.ShapeDtypeStruct((t, LANES), F32),
                   jax.ShapeDtypeStruct((1, LANES), F32)],
        scratch_shapes=[pltpu.VMEM((1, LANES), F32)],
        compiler_params=_params(("arbitrary",)),
        name="route",
    )(logits, tri)


def _lane_prefix(x, lane):
    shift = 1
    while shift < LANES:
        x = x + jnp.where(lane >= shift, pltpu.roll(x, shift, axis=1), 0)
        shift *= 2
    return x


def _dest_kernel(sel_ref, cnt_ref, dest_ref, blk_ref, plan_ref, *, n_blk_pad):
    tm = sel_ref.shape[0]
    lane1 = lax.broadcasted_iota(I32, (8, LANES), 1)
    cnt = jnp.broadcast_to(cnt_ref[...].astype(I32), (8, LANES))
    cnt = jnp.where(lane1 < N_EXPERTS, cnt, 0)
    padded = jnp.bitwise_and(cnt + (MOE_TM - 1), -MOE_TM)
    pend = _lane_prefix(padded, lane1)
    pstart = (pend - padded)[0:1, :]
    sel = sel_ref[...]
    lane = lax.broadcasted_iota(I32, (tm, LANES), 1)
    e1 = sel[:, 0:1]
    e2 = sel[:, 1:2]
    d1 = jnp.sum(jnp.where(lane == e1, pstart, 0), axis=-1, keepdims=True) + sel[:, 2:3]
    d2 = jnp.sum(jnp.where(lane == e2, pstart, 0), axis=-1, keepdims=True) + sel[:, 3:4]
    dest_ref[...] = jnp.where(lane == 0, d1, jnp.where(lane == 1, d2, 0)) * ROW_SUB

    @pl.when(pl.program_id(0) == 0)
    def _():
        brow = lax.broadcasted_iota(I32, (n_blk_pad, LANES), 0) * MOE_TM
        blane = lax.broadcasted_iota(I32, (n_blk_pad, LANES), 1)
        ended = jnp.logical_and(blane < N_EXPERTS, pend[0:1, :] <= brow)
        be = jnp.minimum(jnp.sum(ended.astype(I32), axis=-1, keepdims=True), N_EXPERTS - 1)
        blk_ref[...] = jnp.broadcast_to(be, (n_blk_pad, LANES))
        sub = lax.broadcasted_iota(I32, (8, LANES), 0)
        used = jnp.max(pend, axis=-1, keepdims=True) >> (MOE_TM.bit_length() - 1)
        plan_ref[...] = jnp.where(sub == 0, (pend - padded + cnt) * ROW_SUB,
                                  jnp.where(sub == 1, padded - cnt, used))


def _dest(sel, counts, n_blk, tm=1024):
    t = sel.shape[0]
    tm = min(tm, t)
    n_blk_pad = -(-n_blk // 8) * 8
    return pl.pallas_call(
        functools.partial(_dest_kernel, n_blk_pad=n_blk_pad),
        grid=(t // tm,),
        in_specs=[pl.BlockSpec((tm, LANES), lambda i: (i, 0)),
                  pl.BlockSpec((1, LANES), lambda i: (0, 0))],
        out_specs=[pl.BlockSpec((tm, LANES), lambda i: (i, 0)),
                   pl.BlockSpec((n_blk_pad, LANES), lambda i: (0, 0)),
                   pl.BlockSpec((8, LANES), lambda i: (0, 0))],
        out_shape=[jax.ShapeDtypeStruct((t, LANES), I32),
                   jax.ShapeDtypeStruct((n_blk_pad, LANES), I32),
                   jax.ShapeDtypeStruct((8, LANES), I32)],
        compiler_params=_params(("arbitrary",)),
        name="dest",
    )(sel, counts)


def _dispatch_kernel(dest_ref, plan_ref, h_ref, xs_ref, stage, zeros, sems, zsem, *, n_blk):
    tm = h_ref.shape[0] // ROW_SUB
    step = pl.program_id(0)
    last = pl.num_programs(0) - 1
    slot = lax.rem(step, 2)
    base = step * (2 * tm)
    blk_sub = MOE_TM * ROW_SUB

    def fill(wait):
        def run(copy):
            copy.wait() if wait else copy.start()

        def pads(e, carry):
            first = plan_ref[e]
            n = plan_ref[N_EXPERTS + e]
            bit = MOE_TM // 2
            while bit >= 1:
                @pl.when(jnp.bitwise_and(n, bit) != 0)
                def _(bit=bit):
                    done = jnp.bitwise_and(n, -2 * bit)
                    dst = pl.multiple_of(first + done * ROW_SUB, ROW_SUB)
                    run(pltpu.make_async_copy(zeros.at[pl.ds(0, bit * ROW_SUB)],
                                              xs_ref.at[pl.ds(dst, bit * ROW_SUB)], zsem))
                bit //= 2
            return carry

        lax.fori_loop(0, N_EXPERTS, pads, 0)

        def unused(b, carry):
            dst = pl.multiple_of(b * blk_sub, blk_sub)
            run(pltpu.make_async_copy(zeros, xs_ref.at[pl.ds(dst, blk_sub)], zsem))
            return carry

        lax.fori_loop(plan_ref[2 * N_EXPERTS], n_blk, unused, 0)

    @pl.when(step == 0)
    def _():
        zeros[...] = jnp.zeros_like(zeros)
        fill(wait=False)
        fill(wait=True)

    def wait_tile(sl):
        for _ in range(2):
            pltpu.make_async_copy(stage.at[sl], xs_ref.at[pl.ds(0, tm * ROW_SUB)], sems.at[sl]).wait()

    @pl.when(step >= 2)
    def _():
        wait_tile(slot)

    stage[slot] = h_ref[...]

    def start(c, carry):
        for u in range(ROW_UNROLL):
            r = c * ROW_UNROLL + u
            src = stage.at[slot, pl.ds(pl.multiple_of(r * ROW_SUB, ROW_SUB), ROW_SUB)]
            for k in range(2):
                dst = pl.multiple_of(dest_ref[base + 2 * r + k], ROW_SUB)
                pltpu.make_async_copy(src, xs_ref.at[pl.ds(dst, ROW_SUB)], sems.at[slot]).start()
        return carry

    lax.fori_loop(0, tm // ROW_UNROLL, start, 0)

    @pl.when(step == last)
    def _():
        wait_tile(slot)

        @pl.when(step >= 1)
        def _():
            wait_tile(1 - slot)


def _dispatch(dest_flat, plan_flat, h2p, m_pad, tm=256):
    t = h2p.shape[0] // ROW_SUB
    tm = min(tm, t)
    return pl.pallas_call(
        functools.partial(_dispatch_kernel, n_blk=m_pad // MOE_TM),
        grid_spec=pltpu.PrefetchScalarGridSpec(
            num_scalar_prefetch=2,
            grid=(t // tm,),
            in_specs=[pl.BlockSpec((tm * ROW_SUB, LANES), lambda i, dest, plan: (i, 0))],
            out_specs=pl.BlockSpec(memory_space=pl.ANY),
            scratch_shapes=[pltpu.VMEM((2, tm * ROW_SUB, LANES), h2p.dtype),
                            pltpu.VMEM((MOE_TM * ROW_SUB, LANES), h2p.dtype),
                            pltpu.SemaphoreType.DMA((2,)),
                            pltpu.SemaphoreType.DMA(())]),
        out_shape=jax.ShapeDtypeStruct((m_pad * ROW_SUB, LANES), h2p.dtype),
        compiler_params=_params(("arbitrary",)),
        name="dispatch",
    )(dest_flat, plan_flat, h2p)


def _experts_kernel(blk_ref, xs_ref, wg_hbm, wu_hbm, wd_hbm, ys_ref,
                    wg_f32, wu_f32, wd_f32, wg_bf, wu_bf, wd_bf, slot_ref, sems):
    i = pl.program_id(0)
    n_blk = pl.num_programs(0)
    n_used = blk_ref[n_blk]
    e = blk_ref[i]
    in_use = i < n_used
    first_of_run = jnp.logical_and(
        in_use, jnp.logical_or(i == 0, blk_ref[jnp.maximum(i - 1, 0)] != e))

    def weight_copies(expert, sl):
        return [pltpu.make_async_copy(wg_hbm.at[0, expert], wg_f32.at[sl], sems.at[sl]),
                pltpu.make_async_copy(wu_hbm.at[0, expert], wu_f32.at[sl], sems.at[sl]),
                pltpu.make_async_copy(wd_hbm.at[0, expert], wd_f32.at[sl], sems.at[sl])]

    @pl.when(i == 0)
    def _():
        slot_ref[0] = 0
        for c in weight_copies(e, 0):
            c.start()

    @pl.when(first_of_run)
    def _():
        sl = slot_ref[0]
        for c in weight_copies(e, sl):
            c.wait()
        wg_bf[...] = wg_f32[sl].astype(BF16)
        wu_bf[...] = wu_f32[sl].astype(BF16)
        wd_bf[...] = wd_f32[sl].astype(BF16)
        nxt = lax.while_loop(
            lambda j: jnp.logical_and(j < n_used, blk_ref[jnp.minimum(j, n_blk - 1)] == e),
            lambda j: j + 1, i + 1)

        @pl.when(nxt < n_used)
        def _():
            for c in weight_copies(blk_ref[nxt], 1 - sl):
                c.start()

        slot_ref[0] = 1 - sl

    @pl.when(in_use)
    def _():
        xb = jnp.concatenate([c.astype(BF16) for c in _load_row_tiles(xs_ref, MOE_TM)], axis=1)
        g = jnp.dot(xb, wg_bf[...], preferred_element_type=F32)
        u = jnp.dot(xb, wu_bf[...], preferred_element_type=F32)
        hmid = (g * jax.nn.sigmoid(g) * u).astype(BF16)
        _store_row_tiles(ys_ref, jnp.dot(hmid, wd_bf[...], preferred_element_type=F32))

    @pl.when(jnp.logical_not(in_use))
    def _():
        ys_ref[...] = jnp.zeros_like(ys_ref)


def _experts(blk_expert, xs, w_gate, w_up, w_down):
    d, de = w_gate.shape[-2:]
    n_blk = xs.shape[0] // (MOE_TM * ROW_SUB)
    return pl.pallas_call(
        _experts_kernel,
        grid_spec=pltpu.PrefetchScalarGridSpec(
            num_scalar_prefetch=1,
            grid=(n_blk,),
            in_specs=[pl.BlockSpec((MOE_TM * ROW_SUB, LANES), lambda i, blk: (i, 0)),
                      pl.BlockSpec(memory_space=pl.ANY),
                      pl.BlockSpec(memory_space=pl.ANY),
                      pl.BlockSpec(memory_space=pl.ANY)],
            out_specs=pl.BlockSpec((MOE_TM * ROW_SUB, LANES), lambda i, blk: (i, 0)),
            scratch_shapes=[pltpu.VMEM((2, d, de), F32),
                            pltpu.VMEM((2, d, de), F32),
                            pltpu.VMEM((2, de, d), F32),
                            pltpu.VMEM((d, de), BF16),
                            pltpu.VMEM((d, de), BF16),
                            pltpu.VMEM((de, d), BF16),
                            pltpu.SMEM((1,), I32),
                            pltpu.SemaphoreType.DMA((2,))]),
        out_shape=jax.ShapeDtypeStruct(xs.shape, jnp.uint32),
        compiler_params=_params(("arbitrary",)),
        name="experts",
    )(blk_expert, xs, w_gate, w_up, w_down)


def _combine_kernel(dest_ref, ys_ref, gate_ref, x1_ref, mod_ref, g_ref, b_ref, o_ref, *scratch):
    ybufs, sems = scratch[:COMBINE_SLOTS], scratch[COMBINE_SLOTS]
    tm = x1_ref.shape[1]
    n_steps = pl.num_programs(0) * pl.num_programs(1)
    step = pl.program_id(0) * pl.num_programs(1) + pl.program_id(1)

    def row_copy(base, r, k, sl):
        src = pl.multiple_of(dest_ref[base + 2 * r + k], ROW_SUB)
        return pltpu.make_async_copy(ys_ref.at[pl.ds(src, ROW_SUB)],
                                     ybufs[sl].at[k, pl.ds(pl.multiple_of(r * ROW_SUB, ROW_SUB), ROW_SUB)],
                                     sems.at[sl])

    def wait_slot(sl):
        for k in range(2):
            pltpu.make_async_copy(ys_ref.at[pl.ds(0, tm * ROW_SUB)], ybufs[sl].at[k], sems.at[sl]).wait()

    @pl.when(step == 0)
    def _():
        for ahead in range(COMBINE_SLOTS - 1):
            base = jnp.minimum(ahead, n_steps - 1) * (2 * tm)

            def start(c, carry, base=base, ahead=ahead):
                for u in range(ROW_UNROLL):
                    for k in range(2):
                        row_copy(base, c * ROW_UNROLL + u, k, ahead).start()
                return carry

            lax.fori_loop(0, tm // ROW_UNROLL, start, 0)

    def run(sl):
        wait_slot(sl)
        ahead_base = jnp.minimum(step + COMBINE_SLOTS - 1, n_steps - 1) * (2 * tm)
        ahead_slot = (sl + COMBINE_SLOTS - 1) % COMBINE_SLOTS
        for r in range(tm):
            for k in range(2):
                row_copy(ahead_base, r, k, ahead_slot).start()

        gates = gate_ref[0]
        g0 = gates[:, 0:1]
        g1 = gates[:, 1:2]
        y0 = _load_row_tiles(ybufs[sl].at[0], tm)
        y1 = _load_row_tiles(ybufs[sl].at[1], tm)
        ffn = jnp.concatenate([g0 * a + g1 * b for a, b in zip(y0, y1)], axis=1)
        gate2 = mod_ref[0, 5:6, :]
        o_ref[0] = _layer_norm(ALPHA * x1_ref[0] + (1.0 + gate2) * ffn, g_ref[...], b_ref[...])

        @pl.when(step == n_steps - 1)
        def _():
            for ahead in range(1, COMBINE_SLOTS):
                wait_slot((sl + ahead) % COMBINE_SLOTS)

    for sl in range(COMBINE_SLOTS):
        pl.when(lax.rem(step, COMBINE_SLOTS) == sl)(functools.partial(run, sl))


def _combine(dest_flat, ys, gates3, x1, mod3, ln_g, ln_b, tm=256):
    bsz, s, d = x1.shape
    tm = min(tm, s)
    return pl.pallas_call(
        _combine_kernel,
        grid_spec=pltpu.PrefetchScalarGridSpec(
            num_scalar_prefetch=1,
            grid=(bsz, s // tm),
            in_specs=[pl.BlockSpec(memory_space=pl.ANY),
                      pl.BlockSpec((1, tm, LANES), lambda b, i, dest: (b, i, 0)),
                      pl.BlockSpec((1, tm, d), lambda b, i, dest: (b, i, 0)),
                      pl.BlockSpec((1, 6, d), lambda b, i, dest: (b, 0, 0)),
                      pl.BlockSpec((1, d), lambda b, i, dest: (0, 0)),
                      pl.BlockSpec((1, d), lambda b, i, dest: (0, 0))],
            out_specs=pl.BlockSpec((1, tm, d), lambda b, i, dest: (b, i, 0)),
            scratch_shapes=[pltpu.VMEM((2, tm * ROW_SUB, LANES), jnp.uint32)] * COMBINE_SLOTS
            + [pltpu.SemaphoreType.DMA((COMBINE_SLOTS,))]),
        out_shape=jax.ShapeDtypeStruct((bsz, s, d), F32),
        compiler_params=_params(("arbitrary", "arbitrary")),
        name="combine",
    )(dest_flat, ys, gates3, x1, mod3, ln_g, ln_b)


def kernel(x, c, w_in, w_out, sinks, rel_bias, norm_a, norm_b, w_ada, b_ada, ln1_g, ln1_b,
           ln2_g, ln2_b, w_grp, b_grp, w_rtr, b_rtr, w_gate, w_up, w_down):
    bsz, s, d = x.shape
    t = bsz * s
    d_a = norm_a.shape[-1]
    d_b = norm_b.shape[-1]

    mod3 = _adaln(c, w_ada, b_ada).reshape(bsz, 6, d)

    assert KV_A * HEAD_DIM == LANES
    kv_w = 2 * KV_A * HEAD_DIM
    group_a = d_a // HEAD_DIM // KV_A
    head_order = [g * group_a + p for p in range(group_a) for g in range(KV_A)]
    perm_a = np.concatenate([np.arange(HEAD_DIM) + HEAD_DIM * h for h in head_order])
    w0 = w_in[0]
    w_in_bf = jnp.concatenate(
        [w0[:, d_a + kv_w:d_a + kv_w + d_b] * (ATTN_SCALE * LOG2E),
         w0[:, d_a + kv_w + d_b:],
         w0[:, :d_a][:, perm_a] * (ATTN_SCALE * LOG2E),
         w0[:, d_a:d_a + kv_w]], axis=1).astype(BF16)
    qkv = _qkv(x, mod3, w_in_bf)
    norm_a = norm_a[:, perm_a]
    w_out_bf = jnp.concatenate([w_out[0][:d_a][perm_a], w_out[0][d_a:]], axis=0).astype(BF16)

    o_a = _swa(qkv, sinks[0] * LOG2E, _swa_bias(rel_bias), d_a, d_b)
    o_b = _sb(qkv, d_b)

    w_r = jnp.concatenate([w_grp[0], w_rtr[0]], axis=1)
    w_r = jnp.pad(w_r, ((0, 0), (0, LANES - w_r.shape[1])))
    b_r = jnp.pad(jnp.concatenate([b_grp[0], b_rtr[0]]), (0, LANES - N_GROUPS - N_EXPERTS))[None, :]
    wr_hi = w_r.astype(BF16)
    wr_lo = (w_r - wr_hi.astype(F32)).astype(BF16)
    assert d == 2 * LANES * ROW_SUB, "row tiles hold 256 * ROW_SUB features"
    x1, h2, logits = _mix_ln1(o_a, o_b, x, mod3, norm_a, norm_b, w_out_bf,
                              ln1_g, ln1_b, jnp.concatenate([wr_hi, wr_lo], axis=1), b_r)

    sel, gates, counts = _route(logits.reshape(t, LANES))
    m_pad = 2 * t + N_EXPERTS * MOE_TM
    n_blk = m_pad // MOE_TM
    dest, blk, plan = _dest(sel, counts, n_blk)
    dest_flat = dest[:, :2].reshape(2 * t)
    blk_expert = jnp.concatenate([blk[:n_blk, 0], plan[2, :1]])
    plan_flat = jnp.concatenate([plan[0, :N_EXPERTS], plan[1, :N_EXPERTS], plan[2, :1]])

    xs = _dispatch(dest_flat, plan_flat, h2.reshape(t * ROW_SUB, LANES), m_pad)
    ys = _experts(blk_expert, xs, w_gate, w_up, w_down)
    return _combine(dest_flat, ys, gates.reshape(bsz, s, LANES), x1, mod3, ln2_g, ln2_b)
```

```python
import functools
import math

import jax
import jax.numpy as jnp
import numpy as np
from jax import lax
from jax.experimental import pallas as pl
from jax.experimental.pallas import tpu as pltpu

F32 = jnp.float32
BF16 = jnp.bfloat16
I32 = jnp.int32

HEAD_DIM = 64
KV_A = 2
NUM_BUCKETS = 32
MAX_DISTANCE = 128
WINDOW = 128
Q_BLOCK = 128
N_GROUPS = 4
EXPERTS_PER_GROUP = 8
N_EXPERTS = N_GROUPS * EXPERTS_PER_GROUP
DEPTH = 1
ALPHA = (2.0 * DEPTH) ** 0.25
ATTN_SCALE = 1.0 / math.sqrt(HEAD_DIM)
EPS = 1e-5
NEG_INF = -1e30
LOG2E = math.log2(math.e)

LANES = 128
ROW_SUB = 8
MOE_TM = 256
COMBINE_SLOTS = 3
ROW_UNROLL = 8
SWA_PAIRS = 4
SB_GROUP = 8
SB_TAIL_ROWS = 48
SB_SKIP_BITS = 150.0
VMEM_LIMIT = 48 * 1024 * 1024


def _params(sem, vmem=VMEM_LIMIT):
    return pltpu.CompilerParams(dimension_semantics=sem, vmem_limit_bytes=vmem)


def _store_row_tiles(ref_2d, y):
    n = y.shape[0]
    for s in range(ROW_SUB):
        lo = pltpu.bitcast(y[:, 2 * s * LANES:(2 * s + 1) * LANES].astype(BF16).astype(F32), jnp.uint32)
        hi = pltpu.bitcast(y[:, (2 * s + 1) * LANES:(2 * s + 2) * LANES].astype(BF16).astype(F32), jnp.uint32)
        ref_2d[pl.ds(s, n, stride=ROW_SUB), :] = hi | (lo >> 16)


def _load_row_tiles(ref_2d, n):
    chunks = []
    for s in range(ROW_SUB):
        p = ref_2d[pl.ds(s, n, stride=ROW_SUB), :]
        chunks.append(pltpu.bitcast(p << 16, F32))
        chunks.append(pltpu.bitcast(p & jnp.uint32(0xFFFF0000), F32))
    return chunks


def _adaln_kernel(c_ref, w_ref, b_ref, o_ref):
    c = c_ref[...]
    ca = (c * jax.nn.sigmoid(c)).astype(BF16)
    o_ref[...] = jnp.dot(ca, w_ref[0].astype(BF16), preferred_element_type=F32) + b_ref[...]


def _adaln(c, w_ada, b_ada, tn=1024):
    bsz, d = c.shape
    n = w_ada.shape[-1]
    return pl.pallas_call(
        _adaln_kernel,
        grid=(n // tn,),
        in_specs=[pl.BlockSpec((bsz, d), lambda j: (0, 0)),
                  pl.BlockSpec((1, d, tn), lambda j: (0, 0, j)),
                  pl.BlockSpec((1, tn), lambda j: (0, j))],
        out_specs=pl.BlockSpec((bsz, tn), lambda j: (0, j)),
        out_shape=jax.ShapeDtypeStruct((bsz, n), F32),
        compiler_params=_params(("arbitrary",)),
        name="adaln",
    )(c, w_ada, b_ada)


def _qkv_kernel(x_ref, mod_ref, w_ref, o_ref):
    shift = mod_ref[0, 0:1, :]
    scale = mod_ref[0, 1:2, :]
    h = (x_ref[0] * (1.0 + scale) + shift).astype(BF16)
    o_ref[0] = jnp.dot(h, w_ref[...], preferred_element_type=F32).astype(BF16)


def _qkv(x, mod3, w_in_bf, tm=512, nj=2):
    bsz, s, d = x.shape
    n = w_in_bf.shape[1]
    tn = n // nj
    tm = min(tm, s)
    return pl.pallas_call(
        _qkv_kernel,
        grid=(nj, bsz, s // tm),
        in_specs=[pl.BlockSpec((1, tm, d), lambda j, b, i: (b, i, 0)),
                  pl.BlockSpec((1, 6, d), lambda j, b, i: (b, 0, 0)),
                  pl.BlockSpec((d, tn), lambda j, b, i: (0, j))],
        out_specs=pl.BlockSpec((1, tm, tn), lambda j, b, i: (b, i, j)),
        out_shape=jax.ShapeDtypeStruct((bsz, s, n), BF16),
        compiler_params=_params(("arbitrary", "arbitrary", "arbitrary")),
        name="qkv",
    )(x, mod3, w_in_bf)


def _bucket_map():
    qi = np.arange(WINDOW)[:, None]
    kj = np.arange(2 * WINDOW)[None, :]
    dist = qi + WINDOW - kj
    n = np.maximum(dist, 0)
    max_exact = NUM_BUCKETS // 2
    ratio = np.maximum(n, max_exact).astype(np.float32) / np.float32(max_exact)
    large = max_exact + (np.log(ratio) / np.float32(math.log(MAX_DISTANCE / max_exact))
                         * np.float32(NUM_BUCKETS - max_exact)).astype(np.int32)
    large = np.minimum(large, NUM_BUCKETS - 1)
    bucket = np.where(n < max_exact, n, large)
    band = (dist >= 0) & (dist < WINDOW)
    return np.where(band, bucket, -1).astype(np.int32)


def _swa_bias_kernel(rb_ref, bucket_ref, o_ref):
    first = pl.program_id(0) == 0
    bucket = bucket_ref[...]
    col = lax.broadcasted_iota(I32, bucket.shape, 1)
    hidden = jnp.logical_and(first, col < WINDOW)
    for h in range(o_ref.shape[1]):
        acc = jnp.full(bucket.shape, NEG_INF, F32)
        for b in range(NUM_BUCKETS):
            acc = jnp.where(bucket == b, rb_ref[b, h] * LOG2E, acc)
        o_ref[0, h] = jnp.where(hidden, NEG_INF, acc)


def _swa_bias(rel_bias):
    nh = rel_bias.shape[1]
    bucket = jnp.asarray(_bucket_map())
    return pl.pallas_call(
        _swa_bias_kernel,
        grid=(2,),
        in_specs=[pl.BlockSpec(memory_space=pltpu.SMEM),
                  pl.BlockSpec((WINDOW, 2 * WINDOW), lambda v: (0, 0))],
        out_specs=pl.BlockSpec((1, nh, WINDOW, 2 * WINDOW), lambda v: (v, 0, 0, 0)),
        out_shape=jax.ShapeDtypeStruct((2, nh, WINDOW, 2 * WINDOW), F32),
        compiler_params=_params(("arbitrary",)),
        name="swa_bias",
    )(rel_bias, bucket)


def _swa_kernel(sink_ref, q_ref, kvc_ref, kvp_ref, bias_ref, o_ref, *, n_heads):
    group = n_heads // KV_A
    kv = jnp.concatenate([kvp_ref[0], kvc_ref[0]], axis=0)
    lane = lax.broadcasted_iota(I32, (2 * WINDOW, LANES), 1)
    low = lane < HEAD_DIM

    def halves(pair):
        zero = jnp.zeros_like(pair)
        return [jnp.where(low, pair, zero), jnp.where(low, zero, pair)]

    kz = halves(kv[:, 0:LANES])
    vz = halves(kv[:, LANES:2 * LANES])

    n_pairs = n_heads // KV_A
    for p0 in range(0, n_pairs, SWA_PAIRS):
        pairs = range(p0, min(p0 + SWA_PAIRS, n_pairs))
        heads = [(p, g) for p in pairs for g in range(KV_A)]
        logits, e, den, o = {}, {}, {}, {}
        for p, g in heads:
            qp = q_ref[0, :, p * LANES:(p + 1) * LANES]
            s = lax.dot_general(qp, kz[g], (((1,), (1,)), ((), ())), preferred_element_type=F32)
            logits[p, g] = s + bias_ref[0, g * group + p]
        for p, g in heads:
            sink = sink_ref[g * group + p]
            m = jnp.maximum(jnp.max(logits[p, g], axis=-1, keepdims=True), sink)
            e[p, g] = jnp.exp2(logits[p, g] - m)
            den[p, g] = jnp.sum(e[p, g], axis=-1, keepdims=True) + jnp.exp2(sink - m)
        for p, g in heads:
            o[p, g] = jnp.dot(e[p, g].astype(BF16), vz[g], preferred_element_type=F32)
        for p in pairs:
            acc = o[p, 0] * (1.0 / den[p, 0])
            for g in range(1, KV_A):
                acc = acc + o[p, g] * (1.0 / den[p, g])
            o_ref[0, :, p * LANES:(p + 1) * LANES] = acc.astype(BF16)


def _swa(qkv, sinks, bias, d_a, d_b):
    bsz, s, _ = qkv.shape
    n_heads = d_a // HEAD_DIM
    q_blk = 3 * d_b // d_a
    kv_blk = (3 * d_b + d_a) // (2 * LANES)
    return pl.pallas_call(
        functools.partial(_swa_kernel, n_heads=n_heads),
        grid=(bsz, s // WINDOW),
        in_specs=[pl.BlockSpec(memory_space=pltpu.SMEM),
                  pl.BlockSpec((1, WINDOW, d_a), lambda b, i: (b, i, q_blk)),
                  pl.BlockSpec((1, WINDOW, 2 * LANES), lambda b, i: (b, i, kv_blk)),
                  pl.BlockSpec((1, WINDOW, 2 * LANES),
                               lambda b, i: (b, jnp.maximum(i - 1, 0), kv_blk)),
                  pl.BlockSpec((1, n_heads, WINDOW, 2 * WINDOW),
                               lambda b, i: (jnp.minimum(i, 1), 0, 0, 0))],
        out_specs=pl.BlockSpec((1, WINDOW, d_a), lambda b, i: (b, i, 0)),
        out_shape=jax.ShapeDtypeStruct((bsz, s, d_a), BF16),
        compiler_params=_params(("arbitrary", "arbitrary")),
        name="swa",
    )(sinks, qkv, qkv, qkv, bias)


def _suffix_matrix():
    j = np.arange(Q_BLOCK)[:, None]
    s = np.arange(Q_BLOCK)[None, :]
    return np.concatenate([(j > s), np.ones((Q_BLOCK, Q_BLOCK), bool)], axis=1).astype(np.float32)


def _sb_kernel(q_ref, k_ref, v_ref, lt_ref, o_ref, acc_ref, carry_ref, *, group):
    i = pl.program_id(2)
    lane = lax.broadcasted_iota(I32, (Q_BLOCK, LANES), 1)
    low = lane < HEAD_DIM
    row = lax.broadcasted_iota(I32, (2 * Q_BLOCK, Q_BLOCK), 0)
    col = lax.broadcasted_iota(I32, (2 * Q_BLOCK, Q_BLOCK), 1)
    strict = col < jnp.where(row >= Q_BLOCK, row - Q_BLOCK, row)
    sign = jnp.uint32(0x80000000)

    qh = []
    for g in range(group):
        q = q_ref[0, :, g * LANES:(g + 1) * LANES]
        zero = jnp.zeros_like(q)
        qh.append([jnp.where(low, q, zero), jnp.where(low, zero, q)])

    gs = range(group)

    def scores(j, g, lo, hi):
        return lax.dot_general(jnp.concatenate([qh[g][0][lo:hi], qh[g][1][lo:hi]], axis=0),
                               k_ref[0, pl.ds(pl.multiple_of(j * Q_BLOCK, Q_BLOCK), Q_BLOCK),
                                     g * LANES:(g + 1) * LANES],
                               (((1,), (1,)), ((), ())), preferred_element_type=F32)

    def softplus2(z):
        neg_abs = pltpu.bitcast(pltpu.bitcast(z, jnp.uint32) | sign, F32)
        return jnp.maximum(z, 0.0) + jnp.log2(1.0 + jnp.exp2(neg_abs))

    def suffix(sp):
        return jnp.dot(sp.astype(BF16), lt_ref[...], preferred_element_type=F32)

    def weighted_values(a, j, g, rows):
        a = a.astype(BF16)
        a2 = jnp.concatenate([a[:rows], a[rows:]], axis=1)
        vj = v_ref[0, pl.ds(pl.multiple_of(j * Q_BLOCK, Q_BLOCK), Q_BLOCK), g * LANES:(g + 1) * LANES]
        vzero = jnp.zeros_like(vj)
        vz = jnp.concatenate([jnp.where(low, vj, vzero), jnp.where(low, vzero, vj)], axis=0)
        return jnp.dot(a2, vz, preferred_element_type=F32)

    t = SB_TAIL_ROWS

    def carry_mins(carry_min, lo, hi):
        n = hi - lo
        n_top = max(min(hi, t) - lo, 0)
        top = jnp.min(jnp.minimum(carry_min[:n_top], carry_min[n:n + n_top])) if n_top else None
        rest = jnp.min(jnp.minimum(carry_min[n_top:n], carry_min[n + n_top:])) if n_top < n else None
        return top, rest

    def first_blocks(n_before):
        pieces = [(i, 0, Q_BLOCK)]
        if n_before >= 1:
            pieces.append((i - 1, 0, Q_BLOCK))
        if n_before >= 2:
            pieces.append((i - 2, 0, t))
        z = {(b, g): scores(j, g, lo, hi) for b, (j, lo, hi) in enumerate(pieces) for g in gs}
        sp = {}
        for (b, g), zz in z.items():
            s = softplus2(zz)
            sp[b, g] = jnp.where(strict, s, 0.0) if b == 0 else s
        cs = {bg: suffix(s) for bg, s in sp.items()}
        a, carry_min, head_min = {}, None, None
        for g in gs:
            a[0, g] = jnp.where(strict, jnp.exp2(z[0, g] - sp[0, g] - cs[0, g][:, :Q_BLOCK]), 0.0)
            carry = cs[0, g][:, Q_BLOCK:]
            if n_before >= 1:
                a[1, g] = jnp.exp2(z[1, g] - sp[1, g] - cs[1, g][:, :Q_BLOCK] - carry)
                carry = carry + cs[1, g][:, Q_BLOCK:]
            carry_ref[g, 0] = carry[:Q_BLOCK]
            carry_ref[g, 1] = carry[Q_BLOCK:]
            carry_min = carry if carry_min is None else jnp.minimum(carry_min, carry)
            if n_before >= 2:
                head = jnp.concatenate([carry[:t], carry[Q_BLOCK:Q_BLOCK + t]], axis=0)
                a[2, g] = jnp.exp2(z[2, g] - sp[2, g] - cs[2, g][:, :Q_BLOCK] - head)
                head = head + cs[2, g][:, Q_BLOCK:]
                carry_ref[g, 0, :t] = head[:t]
                carry_ref[g, 1, :t] = head[t:]
                head_min = head if head_min is None else jnp.minimum(head_min, head)
        for g in gs:
            acc = weighted_values(a[0, g], i, g, Q_BLOCK)
            if n_before >= 1:
                acc = acc + weighted_values(a[1, g], i - 1, g, Q_BLOCK)
            acc_ref[g] = acc
            if n_before >= 2:
                acc_ref[g, :t] += weighted_values(a[2, g], i - 2, g, t)
        top, rest = carry_mins(carry_min, 0, Q_BLOCK)
        if n_before >= 2:
            top = carry_mins(head_min, 0, t)[0]
        return top, rest

    def block(j, lo, hi):
        n = hi - lo
        z = [scores(j, g, lo, hi) for g in gs]
        sp = [softplus2(zz) for zz in z]
        cs = [suffix(s) for s in sp]
        a, carry_min = [], None
        for g in gs:
            carry = jnp.concatenate([carry_ref[g, 0, lo:hi], carry_ref[g, 1, lo:hi]], axis=0)
            a.append(jnp.exp2(z[g] - sp[g] - cs[g][:, :Q_BLOCK] - carry))
            carry = carry + cs[g][:, Q_BLOCK:]
            carry_ref[g, 0, lo:hi] = carry[:n]
            carry_ref[g, 1, lo:hi] = carry[n:]
            carry_min = carry if carry_min is None else jnp.minimum(carry_min, carry)
        for g in gs:
            acc_ref[g, lo:hi] += weighted_values(a[g], j, g, n)
        return carry_mins(carry_min, lo, hi)

    def two_before():
        top, rest = first_blocks(2)
        rest = lax.cond(rest < SB_SKIP_BITS, lambda: block(i - 2, t, Q_BLOCK)[1], lambda: rest)
        return top, rest

    top0, rest0 = lax.cond(
        i >= 2, two_before,
        lambda: lax.cond(i == 1, lambda: first_blocks(1), lambda: first_blocks(0)))

    def more(state):
        jj, top, rest = state
        return jnp.logical_and(jj < i - 2, jnp.minimum(top, rest) < SB_SKIP_BITS)

    def body(state):
        jj, _, rest = state
        j = i - 3 - jj

        def tail_rows():
            return block(j, 0, t)[0], rest

        def all_rows():
            return block(j, 0, Q_BLOCK)

        top, rest = lax.cond(rest >= SB_SKIP_BITS, tail_rows, all_rows)
        return jj + 1, top, rest

    lax.while_loop(more, body, (jnp.int32(0), top0, rest0))
    for g in range(group):
        o_ref[0, :, g * LANES:(g + 1) * LANES] = acc_ref[g].astype(BF16)


def _sb(qkv, d_b, group=SB_GROUP):
    bsz, s, _ = qkv.shape
    pairs = d_b // LANES
    ng = pairs // group
    w = group * LANES
    lt = jnp.asarray(_suffix_matrix(), BF16)
    return pl.pallas_call(
        functools.partial(_sb_kernel, group=group),
        grid=(bsz, ng, s // Q_BLOCK),
        in_specs=[pl.BlockSpec((1, Q_BLOCK, w), lambda b, p, i: (b, i, p)),
                  pl.BlockSpec((1, s, w), lambda b, p, i: (b, 0, ng + p)),
                  pl.BlockSpec((1, s, w), lambda b, p, i: (b, 0, 2 * ng + p)),
                  pl.BlockSpec((Q_BLOCK, 2 * Q_BLOCK), lambda b, p, i: (0, 0))],
        out_specs=pl.BlockSpec((1, Q_BLOCK, w), lambda b, p, i: (b, i, p)),
        out_shape=jax.ShapeDtypeStruct((bsz, s, d_b), BF16),
        scratch_shapes=[pltpu.VMEM((group, Q_BLOCK, LANES), F32),
                        pltpu.VMEM((group, 2, Q_BLOCK, LANES), F32)],
        compiler_params=_params(("arbitrary", "arbitrary", "arbitrary")),
        name="sb",
    )(qkv, qkv, qkv, lt)


def _layer_norm(y, g, b):
    mu = jnp.mean(y, axis=-1, keepdims=True)
    yc = y - mu
    var = jnp.mean(yc * yc, axis=-1, keepdims=True)
    return yc * lax.rsqrt(var + EPS) * g + b


def _rms(o, g):
    return o * lax.rsqrt(jnp.mean(o * o, axis=-1, keepdims=True) + EPS) * g


def _mix_ln1_kernel(oa_ref, ob_ref, x_ref, mod_ref, na_ref, nb_ref, wo_ref, g_ref, b_ref,
                    wrc_ref, br_ref, x1_ref, h2_ref, lg_ref, *, d_a, parts):
    hm = x_ref.shape[1] // parts
    rows = [pl.ds(p * hm, hm) for p in range(parts)]
    gate1 = mod_ref[0, 2:3, :]
    shift2 = mod_ref[0, 3:4, :]
    scale2 = mod_ref[0, 4:5, :]
    ra = [_rms(oa_ref[0, r, :].astype(F32), na_ref[...]).astype(BF16) for r in rows]
    rb = [_rms(ob_ref[0, r, :].astype(F32), nb_ref[...]).astype(BF16) for r in rows]
    mix = [jnp.dot(ra[p], wo_ref[:d_a, :], preferred_element_type=F32)
           + jnp.dot(rb[p], wo_ref[d_a:, :], preferred_element_type=F32) for p in range(parts)]
    hi, lo = [], []
    for p, r in enumerate(rows):
        x1 = _layer_norm(ALPHA * x_ref[0, r, :] + (1.0 + gate1) * mix[p], g_ref[...], b_ref[...])
        x1_ref[0, r, :] = x1
        h2 = x1 * (1.0 + scale2) + shift2
        _store_row_tiles(h2_ref.at[0, pl.ds(p * hm * ROW_SUB, hm * ROW_SUB)], h2)
        hi.append(h2.astype(BF16))
        lo.append((h2 - hi[p].astype(F32)).astype(BF16))
    for p, r in enumerate(rows):
        both = jnp.dot(hi[p], wrc_ref[...], preferred_element_type=F32)
        lg_ref[0, r, :] = (both[:, :LANES] + both[:, LANES:]
                           + jnp.dot(lo[p], wrc_ref[:, :LANES], preferred_element_type=F32)
                           + br_ref[...])


def _mix_ln1(o_a, o_b, x, mod3, norm_a, norm_b, w_out_bf, ln_g, ln_b, wr_cat, b_r, tm=512, parts=2):
    bsz, s, d = x.shape
    d_a = o_a.shape[-1]
    d_b = o_b.shape[-1]
    tm = min(tm, s)
    row = lambda b, i: (b, i, 0)
    const2 = lambda b, i: (0, 0)
    once = pl.Buffered(1)
    return pl.pallas_call(
        functools.partial(_mix_ln1_kernel, d_a=d_a, parts=parts),
        grid=(bsz, s // tm),
        in_specs=[pl.BlockSpec((1, tm, d_a), row),
                  pl.BlockSpec((1, tm, d_b), row),
                  pl.BlockSpec((1, tm, d), row),
                  pl.BlockSpec((1, 6, d), lambda b, i: (b, 0, 0)),
                  pl.BlockSpec((1, d_a), const2),
                  pl.BlockSpec((1, d_b), const2),
                  pl.BlockSpec((d_a + d_b, d), const2, pipeline_mode=once),
                  pl.BlockSpec((1, d), const2),
                  pl.BlockSpec((1, d), const2),
                  pl.BlockSpec((d, 2 * LANES), const2, pipeline_mode=once),
                  pl.BlockSpec((1, LANES), const2)],
        out_specs=[pl.BlockSpec((1, tm, d), row),
                   pl.BlockSpec((1, tm * ROW_SUB, LANES), row),
                   pl.BlockSpec((1, tm, LANES), row)],
        out_shape=[jax.ShapeDtypeStruct((bsz, s, d), F32),
                   jax.ShapeDtypeStruct((bsz, s * ROW_SUB, LANES), jnp.uint32),
                   jax.ShapeDtypeStruct((bsz, s, LANES), F32)],
        compiler_params=_params(("arbitrary", "arbitrary")),
        name="mix_ln1",
    )(o_a, o_b, x, mod3, norm_a, norm_b, w_out_bf, ln_g, ln_b, wr_cat, b_r)


def _route_kernel(lg_ref, tri_ref, sel_ref, gate_ref, cnt_ref, base_ref):
    step = pl.program_id(0)

    @pl.when(step == 0)
    def _():
        base_ref[...] = jnp.zeros_like(base_ref)

    lg = lg_ref[...]
    tm = lg.shape[0]
    lane = lax.broadcasted_iota(I32, (tm, LANES), 1)
    big = jnp.int32(2 * LANES)
    glog = jnp.where(lane < N_GROUPS, lg, -jnp.inf)
    gmax = jnp.max(glog, axis=-1, keepdims=True)
    g_sel = jnp.min(jnp.where(glog == gmax, lane, big), axis=-1, keepdims=True)
    p_g = 1.0 / jnp.sum(jnp.exp(glog - gmax), axis=-1, keepdims=True)
    lo = N_GROUPS + g_sel * EXPERTS_PER_GROUP
    in_grp = jnp.logical_and(lane >= lo, lane < lo + EXPERTS_PER_GROUP)
    el = jnp.where(in_grp, lg, -jnp.inf)
    v1 = jnp.max(el, axis=-1, keepdims=True)
    i1 = jnp.min(jnp.where(el == v1, lane, big), axis=-1, keepdims=True)
    el2 = jnp.where(lane == i1, -jnp.inf, el)
    v2 = jnp.max(el2, axis=-1, keepdims=True)
    i2 = jnp.min(jnp.where(el2 == v2, lane, big), axis=-1, keepdims=True)
    r = jnp.exp(v2 - v1)
    w1 = 1.0 / (1.0 + r)
    g1 = p_g * w1
    g2 = p_g * (r * w1)
    e1 = i1 - N_GROUPS
    e2 = i2 - N_GROUPS
    oh1 = (lane == e1)
    oh2 = (lane == e2)
    occ = oh1.astype(F32) + oh2.astype(F32)
    before = jnp.dot(tri_ref[...], occ.astype(BF16), preferred_element_type=F32) + base_ref[...]
    r1 = jnp.sum(jnp.where(oh1, before, 0.0), axis=-1, keepdims=True)
    r2 = jnp.sum(jnp.where(oh2, before, 0.0), axis=-1, keepdims=True)
    base_ref[...] += jnp.sum(occ, axis=0, keepdims=True)
    cnt_ref[...] = base_ref[...]
    sel = jnp.where(lane == 0, e1, jnp.where(lane == 1, e2, 0))
    sel = jnp.where(lane == 2, r1.astype(I32), jnp.where(lane == 3, r2.astype(I32), sel))
    sel_ref[...] = sel
    gate_ref[...] = jnp.where(lane == 0, g1, jnp.where(lane == 1, g2, 0.0))


def _route(logits, tm=1024):
    t = logits.shape[0]
    tm = min(tm, t)
    tri = jnp.asarray(np.tril(np.ones((tm, tm), np.float32), -1), BF16)
    return pl.pallas_call(
        _route_kernel,
        grid=(t // tm,),
        in_specs=[pl.BlockSpec((tm, LANES), lambda i: (i, 0)),
                  pl.BlockSpec((tm, tm), lambda i: (0, 0))],
        out_specs=[pl.BlockSpec((tm, LANES), lambda i: (i, 0)),
                   pl.BlockSpec((tm, LANES), lambda i: (i, 0)),
                   pl.BlockSpec((1, LANES), lambda i: (0, 0))],
        out_shape=[jax.ShapeDtypeStruct((t, LANES), I32),
                   jax.ShapeDtypeStruct((t, LANES), F32),
                   jax.ShapeDtypeStruct((1, LANES), F32)],
        scratch_shapes=[pltpu.VMEM((1, LANES), F32)],
        compiler_params=_params(("arbitrary",)),
        name="route",
    )(logits, tri)


def _lane_prefix(x, lane):
    shift = 1
    while shift < LANES:
        x = x + jnp.where(lane >= shift, pltpu.roll(x, shift, axis=1), 0)
        shift *= 2
    return x


def _dest_kernel(sel_ref, cnt_ref, dest_ref, blk_ref, plan_ref, *, n_blk_pad):
    tm = sel_ref.shape[0]
    lane1 = lax.broadcasted_iota(I32, (8, LANES), 1)
    cnt = jnp.broadcast_to(cnt_ref[...].astype(I32), (8, LANES))
    cnt = jnp.where(lane1 < N_EXPERTS, cnt, 0)
    padded = jnp.bitwise_and(cnt + (MOE_TM - 1), -MOE_TM)
    pend = _lane_prefix(padded, lane1)
    pstart = (pend - padded)[0:1, :]
    sel = sel_ref[...]
    lane = lax.broadcasted_iota(I32, (tm, LANES), 1)
    e1 = sel[:, 0:1]
    e2 = sel[:, 1:2]
    d1 = jnp.sum(jnp.where(lane == e1, pstart, 0), axis=-1, keepdims=True) + sel[:, 2:3]
    d2 = jnp.sum(jnp.where(lane == e2, pstart, 0), axis=-1, keepdims=True) + sel[:, 3:4]
    dest_ref[...] = jnp.where(lane == 0, d1, jnp.where(lane == 1, d2, 0)) * ROW_SUB

    @pl.when(pl.program_id(0) == 0)
    def _():
        brow = lax.broadcasted_iota(I32, (n_blk_pad, LANES), 0) * MOE_TM
        blane = lax.broadcasted_iota(I32, (n_blk_pad, LANES), 1)
        ended = jnp.logical_and(blane < N_EXPERTS, pend[0:1, :] <= brow)
        be = jnp.minimum(jnp.sum(ended.astype(I32), axis=-1, keepdims=True), N_EXPERTS - 1)
        blk_ref[...] = jnp.broadcast_to(be, (n_blk_pad, LANES))
        sub = lax.broadcasted_iota(I32, (8, LANES), 0)
        used = jnp.max(pend, axis=-1, keepdims=True) >> (MOE_TM.bit_length() - 1)
        plan_ref[...] = jnp.where(sub == 0, pend - padded + cnt,
                                  jnp.where(sub == 1, padded - cnt, used))


def _dest(sel, counts, n_blk, tm=1024):
    t = sel.shape[0]
    tm = min(tm, t)
    n_blk_pad = -(-n_blk // 8) * 8
    return pl.pallas_call(
        functools.partial(_dest_kernel, n_blk_pad=n_blk_pad),
        grid=(t // tm,),
        in_specs=[pl.BlockSpec((tm, LANES), lambda i: (i, 0)),
                  pl.BlockSpec((1, LANES), lambda i: (0, 0))],
        out_specs=[pl.BlockSpec((tm, LANES), lambda i: (i, 0)),
                   pl.BlockSpec((n_blk_pad, LANES), lambda i: (0, 0)),
                   pl.BlockSpec((8, LANES), lambda i: (0, 0))],
        out_shape=[jax.ShapeDtypeStruct((t, LANES), I32),
                   jax.ShapeDtypeStruct((n_blk_pad, LANES), I32),
                   jax.ShapeDtypeStruct((8, LANES), I32)],
        compiler_params=_params(("arbitrary",)),
        name="dest",
    )(sel, counts)


def _invert_kernel(dest_ref, plan_ref, tok_ref):
    n_slots = tok_ref.shape[0]
    n_tok = dest_ref.shape[0] // 2
    shift = ROW_SUB.bit_length() - 1

    def clear_range(lo, hi):
        def one(m, carry):
            tok_ref[m] = 0
            return carry

        lax.fori_loop(lo, hi, one, 0)

    def pads(e, carry):
        clear_range(plan_ref[e], plan_ref[e] + plan_ref[N_EXPERTS + e])
        return carry

    lax.fori_loop(0, N_EXPERTS, pads, 0)
    clear_range(plan_ref[2 * N_EXPERTS] * MOE_TM, n_slots)

    def fill(c, carry):
        for u in range(ROW_UNROLL // 2):
            tok = c * (ROW_UNROLL // 2) + u
            for k in range(2):
                tok_ref[dest_ref[2 * tok + k] >> shift] = tok * ROW_SUB
        return carry

    lax.fori_loop(0, n_tok // (ROW_UNROLL // 2), fill, 0)


def _invert(dest_flat, plan_flat, m_pad):
    return pl.pallas_call(
        _invert_kernel,
        grid_spec=pltpu.PrefetchScalarGridSpec(
            num_scalar_prefetch=2,
            grid=(1,),
            in_specs=[],
            out_specs=pl.BlockSpec(memory_space=pltpu.SMEM)),
        out_shape=jax.ShapeDtypeStruct((m_pad,), I32),
        compiler_params=_params(("arbitrary",)),
        name="invert",
    )(dest_flat, plan_flat)


def _experts_kernel(blk_ref, tok_ref, h_hbm, wg_hbm, wu_hbm, wd_hbm, ys_ref,
                    xbuf0, xbuf1, xbuf2, wg_f32, wu_f32, wd_f32, wg_bf, wu_bf, wd_bf,
                    slot_ref, sems, xsems):
    xbufs = (xbuf0, xbuf1, xbuf2)
    i = pl.program_id(0)
    n_blk = pl.num_programs(0)
    n_used = blk_ref[n_blk]
    e = blk_ref[i]
    in_use = i < n_used
    first_of_run = jnp.logical_and(
        in_use, jnp.logical_or(i == 0, blk_ref[jnp.maximum(i - 1, 0)] != e))

    def row_copy(blk, r, sl):
        src = pl.multiple_of(tok_ref[blk * MOE_TM + r], ROW_SUB)
        return pltpu.make_async_copy(
            h_hbm.at[pl.ds(src, ROW_SUB)],
            xbufs[sl].at[pl.ds(pl.multiple_of(r * ROW_SUB, ROW_SUB), ROW_SUB)], xsems.at[sl])

    def wait_rows(sl):
        pltpu.make_async_copy(h_hbm.at[pl.ds(0, MOE_TM * ROW_SUB)], xbufs[sl], xsems.at[sl]).wait()

    def weight_copies(expert, sl):
        return [pltpu.make_async_copy(wg_hbm.at[0, expert], wg_f32.at[sl], sems.at[sl]),
                pltpu.make_async_copy(wu_hbm.at[0, expert], wu_f32.at[sl], sems.at[sl]),
                pltpu.make_async_copy(wd_hbm.at[0, expert], wd_f32.at[sl], sems.at[sl])]

    @pl.when(i == 0)
    def _():
        slot_ref[0] = 0
        for c in weight_copies(e, 0):
            c.start()
        for ahead in range(2):
            blk = jnp.minimum(ahead, n_used - 1)

            def start(c, carry, blk=blk, ahead=ahead):
                for u in range(ROW_UNROLL):
                    row_copy(blk, c * ROW_UNROLL + u, ahead).start()
                return carry

            lax.fori_loop(0, MOE_TM // ROW_UNROLL, start, 0)

    @pl.when(first_of_run)
    def _():
        sl = slot_ref[0]
        for c in weight_copies(e, sl):
            c.wait()
        wg_bf[...] = wg_f32[sl].astype(BF16)
        wu_bf[...] = wu_f32[sl].astype(BF16)
        wd_bf[...] = wd_f32[sl].astype(BF16)
        nxt = lax.while_loop(
            lambda j: jnp.logical_and(j < n_used, blk_ref[jnp.minimum(j, n_blk - 1)] == e),
            lambda j: j + 1, i + 1)

        @pl.when(nxt < n_used)
        def _():
            for c in weight_copies(blk_ref[nxt], 1 - sl):
                c.start(priority=1)

        slot_ref[0] = 1 - sl

    def run(sl):
        wait_rows(sl)
        ahead_blk = jnp.minimum(i + 2, n_used - 1)
        for r in range(MOE_TM):
            row_copy(ahead_blk, r, (sl + 2) % 3).start()
        xb = jnp.concatenate([c.astype(BF16) for c in _load_row_tiles(xbufs[sl], MOE_TM)], axis=1)
        g = jnp.dot(xb, wg_bf[...], preferred_element_type=F32)
        u = jnp.dot(xb, wu_bf[...], preferred_element_type=F32)
        hmid = (g * jax.nn.sigmoid(g) * u).astype(BF16)
        _store_row_tiles(ys_ref, jnp.dot(hmid, wd_bf[...], preferred_element_type=F32))

        @pl.when(i == n_used - 1)
        def _():
            wait_rows((sl + 1) % 3)
            wait_rows((sl + 2) % 3)

    for sl in range(3):
        pl.when(jnp.logical_and(in_use, lax.rem(i, 3) == sl))(functools.partial(run, sl))

    @pl.when(jnp.logical_not(in_use))
    def _():
        ys_ref[...] = jnp.zeros_like(ys_ref)


def _experts(blk_expert, slot_tok, h2p, w_gate, w_up, w_down):
    d, de = w_gate.shape[-2:]
    m_pad = slot_tok.shape[0]
    xbuf = pltpu.VMEM((MOE_TM * ROW_SUB, LANES), h2p.dtype)
    return pl.pallas_call(
        _experts_kernel,
        grid_spec=pltpu.PrefetchScalarGridSpec(
            num_scalar_prefetch=2,
            grid=(m_pad // MOE_TM,),
            in_specs=[pl.BlockSpec(memory_space=pl.ANY),
                      pl.BlockSpec(memory_space=pl.ANY),
                      pl.BlockSpec(memory_space=pl.ANY),
                      pl.BlockSpec(memory_space=pl.ANY)],
            out_specs=pl.BlockSpec((MOE_TM * ROW_SUB, LANES), lambda i, blk, tok: (i, 0)),
            scratch_shapes=[xbuf, xbuf, xbuf,
                            pltpu.VMEM((2, d, de), F32),
                            pltpu.VMEM((2, d, de), F32),
                            pltpu.VMEM((2, de, d), F32),
                            pltpu.VMEM((d, de), BF16),
                            pltpu.VMEM((d, de), BF16),
                            pltpu.VMEM((de, d), BF16),
                            pltpu.SMEM((1,), I32),
                            pltpu.SemaphoreType.DMA((2,)),
                            pltpu.SemaphoreType.DMA((3,))]),
        out_shape=jax.ShapeDtypeStruct((m_pad * ROW_SUB, LANES), jnp.uint32),
        compiler_params=_params(("arbitrary",)),
        name="experts",
    )(blk_expert, slot_tok, h2p, w_gate, w_up, w_down)


def _combine_kernel(dest_ref, ys_ref, gate_ref, x1_ref, mod_ref, g_ref, b_ref, o_ref, *scratch):
    ybufs, sems = scratch[:COMBINE_SLOTS], scratch[COMBINE_SLOTS]
    tm = x1_ref.shape[1]
    n_steps = pl.num_programs(0) * pl.num_programs(1)
    step = pl.program_id(0) * pl.num_programs(1) + pl.program_id(1)

    def row_copy(base, r, k, sl):
        src = pl.multiple_of(dest_ref[base + 2 * r + k], ROW_SUB)
        return pltpu.make_async_copy(ys_ref.at[pl.ds(src, ROW_SUB)],
                                     ybufs[sl].at[k, pl.ds(pl.multiple_of(r * ROW_SUB, ROW_SUB), ROW_SUB)],
                                     sems.at[sl])

    def wait_slot(sl):
        for k in range(2):
            pltpu.make_async_copy(ys_ref.at[pl.ds(0, tm * ROW_SUB)], ybufs[sl].at[k], sems.at[sl]).wait()

    @pl.when(step == 0)
    def _():
        for ahead in range(COMBINE_SLOTS - 1):
            base = jnp.minimum(ahead, n_steps - 1) * (2 * tm)

            def start(c, carry, base=base, ahead=ahead):
                for u in range(ROW_UNROLL):
                    for k in range(2):
                        row_copy(base, c * ROW_UNROLL + u, k, ahead).start()
                return carry

            lax.fori_loop(0, tm // ROW_UNROLL, start, 0)

    def run(sl):
        wait_slot(sl)
        ahead_base = jnp.minimum(step + COMBINE_SLOTS - 1, n_steps - 1) * (2 * tm)
        ahead_slot = (sl + COMBINE_SLOTS - 1) % COMBINE_SLOTS
        for r in range(tm):
            for k in range(2):
                row_copy(ahead_base, r, k, ahead_slot).start()

        gates = gate_ref[0]
        g0 = gates[:, 0:1]
        g1 = gates[:, 1:2]
        y0 = _load_row_tiles(ybufs[sl].at[0], tm)
        y1 = _load_row_tiles(ybufs[sl].at[1], tm)
        ffn = jnp.concatenate([g0 * a + g1 * b for a, b in zip(y0, y1)], axis=1)
        gate2 = mod_ref[0, 5:6, :]
        o_ref[0] = _layer_norm(ALPHA * x1_ref[0] + (1.0 + gate2) * ffn, g_ref[...], b_ref[...])

        @pl.when(step == n_steps - 1)
        def _():
            for ahead in range(1, COMBINE_SLOTS):
                wait_slot((sl + ahead) % COMBINE_SLOTS)

    for sl in range(COMBINE_SLOTS):
        pl.when(lax.rem(step, COMBINE_SLOTS) == sl)(functools.partial(run, sl))


def _combine(dest_flat, ys, gates3, x1, mod3, ln_g, ln_b, tm=256):
    bsz, s, d = x1.shape
    tm = min(tm, s)
    return pl.pallas_call(
        _combine_kernel,
        grid_spec=pltpu.PrefetchScalarGridSpec(
            num_scalar_prefetch=1,
            grid=(bsz, s // tm),
            in_specs=[pl.BlockSpec(memory_space=pl.ANY),
                      pl.BlockSpec((1, tm, LANES), lambda b, i, dest: (b, i, 0)),
                      pl.BlockSpec((1, tm, d), lambda b, i, dest: (b, i, 0)),
                      pl.BlockSpec((1, 6, d), lambda b, i, dest: (b, 0, 0)),
                      pl.BlockSpec((1, d), lambda b, i, dest: (0, 0)),
                      pl.BlockSpec((1, d), lambda b, i, dest: (0, 0))],
            out_specs=pl.BlockSpec((1, tm, d), lambda b, i, dest: (b, i, 0)),
            scratch_shapes=[pltpu.VMEM((2, tm * ROW_SUB, LANES), jnp.uint32)] * COMBINE_SLOTS
            + [pltpu.SemaphoreType.DMA((COMBINE_SLOTS,))]),
        out_shape=jax.ShapeDtypeStruct((bsz, s, d), F32),
        compiler_params=_params(("arbitrary", "arbitrary")),
        name="combine",
    )(dest_flat, ys, gates3, x1, mod3, ln_g, ln_b)


def kernel(x, c, w_in, w_out, sinks, rel_bias, norm_a, norm_b, w_ada, b_ada, ln1_g, ln1_b,
           ln2_g, ln2_b, w_grp, b_grp, w_rtr, b_rtr, w_gate, w_up, w_down):
    bsz, s, d = x.shape
    t = bsz * s
    d_a = norm_a.shape[-1]
    d_b = norm_b.shape[-1]

    mod3 = _adaln(c, w_ada, b_ada).reshape(bsz, 6, d)

    assert KV_A * HEAD_DIM == LANES
    kv_w = 2 * KV_A * HEAD_DIM
    group_a = d_a // HEAD_DIM // KV_A
    head_order = [g * group_a + p for p in range(group_a) for g in range(KV_A)]
    perm_a = np.concatenate([np.arange(HEAD_DIM) + HEAD_DIM * h for h in head_order])
    w0 = w_in[0]
    w_in_bf = jnp.concatenate(
        [w0[:, d_a + kv_w:d_a + kv_w + d_b] * (ATTN_SCALE * LOG2E),
         w0[:, d_a + kv_w + d_b:],
         w0[:, :d_a][:, perm_a] * (ATTN_SCALE * LOG2E),
         w0[:, d_a:d_a + kv_w]], axis=1).astype(BF16)
    qkv = _qkv(x, mod3, w_in_bf)
    norm_a = norm_a[:, perm_a]
    w_out_bf = jnp.concatenate([w_out[0][:d_a][perm_a], w_out[0][d_a:]], axis=0).astype(BF16)

    o_a = _swa(qkv, sinks[0] * LOG2E, _swa_bias(rel_bias), d_a, d_b)
    o_b = _sb(qkv, d_b)

    w_r = jnp.concatenate([w_grp[0], w_rtr[0]], axis=1)
    w_r = jnp.pad(w_r, ((0, 0), (0, LANES - w_r.shape[1])))
    b_r = jnp.pad(jnp.concatenate([b_grp[0], b_rtr[0]]), (0, LANES - N_GROUPS - N_EXPERTS))[None, :]
    wr_hi = w_r.astype(BF16)
    wr_lo = (w_r - wr_hi.astype(F32)).astype(BF16)
    assert d == 2 * LANES * ROW_SUB, "row tiles hold 256 * ROW_SUB features"
    x1, h2, logits = _mix_ln1(o_a, o_b, x, mod3, norm_a, norm_b, w_out_bf,
                              ln1_g, ln1_b, jnp.concatenate([wr_hi, wr_lo], axis=1), b_r)

    sel, gates, counts = _route(logits.reshape(t, LANES))
    m_pad = 2 * t + N_EXPERTS * MOE_TM
    n_blk = m_pad // MOE_TM
    dest, blk, plan = _dest(sel, counts, n_blk)
    dest_flat = dest[:, :2].reshape(2 * t)
    blk_expert = jnp.concatenate([blk[:n_blk, 0], plan[2, :1]])
    plan_flat = jnp.concatenate([plan[0, :N_EXPERTS], plan[1, :N_EXPERTS], plan[2, :1]])
    slot_tok = _invert(dest_flat, plan_flat, m_pad)
    ys = _experts(blk_expert, slot_tok, h2.reshape(t * ROW_SUB, LANES), w_gate, w_up, w_down)
    return _combine(dest_flat, ys, gates.reshape(bsz, s, LANES), x1, mod3, ln2_g, ln2_b)
```

```python
import functools
import math

import jax
import jax.numpy as jnp
import numpy as np
from jax import lax
from jax.experimental import pallas as pl
from jax.experimental.pallas import tpu as pltpu

F32 = jnp.float32
BF16 = jnp.bfloat16
I32 = jnp.int32

HEAD_DIM = 64
KV_A = 2
NUM_BUCKETS = 32
MAX_DISTANCE = 128
WINDOW = 128
Q_BLOCK = 128
N_GROUPS = 4
EXPERTS_PER_GROUP = 8
N_EXPERTS = N_GROUPS * EXPERTS_PER_GROUP
DEPTH = 1
ALPHA = (2.0 * DEPTH) ** 0.25
ATTN_SCALE = 1.0 / math.sqrt(HEAD_DIM)
EPS = 1e-5
NEG_INF = -1e30
LOG2E = math.log2(math.e)

LANES = 128
ROW_SUB = 8
MOE_TM = 256
COMBINE_SLOTS = 3
ROW_UNROLL = 8
SWA_PAIRS = 4
SB_GROUP = 8
SB_TAIL_ROWS = 48
SB_SKIP_BITS = 150.0
VMEM_LIMIT = 48 * 1024 * 1024


def _params(sem, vmem=VMEM_LIMIT):
    return pltpu.CompilerParams(dimension_semantics=sem, vmem_limit_bytes=vmem)


def _store_row_tiles(ref_2d, y):
    n = y.shape[0]
    for s in range(ROW_SUB):
        lo = pltpu.bitcast(y[:, 2 * s * LANES:(2 * s + 1) * LANES].astype(BF16).astype(F32), jnp.uint32)
        hi = pltpu.bitcast(y[:, (2 * s + 1) * LANES:(2 * s + 2) * LANES].astype(BF16).astype(F32), jnp.uint32)
        ref_2d[pl.ds(s, n, stride=ROW_SUB), :] = hi | (lo >> 16)


def _load_row_tiles(ref_2d, n):
    chunks = []
    for s in range(ROW_SUB):
        p = ref_2d[pl.ds(s, n, stride=ROW_SUB), :]
        chunks.append(pltpu.bitcast(p << 16, F32))
        chunks.append(pltpu.bitcast(p & jnp.uint32(0xFFFF0000), F32))
    return chunks


def _adaln_kernel(c_ref, w_ref, b_ref, o_ref):
    c = c_ref[...]
    ca = (c * jax.nn.sigmoid(c)).astype(BF16)
    o_ref[...] = jnp.dot(ca, w_ref[0].astype(BF16), preferred_element_type=F32) + b_ref[...]


def _adaln(c, w_ada, b_ada, tn=1024):
    bsz, d = c.shape
    n = w_ada.shape[-1]
    return pl.pallas_call(
        _adaln_kernel,
        grid=(n // tn,),
        in_specs=[pl.BlockSpec((bsz, d), lambda j: (0, 0)),
                  pl.BlockSpec((1, d, tn), lambda j: (0, 0, j)),
                  pl.BlockSpec((1, tn), lambda j: (0, j))],
        out_specs=pl.BlockSpec((bsz, tn), lambda j: (0, j)),
        out_shape=jax.ShapeDtypeStruct((bsz, n), F32),
        compiler_params=_params(("arbitrary",)),
        name="adaln",
    )(c, w_ada, b_ada)


def _qkv_kernel(x_ref, mod_ref, w_ref, o_ref):
    shift = mod_ref[0, 0:1, :]
    scale = mod_ref[0, 1:2, :]
    h = (x_ref[0] * (1.0 + scale) + shift).astype(BF16)
    o_ref[0] = jnp.dot(h, w_ref[...], preferred_element_type=F32).astype(BF16)


def _qkv(x, mod3, w_in_bf, tm=512, nj=2):
    bsz, s, d = x.shape
    n = w_in_bf.shape[1]
    tn = n // nj
    tm = min(tm, s)
    return pl.pallas_call(
        _qkv_kernel,
        grid=(nj, bsz, s // tm),
        in_specs=[pl.BlockSpec((1, tm, d), lambda j, b, i: (b, i, 0)),
                  pl.BlockSpec((1, 6, d), lambda j, b, i: (b, 0, 0)),
                  pl.BlockSpec((d, tn), lambda j, b, i: (0, j))],
        out_specs=pl.BlockSpec((1, tm, tn), lambda j, b, i: (b, i, j)),
        out_shape=jax.ShapeDtypeStruct((bsz, s, n), BF16),
        compiler_params=_params(("arbitrary", "arbitrary", "arbitrary")),
        name="qkv",
    )(x, mod3, w_in_bf)


def _bucket_map():
    qi = np.arange(WINDOW)[:, None]
    kj = np.arange(2 * WINDOW)[None, :]
    dist = qi + WINDOW - kj
    n = np.maximum(dist, 0)
    max_exact = NUM_BUCKETS // 2
    ratio = np.maximum(n, max_exact).astype(np.float32) / np.float32(max_exact)
    large = max_exact + (np.log(ratio) / np.float32(math.log(MAX_DISTANCE / max_exact))
                         * np.float32(NUM_BUCKETS - max_exact)).astype(np.int32)
    large = np.minimum(large, NUM_BUCKETS - 1)
    bucket = np.where(n < max_exact, n, large)
    band = (dist >= 0) & (dist < WINDOW)
    return np.where(band, bucket, -1).astype(np.int32)


def _swa_bias_kernel(rb_ref, bucket_ref, o_ref):
    first = pl.program_id(0) == 0
    bucket = bucket_ref[...]
    col = lax.broadcasted_iota(I32, bucket.shape, 1)
    hidden = jnp.logical_and(first, col < WINDOW)
    for h in range(o_ref.shape[1]):
        acc = jnp.full(bucket.shape, NEG_INF, F32)
        for b in range(NUM_BUCKETS):
            acc = jnp.where(bucket == b, rb_ref[b, h] * LOG2E, acc)
        o_ref[0, h] = jnp.where(hidden, NEG_INF, acc)


def _swa_bias(rel_bias):
    nh = rel_bias.shape[1]
    bucket = jnp.asarray(_bucket_map())
    return pl.pallas_call(
        _swa_bias_kernel,
        grid=(2,),
        in_specs=[pl.BlockSpec(memory_space=pltpu.SMEM),
                  pl.BlockSpec((WINDOW, 2 * WINDOW), lambda v: (0, 0))],
        out_specs=pl.BlockSpec((1, nh, WINDOW, 2 * WINDOW), lambda v: (v, 0, 0, 0)),
        out_shape=jax.ShapeDtypeStruct((2, nh, WINDOW, 2 * WINDOW), F32),
        compiler_params=_params(("arbitrary",)),
        name="swa_bias",
    )(rel_bias, bucket)


def _swa_kernel(sink_ref, q_ref, kvc_ref, kvp_ref, bias_ref, o_ref, *, n_heads):
    group = n_heads // KV_A
    kv = jnp.concatenate([kvp_ref[0], kvc_ref[0]], axis=0)
    lane = lax.broadcasted_iota(I32, (2 * WINDOW, LANES), 1)
    low = lane < HEAD_DIM

    def halves(pair):
        zero = jnp.zeros_like(pair)
        return [jnp.where(low, pair, zero), jnp.where(low, zero, pair)]

    kz = halves(kv[:, 0:LANES])
    vz = halves(kv[:, LANES:2 * LANES])

    n_pairs = n_heads // KV_A
    for p0 in range(0, n_pairs, SWA_PAIRS):
        pairs = range(p0, min(p0 + SWA_PAIRS, n_pairs))
        heads = [(p, g) for p in pairs for g in range(KV_A)]
        logits, e, den, o = {}, {}, {}, {}
        for p, g in heads:
            qp = q_ref[0, :, p * LANES:(p + 1) * LANES]
            s = lax.dot_general(qp, kz[g], (((1,), (1,)), ((), ())), preferred_element_type=F32)
            logits[p, g] = s + bias_ref[0, g * group + p]
        for p, g in heads:
            sink = sink_ref[g * group + p]
            m = jnp.maximum(jnp.max(logits[p, g], axis=-1, keepdims=True), sink)
            e[p, g] = jnp.exp2(logits[p, g] - m)
            den[p, g] = jnp.sum(e[p, g], axis=-1, keepdims=True) + jnp.exp2(sink - m)
        for p, g in heads:
            o[p, g] = jnp.dot(e[p, g].astype(BF16), vz[g], preferred_element_type=F32)
        for p in pairs:
            acc = o[p, 0] * (1.0 / den[p, 0])
            for g in range(1, KV_A):
                acc = acc + o[p, g] * (1.0 / den[p, g])
            o_ref[0, :, p * LANES:(p + 1) * LANES] = acc.astype(BF16)


def _swa(qkv, sinks, bias, d_a, d_b):
    bsz, s, _ = qkv.shape
    n_heads = d_a // HEAD_DIM
    q_blk = 3 * d_b // d_a
    kv_blk = (3 * d_b + d_a) // (2 * LANES)
    return pl.pallas_call(
        functools.partial(_swa_kernel, n_heads=n_heads),
        grid=(bsz, s // WINDOW),
        in_specs=[pl.BlockSpec(memory_space=pltpu.SMEM),
                  pl.BlockSpec((1, WINDOW, d_a), lambda b, i: (b, i, q_blk)),
                  pl.BlockSpec((1, WINDOW, 2 * LANES), lambda b, i: (b, i, kv_blk)),
                  pl.BlockSpec((1, WINDOW, 2 * LANES),
                               lambda b, i: (b, jnp.maximum(i - 1, 0), kv_blk)),
                  pl.BlockSpec((1, n_heads, WINDOW, 2 * WINDOW),
                               lambda b, i: (jnp.minimum(i, 1), 0, 0, 0))],
        out_specs=pl.BlockSpec((1, WINDOW, d_a), lambda b, i: (b, i, 0)),
        out_shape=jax.ShapeDtypeStruct((bsz, s, d_a), BF16),
        compiler_params=_params(("arbitrary", "arbitrary")),
        name="swa",
    )(sinks, qkv, qkv, qkv, bias)


def _suffix_matrix():
    j = np.arange(Q_BLOCK)[:, None]
    s = np.arange(Q_BLOCK)[None, :]
    return np.concatenate([(j > s), np.ones((Q_BLOCK, Q_BLOCK), bool)], axis=1).astype(np.float32)


def _sb_kernel(q_ref, k_ref, v_ref, lt_ref, o_ref, acc_ref, carry_ref, *, group):
    i = pl.program_id(2)
    lane = lax.broadcasted_iota(I32, (Q_BLOCK, LANES), 1)
    low = lane < HEAD_DIM
    row = lax.broadcasted_iota(I32, (2 * Q_BLOCK, Q_BLOCK), 0)
    col = lax.broadcasted_iota(I32, (2 * Q_BLOCK, Q_BLOCK), 1)
    strict = col < jnp.where(row >= Q_BLOCK, row - Q_BLOCK, row)
    sign = jnp.uint32(0x80000000)

    qh = []
    for g in range(group):
        q = q_ref[0, :, g * LANES:(g + 1) * LANES]
        zero = jnp.zeros_like(q)
        qh.append([jnp.where(low, q, zero), jnp.where(low, zero, q)])

    gs = range(group)

    def scores(j, g, lo, hi):
        return lax.dot_general(jnp.concatenate([qh[g][0][lo:hi], qh[g][1][lo:hi]], axis=0),
                               k_ref[0, pl.ds(pl.multiple_of(j * Q_BLOCK, Q_BLOCK), Q_BLOCK),
                                     g * LANES:(g + 1) * LANES],
                               (((1,), (1,)), ((), ())), preferred_element_type=F32)

    def softplus2(z):
        neg_abs = pltpu.bitcast(pltpu.bitcast(z, jnp.uint32) | sign, F32)
        return jnp.maximum(z, 0.0) + jnp.log2(1.0 + jnp.exp2(neg_abs))

    def suffix(sp):
        return jnp.dot(sp.astype(BF16), lt_ref[...], preferred_element_type=F32)

    def weighted_values(a, j, g, rows):
        a = a.astype(BF16)
        a2 = jnp.concatenate([a[:rows], a[rows:]], axis=1)
        vj = v_ref[0, pl.ds(pl.multiple_of(j * Q_BLOCK, Q_BLOCK), Q_BLOCK), g * LANES:(g + 1) * LANES]
        vzero = jnp.zeros_like(vj)
        vz = jnp.concatenate([jnp.where(low, vj, vzero), jnp.where(low, vzero, vj)], axis=0)
        return jnp.dot(a2, vz, preferred_element_type=F32)

    t = SB_TAIL_ROWS

    def carry_mins(carry_min, lo, hi):
        n = hi - lo
        n_top = max(min(hi, t) - lo, 0)
        top = jnp.min(jnp.minimum(carry_min[:n_top], carry_min[n:n + n_top])) if n_top else None
        rest = jnp.min(jnp.minimum(carry_min[n_top:n], carry_min[n + n_top:])) if n_top < n else None
        return top, rest

    def first_blocks(n_before):
        pieces = [(i, 0, Q_BLOCK)]
        if n_before >= 1:
            pieces.append((i - 1, 0, Q_BLOCK))
        if n_before >= 2:
            pieces.append((i - 2, 0, t))
        z = {(b, g): scores(j, g, lo, hi) for b, (j, lo, hi) in enumerate(pieces) for g in gs}
        sp = {}
        for (b, g), zz in z.items():
            s = softplus2(zz)
            sp[b, g] = jnp.where(strict, s, 0.0) if b == 0 else s
        cs = {bg: suffix(s) for bg, s in sp.items()}
        a, carry_min, head_min = {}, None, None
        for g in gs:
            a[0, g] = jnp.where(strict, jnp.exp2(z[0, g] - sp[0, g] - cs[0, g][:, :Q_BLOCK]), 0.0)
            carry = cs[0, g][:, Q_BLOCK:]
            if n_before >= 1:
                a[1, g] = jnp.exp2(z[1, g] - sp[1, g] - cs[1, g][:, :Q_BLOCK] - carry)
                carry = carry + cs[1, g][:, Q_BLOCK:]
            carry_ref[g, 0] = carry[:Q_BLOCK]
            carry_ref[g, 1] = carry[Q_BLOCK:]
            carry_min = carry if carry_min is None else jnp.minimum(carry_min, carry)
            if n_before >= 2:
                head = jnp.concatenate([carry[:t], carry[Q_BLOCK:Q_BLOCK + t]], axis=0)
                a[2, g] = jnp.exp2(z[2, g] - sp[2, g] - cs[2, g][:, :Q_BLOCK] - head)
                head = head + cs[2, g][:, Q_BLOCK:]
                carry_ref[g, 0, :t] = head[:t]
                carry_ref[g, 1, :t] = head[t:]
                head_min = head if head_min is None else jnp.minimum(head_min, head)
        for g in gs:
            acc = weighted_values(a[0, g], i, g, Q_BLOCK)
            if n_before >= 1:
                acc = acc + weighted_values(a[1, g], i - 1, g, Q_BLOCK)
            acc_ref[g] = acc
            if n_before >= 2:
                acc_ref[g, :t] += weighted_values(a[2, g], i - 2, g, t)
        top, rest = carry_mins(carry_min, 0, Q_BLOCK)
        if n_before >= 2:
            top = carry_mins(head_min, 0, t)[0]
        return top, rest

    def block(j, lo, hi):
        n = hi - lo
        z = [scores(j, g, lo, hi) for g in gs]
        sp = [softplus2(zz) for zz in z]
        cs = [suffix(s) for s in sp]
        a, carry_min = [], None
        for g in gs:
            carry = jnp.concatenate([carry_ref[g, 0, lo:hi], carry_ref[g, 1, lo:hi]], axis=0)
            a.append(jnp.exp2(z[g] - sp[g] - cs[g][:, :Q_BLOCK] - carry))
            carry = carry + cs[g][:, Q_BLOCK:]
            carry_ref[g, 0, lo:hi] = carry[:n]
            carry_ref[g, 1, lo:hi] = carry[n:]
            carry_min = carry if carry_min is None else jnp.minimum(carry_min, carry)
        for g in gs:
            acc_ref[g, lo:hi] += weighted_values(a[g], j, g, n)
        return carry_mins(carry_min, lo, hi)

    def two_before():
        top, rest = first_blocks(2)
        rest = lax.cond(rest < SB_SKIP_BITS, lambda: block(i - 2, t, Q_BLOCK)[1], lambda: rest)
        return top, rest

    top0, rest0 = lax.cond(
        i >= 2, two_before,
        lambda: lax.cond(i == 1, lambda: first_blocks(1), lambda: first_blocks(0)))

    def more(state):
        jj, top, rest = state
        return jnp.logical_and(jj < i - 2, jnp.minimum(top, rest) < SB_SKIP_BITS)

    def body(state):
        jj, _, rest = state
        j = i - 3 - jj

        def tail_rows():
            return block(j, 0, t)[0], rest

        def all_rows():
            return block(j, 0, Q_BLOCK)

        top, rest = lax.cond(rest >= SB_SKIP_BITS, tail_rows, all_rows)
        return jj + 1, top, rest

    lax.while_loop(more, body, (jnp.int32(0), top0, rest0))
    for g in range(group):
        o_ref[0, :, g * LANES:(g + 1) * LANES] = acc_ref[g].astype(BF16)


def _sb(qkv, d_b, group=SB_GROUP):
    bsz, s, _ = qkv.shape
    pairs = d_b // LANES
    ng = pairs // group
    w = group * LANES
    lt = jnp.asarray(_suffix_matrix(), BF16)
    return pl.pallas_call(
        functools.partial(_sb_kernel, group=group),
        grid=(bsz, ng, s // Q_BLOCK),
        in_specs=[pl.BlockSpec((1, Q_BLOCK, w), lambda b, p, i: (b, i, p)),
                  pl.BlockSpec((1, s, w), lambda b, p, i: (b, 0, ng + p)),
                  pl.BlockSpec((1, s, w), lambda b, p, i: (b, 0, 2 * ng + p)),
                  pl.BlockSpec((Q_BLOCK, 2 * Q_BLOCK), lambda b, p, i: (0, 0))],
        out_specs=pl.BlockSpec((1, Q_BLOCK, w), lambda b, p, i: (b, i, p)),
        out_shape=jax.ShapeDtypeStruct((bsz, s, d_b), BF16),
        scratch_shapes=[pltpu.VMEM((group, Q_BLOCK, LANES), F32),
                        pltpu.VMEM((group, 2, Q_BLOCK, LANES), F32)],
        compiler_params=_params(("arbitrary", "arbitrary", "arbitrary")),
        name="sb",
    )(qkv, qkv, qkv, lt)


def _layer_norm(y, g, b):
    mu = jnp.mean(y, axis=-1, keepdims=True)
    yc = y - mu
    var = jnp.mean(yc * yc, axis=-1, keepdims=True)
    return yc * lax.rsqrt(var + EPS) * g + b


def _rms(o, g):
    return o * lax.rsqrt(jnp.mean(o * o, axis=-1, keepdims=True) + EPS) * g


def _mix_ln1_kernel(oa_ref, ob_ref, x_ref, mod_ref, na_ref, nb_ref, wo_ref, g_ref, b_ref,
                    wrc_ref, br_ref, x1_ref, h2_ref, lg_ref, *, d_a, parts):
    hm = x_ref.shape[1] // parts
    rows = [pl.ds(p * hm, hm) for p in range(parts)]
    gate1 = mod_ref[0, 2:3, :]
    shift2 = mod_ref[0, 3:4, :]
    scale2 = mod_ref[0, 4:5, :]
    ra = [_rms(oa_ref[0, r, :].astype(F32), na_ref[...]).astype(BF16) for r in rows]
    rb = [_rms(ob_ref[0, r, :].astype(F32), nb_ref[...]).astype(BF16) for r in rows]
    mix = [jnp.dot(ra[p], wo_ref[:d_a, :], preferred_element_type=F32)
           + jnp.dot(rb[p], wo_ref[d_a:, :], preferred_element_type=F32) for p in range(parts)]
    hi, lo = [], []
    for p, r in enumerate(rows):
        x1 = _layer_norm(ALPHA * x_ref[0, r, :] + (1.0 + gate1) * mix[p], g_ref[...], b_ref[...])
        x1_ref[0, r, :] = x1
        h2 = x1 * (1.0 + scale2) + shift2
        _store_row_tiles(h2_ref.at[0, pl.ds(p * hm * ROW_SUB, hm * ROW_SUB)], h2)
        hi.append(h2.astype(BF16))
        lo.append((h2 - hi[p].astype(F32)).astype(BF16))
    for p, r in enumerate(rows):
        both = jnp.dot(hi[p], wrc_ref[...], preferred_element_type=F32)
        lg_ref[0, r, :] = (both[:, :LANES] + both[:, LANES:]
                           + jnp.dot(lo[p], wrc_ref[:, :LANES], preferred_element_type=F32)
                           + br_ref[...])


def _mix_ln1(o_a, o_b, x, mod3, norm_a, norm_b, w_out_bf, ln_g, ln_b, wr_cat, b_r, tm=512, parts=2):
    bsz, s, d = x.shape
    d_a = o_a.shape[-1]
    d_b = o_b.shape[-1]
    tm = min(tm, s)
    row = lambda b, i: (b, i, 0)
    const2 = lambda b, i: (0, 0)
    once = pl.Buffered(1)
    return pl.pallas_call(
        functools.partial(_mix_ln1_kernel, d_a=d_a, parts=parts),
        grid=(bsz, s // tm),
        in_specs=[pl.BlockSpec((1, tm, d_a), row),
                  pl.BlockSpec((1, tm, d_b), row),
                  pl.BlockSpec((1, tm, d), row),
                  pl.BlockSpec((1, 6, d), lambda b, i: (b, 0, 0)),
                  pl.BlockSpec((1, d_a), const2),
                  pl.BlockSpec((1, d_b), const2),
                  pl.BlockSpec((d_a + d_b, d), const2, pipeline_mode=once),
                  pl.BlockSpec((1, d), const2),
                  pl.BlockSpec((1, d), const2),
                  pl.BlockSpec((d, 2 * LANES), const2, pipeline_mode=once),
                  pl.BlockSpec((1, LANES), const2)],
        out_specs=[pl.BlockSpec((1, tm, d), row),
                   pl.BlockSpec((1, tm * ROW_SUB, LANES), row),
                   pl.BlockSpec((1, tm, LANES), row)],
        out_shape=[jax.ShapeDtypeStruct((bsz, s, d), F32),
                   jax.ShapeDtypeStruct((bsz, s * ROW_SUB, LANES), jnp.uint32),
                   jax.ShapeDtypeStruct((bsz, s, LANES), F32)],
        compiler_params=_params(("arbitrary", "arbitrary")),
        name="mix_ln1",
    )(o_a, o_b, x, mod3, norm_a, norm_b, w_out_bf, ln_g, ln_b, wr_cat, b_r)


def _route_kernel(lg_ref, tri_ref, sel_ref, gate_ref, cnt_ref, base_ref):
    step = pl.program_id(0)

    @pl.when(step == 0)
    def _():
        base_ref[...] = jnp.zeros_like(base_ref)

    lg = lg_ref[...]
    tm = lg.shape[0]
    lane = lax.broadcasted_iota(I32, (tm, LANES), 1)
    big = jnp.int32(2 * LANES)
    glog = jnp.where(lane < N_GROUPS, lg, -jnp.inf)
    gmax = jnp.max(glog, axis=-1, keepdims=True)
    g_sel = jnp.min(jnp.where(glog == gmax, lane, big), axis=-1, keepdims=True)
    p_g = 1.0 / jnp.sum(jnp.exp(glog - gmax), axis=-1, keepdims=True)
    lo = N_GROUPS + g_sel * EXPERTS_PER_GROUP
    in_grp = jnp.logical_and(lane >= lo, lane < lo + EXPERTS_PER_GROUP)
    el = jnp.where(in_grp, lg, -jnp.inf)
    v1 = jnp.max(el, axis=-1, keepdims=True)
    i1 = jnp.min(jnp.where(el == v1, lane, big), axis=-1, keepdims=True)
    el2 = jnp.where(lane == i1, -jnp.inf, el)
    v2 = jnp.max(el2, axis=-1, keepdims=True)
    i2 = jnp.min(jnp.where(el2 == v2, lane, big), axis=-1, keepdims=True)
    r = jnp.exp(v2 - v1)
    w1 = 1.0 / (1.0 + r)
    g1 = p_g * w1
    g2 = p_g * (r * w1)
    e1 = i1 - N_GROUPS
    e2 = i2 - N_GROUPS
    oh1 = (lane == e1)
    oh2 = (lane == e2)
    occ = oh1.astype(F32) + oh2.astype(F32)
    before = jnp.dot(tri_ref[...], occ.astype(BF16), preferred_element_type=F32) + base_ref[...]
    r1 = jnp.sum(jnp.where(oh1, before, 0.0), axis=-1, keepdims=True)
    r2 = jnp.sum(jnp.where(oh2, before, 0.0), axis=-1, keepdims=True)
    base_ref[...] += jnp.sum(occ, axis=0, keepdims=True)
    cnt_ref[...] = base_ref[...]
    sel = jnp.where(lane == 0, e1, jnp.where(lane == 1, e2, 0))
    sel = jnp.where(lane == 2, r1.astype(I32), jnp.where(lane == 3, r2.astype(I32), sel))
    sel_ref[...] = sel
    gate_ref[...] = jnp.where(lane == 0, g1, jnp.where(lane == 1, g2, 0.0))


def _route(logits, tm=1024):
    t = logits.shape[0]
    tm = min(tm, t)
    tri = jnp.asarray(np.tril(np.ones((tm, tm), np.float32), -1), BF16)
    return pl.pallas_call(
        _route_kernel,
        grid=(t // tm,),
        in_specs=[pl.BlockSpec((tm, LANES), lambda i: (i, 0)),
                  pl.BlockSpec((tm, tm), lambda i: (0, 0))],
        out_specs=[pl.BlockSpec((tm, LANES), lambda i: (i, 0)),
                   pl.BlockSpec((tm, LANES), lambda i: (i, 0)),
                   pl.BlockSpec((1, LANES), lambda i: (0, 0))],
        out_shape=[jax.ShapeDtypeStruct((t, LANES), I32),
                   jax.ShapeDtypeStruct((t, LANES), F32),
                   jax.ShapeDtypeStruct((1, LANES), F32)],
        scratch_shapes=[pltpu.VMEM((1, LANES), F32)],
        compiler_params=_params(("arbitrary",)),
        name="route",
    )(logits, tri)


def _lane_prefix(x, lane):
    shift = 1
    while shift < LANES:
        x = x + jnp.where(lane >= shift, pltpu.roll(x, shift, axis=1), 0)
        shift *= 2
    return x


def _dest_kernel(sel_ref, cnt_ref, dest_ref, blk_ref, plan_ref, *, n_blk_pad):
    tm = sel_ref.shape[0]
    lane1 = lax.broadcasted_iota(I32, (8, LANES), 1)
    cnt = jnp.broadcast_to(cnt_ref[...].astype(I32), (8, LANES))
    cnt = jnp.where(lane1 < N_EXPERTS, cnt, 0)
    padded = jnp.bitwise_and(cnt + (MOE_TM - 1), -MOE_TM)
    pend = _lane_prefix(padded, lane1)
    pstart = (pend - padded)[0:1, :]
    sel = sel_ref[...]
    lane = lax.broadcasted_iota(I32, (tm, LANES), 1)
    e1 = sel[:, 0:1]
    e2 = sel[:, 1:2]
    d1 = jnp.sum(jnp.where(lane == e1, pstart, 0), axis=-1, keepdims=True) + sel[:, 2:3]
    d2 = jnp.sum(jnp.where(lane == e2, pstart, 0), axis=-1, keepdims=True) + sel[:, 3:4]
    dest_ref[...] = jnp.where(lane == 0, d1, jnp.where(lane == 1, d2, 0))

    @pl.when(pl.program_id(0) == 0)
    def _():
        brow = lax.broadcasted_iota(I32, (n_blk_pad, LANES), 0) * MOE_TM
        blane = lax.broadcasted_iota(I32, (n_blk_pad, LANES), 1)
        ended = jnp.logical_and(blane < N_EXPERTS, pend[0:1, :] <= brow)
        be = jnp.minimum(jnp.sum(ended.astype(I32), axis=-1, keepdims=True), N_EXPERTS - 1)
        blk_ref[...] = jnp.broadcast_to(be, (n_blk_pad, LANES))
        sub = lax.broadcasted_iota(I32, (8, LANES), 0)
        used = jnp.max(pend, axis=-1, keepdims=True) >> (MOE_TM.bit_length() - 1)
        plan_ref[...] = jnp.where(sub == 0, pend - padded + cnt,
                                  jnp.where(sub == 1, padded - cnt, used))


def _dest(sel, counts, n_blk, tm=1024):
    t = sel.shape[0]
    tm = min(tm, t)
    n_blk_pad = -(-n_blk // 8) * 8
    return pl.pallas_call(
        functools.partial(_dest_kernel, n_blk_pad=n_blk_pad),
        grid=(t // tm,),
        in_specs=[pl.BlockSpec((tm, LANES), lambda i: (i, 0)),
                  pl.BlockSpec((1, LANES), lambda i: (0, 0))],
        out_specs=[pl.BlockSpec((tm, LANES), lambda i: (i, 0)),
                   pl.BlockSpec((n_blk_pad, LANES), lambda i: (0, 0)),
                   pl.BlockSpec((8, LANES), lambda i: (0, 0))],
        out_shape=[jax.ShapeDtypeStruct((t, LANES), I32),
                   jax.ShapeDtypeStruct((n_blk_pad, LANES), I32),
                   jax.ShapeDtypeStruct((8, LANES), I32)],
        compiler_params=_params(("arbitrary",)),
        name="dest",
    )(sel, counts)


def _invert_kernel(dest_ref, plan_ref, tok_ref):
    n_slots = tok_ref.shape[0]
    n_tok = dest_ref.shape[0] // 2

    def pads(e, carry):
        def one(m, c):
            tok_ref[m] = 0
            return c

        lax.fori_loop(plan_ref[e], plan_ref[e] + plan_ref[N_EXPERTS + e], one, 0)
        return carry

    lax.fori_loop(0, N_EXPERTS, pads, 0)

    def unused(c, carry):
        for u in range(ROW_UNROLL):
            tok_ref[c * ROW_UNROLL + u] = 0
        return carry

    lax.fori_loop(plan_ref[2 * N_EXPERTS] * (MOE_TM // ROW_UNROLL), n_slots // ROW_UNROLL, unused, 0)

    def fill(c, carry):
        slots = [dest_ref[2 * ROW_UNROLL * c + m] for m in range(2 * ROW_UNROLL)]
        for m, slot in enumerate(slots):
            tok_ref[slot] = (c * ROW_UNROLL + m // 2) * ROW_SUB
        return carry

    lax.fori_loop(0, n_tok // ROW_UNROLL, fill, 0)


def _invert(dest_flat, plan_flat, m_pad):
    return pl.pallas_call(
        _invert_kernel,
        grid_spec=pltpu.PrefetchScalarGridSpec(
            num_scalar_prefetch=2,
            grid=(1,),
            in_specs=[],
            out_specs=pl.BlockSpec(memory_space=pltpu.SMEM)),
        out_shape=jax.ShapeDtypeStruct((m_pad,), I32),
        compiler_params=_params(("arbitrary",)),
        name="invert",
    )(dest_flat, plan_flat)


def _experts_kernel(blk_ref, tok_ref, h_hbm, wg_hbm, wu_hbm, wd_hbm, ys_ref,
                    xbuf0, xbuf1, xbuf2, wg_f32, wu_f32, wd_f32, wg_bf, wu_bf, wd_bf,
                    slot_ref, sems, xsems):
    xbufs = (xbuf0, xbuf1, xbuf2)
    i = pl.program_id(0)
    n_blk = pl.num_programs(0)
    n_used = blk_ref[n_blk]
    e = blk_ref[i]
    in_use = i < n_used
    first_of_run = jnp.logical_and(
        in_use, jnp.logical_or(i == 0, blk_ref[jnp.maximum(i - 1, 0)] != e))

    def row_copy(blk, r, sl):
        src = pl.multiple_of(tok_ref[blk * MOE_TM + r], ROW_SUB)
        return pltpu.make_async_copy(
            h_hbm.at[pl.ds(src, ROW_SUB)],
            xbufs[sl].at[pl.ds(pl.multiple_of(r * ROW_SUB, ROW_SUB), ROW_SUB)], xsems.at[sl])

    def wait_rows(sl):
        pltpu.make_async_copy(h_hbm.at[pl.ds(0, MOE_TM * ROW_SUB)], xbufs[sl], xsems.at[sl]).wait()

    def weight_copies(expert, sl):
        return [pltpu.make_async_copy(wg_hbm.at[0, expert], wg_f32.at[sl], sems.at[sl]),
                pltpu.make_async_copy(wu_hbm.at[0, expert], wu_f32.at[sl], sems.at[sl]),
                pltpu.make_async_copy(wd_hbm.at[0, expert], wd_f32.at[sl], sems.at[sl])]

    @pl.when(i == 0)
    def _():
        slot_ref[0] = 0
        for c in weight_copies(e, 0):
            c.start()
        for ahead in range(2):
            blk = jnp.minimum(ahead, n_used - 1)

            def start(c, carry, blk=blk, ahead=ahead):
                for u in range(ROW_UNROLL):
                    row_copy(blk, c * ROW_UNROLL + u, ahead).start()
                return carry

            lax.fori_loop(0, MOE_TM // ROW_UNROLL, start, 0)

    @pl.when(first_of_run)
    def _():
        sl = slot_ref[0]
        for c in weight_copies(e, sl):
            c.wait()
        wg_bf[...] = wg_f32[sl].astype(BF16)
        wu_bf[...] = wu_f32[sl].astype(BF16)
        wd_bf[...] = wd_f32[sl].astype(BF16)
        nxt = lax.while_loop(
            lambda j: jnp.logical_and(j < n_used, blk_ref[jnp.minimum(j, n_blk - 1)] == e),
            lambda j: j + 1, i + 1)

        @pl.when(nxt < n_used)
        def _():
            for c in weight_copies(blk_ref[nxt], 1 - sl):
                c.start()

        slot_ref[0] = 1 - sl

    def run(sl):
        wait_rows(sl)
        ahead_blk = jnp.minimum(i + 2, n_used - 1)
        for r in range(MOE_TM):
            row_copy(ahead_blk, r, (sl + 2) % 3).start()
        xb = jnp.concatenate([c.astype(BF16) for c in _load_row_tiles(xbufs[sl], MOE_TM)], axis=1)
        g = jnp.dot(xb, wg_bf[...], preferred_element_type=F32)
        u = jnp.dot(xb, wu_bf[...], preferred_element_type=F32)
        hmid = (g * jax.nn.sigmoid(g) * u).astype(BF16)
        _store_row_tiles(ys_ref, jnp.dot(hmid, wd_bf[...], preferred_element_type=F32))

        @pl.when(i == n_used - 1)
        def _():
            wait_rows((sl + 1) % 3)
            wait_rows((sl + 2) % 3)

    for sl in range(3):
        pl.when(jnp.logical_and(in_use, lax.rem(i, 3) == sl))(functools.partial(run, sl))

    @pl.when(jnp.logical_not(in_use))
    def _():
        ys_ref[...] = jnp.zeros_like(ys_ref)


def _experts(blk_expert, slot_tok, h2p, w_gate, w_up, w_down):
    d, de = w_gate.shape[-2:]
    m_pad = slot_tok.shape[0]
    xbuf = pltpu.VMEM((MOE_TM * ROW_SUB, LANES), h2p.dtype)
    return pl.pallas_call(
        _experts_kernel,
        grid_spec=pltpu.PrefetchScalarGridSpec(
            num_scalar_prefetch=2,
            grid=(m_pad // MOE_TM,),
            in_specs=[pl.BlockSpec(memory_space=pl.ANY),
                      pl.BlockSpec(memory_space=pl.ANY),
                      pl.BlockSpec(memory_space=pl.ANY),
                      pl.BlockSpec(memory_space=pl.ANY)],
            out_specs=pl.BlockSpec((MOE_TM * ROW_SUB, LANES), lambda i, blk, tok: (i, 0)),
            scratch_shapes=[xbuf, xbuf, xbuf,
                            pltpu.VMEM((2, d, de), F32),
                            pltpu.VMEM((2, d, de), F32),
                            pltpu.VMEM((2, de, d), F32),
                            pltpu.VMEM((d, de), BF16),
                            pltpu.VMEM((d, de), BF16),
                            pltpu.VMEM((de, d), BF16),
                            pltpu.SMEM((1,), I32),
                            pltpu.SemaphoreType.DMA((2,)),
                            pltpu.SemaphoreType.DMA((3,))]),
        out_shape=jax.ShapeDtypeStruct((m_pad * ROW_SUB, LANES), jnp.uint32),
        compiler_params=_params(("arbitrary",)),
        name="experts",
    )(blk_expert, slot_tok, h2p, w_gate, w_up, w_down)


def _combine_kernel(dest_ref, ys_ref, gate_ref, x1_ref, mod_ref, g_ref, b_ref, o_ref, *scratch):
    ybufs, sems = scratch[:COMBINE_SLOTS], scratch[COMBINE_SLOTS]
    tm = x1_ref.shape[1]
    n_steps = pl.num_programs(0) * pl.num_programs(1)
    step = pl.program_id(0) * pl.num_programs(1) + pl.program_id(1)

    def row_copy(base, r, k, sl):
        src = pl.multiple_of(dest_ref[base + 2 * r + k] * ROW_SUB, ROW_SUB)
        return pltpu.make_async_copy(ys_ref.at[pl.ds(src, ROW_SUB)],
                                     ybufs[sl].at[k, pl.ds(pl.multiple_of(r * ROW_SUB, ROW_SUB), ROW_SUB)],
                                     sems.at[sl])

    def wait_slot(sl):
        for k in range(2):
            pltpu.make_async_copy(ys_ref.at[pl.ds(0, tm * ROW_SUB)], ybufs[sl].at[k], sems.at[sl]).wait()

    @pl.when(step == 0)
    def _():
        for ahead in range(COMBINE_SLOTS - 1):
            base = jnp.minimum(ahead, n_steps - 1) * (2 * tm)

            def start(c, carry, base=base, ahead=ahead):
                for u in range(ROW_UNROLL):
                    for k in range(2):
                        row_copy(base, c * ROW_UNROLL + u, k, ahead).start()
                return carry

            lax.fori_loop(0, tm // ROW_UNROLL, start, 0)

    def run(sl):
        wait_slot(sl)
        ahead_base = jnp.minimum(step + COMBINE_SLOTS - 1, n_steps - 1) * (2 * tm)
        ahead_slot = (sl + COMBINE_SLOTS - 1) % COMBINE_SLOTS
        for r in range(tm):
            for k in range(2):
                row_copy(ahead_base, r, k, ahead_slot).start()

        gates = gate_ref[0]
        g0 = gates[:, 0:1]
        g1 = gates[:, 1:2]
        y0 = _load_row_tiles(ybufs[sl].at[0], tm)
        y1 = _load_row_tiles(ybufs[sl].at[1], tm)
        ffn = jnp.concatenate([g0 * a + g1 * b for a, b in zip(y0, y1)], axis=1)
        gate2 = mod_ref[0, 5:6, :]
        o_ref[0] = _layer_norm(ALPHA * x1_ref[0] + (1.0 + gate2) * ffn, g_ref[...], b_ref[...])

        @pl.when(step == n_steps - 1)
        def _():
            for ahead in range(1, COMBINE_SLOTS):
                wait_slot((sl + ahead) % COMBINE_SLOTS)

    for sl in range(COMBINE_SLOTS):
        pl.when(lax.rem(step, COMBINE_SLOTS) == sl)(functools.partial(run, sl))


def _combine(dest_flat, ys, gates3, x1, mod3, ln_g, ln_b, tm=256):
    bsz, s, d = x1.shape
    tm = min(tm, s)
    return pl.pallas_call(
        _combine_kernel,
        grid_spec=pltpu.PrefetchScalarGridSpec(
            num_scalar_prefetch=1,
            grid=(bsz, s // tm),
            in_specs=[pl.BlockSpec(memory_space=pl.ANY),
                      pl.BlockSpec((1, tm, LANES), lambda b, i, dest: (b, i, 0)),
                      pl.BlockSpec((1, tm, d), lambda b, i, dest: (b, i, 0)),
                      pl.BlockSpec((1, 6, d), lambda b, i, dest: (b, 0, 0)),
                      pl.BlockSpec((1, d), lambda b, i, dest: (0, 0)),
                      pl.BlockSpec((1, d), lambda b, i, dest: (0, 0))],
            out_specs=pl.BlockSpec((1, tm, d), lambda b, i, dest: (b, i, 0)),
            scratch_shapes=[pltpu.VMEM((2, tm * ROW_SUB, LANES), jnp.uint32)] * COMBINE_SLOTS
            + [pltpu.SemaphoreType.DMA((COMBINE_SLOTS,))]),
        out_shape=jax.ShapeDtypeStruct((bsz, s, d), F32),
        compiler_params=_params(("arbitrary", "arbitrary")),
        name="combine",
    )(dest_flat, ys, gates3, x1, mod3, ln_g, ln_b)


def kernel(x, c, w_in, w_out, sinks, rel_bias, norm_a, norm_b, w_ada, b_ada, ln1_g, ln1_b,
           ln2_g, ln2_b, w_grp, b_grp, w_rtr, b_rtr, w_gate, w_up, w_down):
    bsz, s, d = x.shape
    t = bsz * s
    d_a = norm_a.shape[-1]
    d_b = norm_b.shape[-1]

    mod3 = _adaln(c, w_ada, b_ada).reshape(bsz, 6, d)

    assert KV_A * HEAD_DIM == LANES
    kv_w = 2 * KV_A * HEAD_DIM
    group_a = d_a // HEAD_DIM // KV_A
    head_order = [g * group_a + p for p in range(group_a) for g in range(KV_A)]
    perm_a = np.concatenate([np.arange(HEAD_DIM) + HEAD_DIM * h for h in head_order])
    w0 = w_in[0]
    w_in_bf = jnp.concatenate(
        [w0[:, d_a + kv_w:d_a + kv_w + d_b] * (ATTN_SCALE * LOG2E),
         w0[:, d_a + kv_w + d_b:],
         w0[:, :d_a][:, perm_a] * (ATTN_SCALE * LOG2E),
         w0[:, d_a:d_a + kv_w]], axis=1).astype(BF16)
    qkv = _qkv(x, mod3, w_in_bf)
    norm_a = norm_a[:, perm_a]
    w_out_bf = jnp.concatenate([w_out[0][:d_a][perm_a], w_out[0][d_a:]], axis=0).astype(BF16)

    o_a = _swa(qkv, sinks[0] * LOG2E, _swa_bias(rel_bias), d_a, d_b)
    o_b = _sb(qkv, d_b)

    w_r = jnp.concatenate([w_grp[0], w_rtr[0]], axis=1)
    w_r = jnp.pad(w_r, ((0, 0), (0, LANES - w_r.shape[1])))
    b_r = jnp.pad(jnp.concatenate([b_grp[0], b_rtr[0]]), (0, LANES - N_GROUPS - N_EXPERTS))[None, :]
    wr_hi = w_r.astype(BF16)
    wr_lo = (w_r - wr_hi.astype(F32)).astype(BF16)
    assert d == 2 * LANES * ROW_SUB, "row tiles hold 256 * ROW_SUB features"
    x1, h2, logits = _mix_ln1(o_a, o_b, x, mod3, norm_a, norm_b, w_out_bf,
                              ln1_g, ln1_b, jnp.concatenate([wr_hi, wr_lo], axis=1), b_r)

    sel, gates, counts = _route(logits.reshape(t, LANES))
    m_pad = 2 * t + N_EXPERTS * MOE_TM
    n_blk = m_pad // MOE_TM
    dest, blk, plan = _dest(sel, counts, n_blk)
    dest_flat = dest[:, :2].reshape(2 * t)
    blk_expert = jnp.concatenate([blk[:n_blk, 0], plan[2, :1]])
    plan_flat = jnp.concatenate([plan[0, :N_EXPERTS], plan[1, :N_EXPERTS], plan[2, :1]])
    slot_tok = _invert(dest_flat, plan_flat, m_pad)
    ys = _experts(blk_expert, slot_tok, h2.reshape(t * ROW_SUB, LANES), w_gate, w_up, w_down)
    return _combine(dest_flat, ys, gates.reshape(bsz, s, LANES), x1, mod3, ln2_g, ln2_b)
```

```python
import functools
import math

import jax
import jax.numpy as jnp
import numpy as np
from jax import lax
from jax.experimental import pallas as pl
from jax.experimental.pallas import tpu as pltpu

F32 = jnp.float32
BF16 = jnp.bfloat16
I32 = jnp.int32

HEAD_DIM = 64
KV_A = 2
NUM_BUCKETS = 32
MAX_DISTANCE = 128
WINDOW = 128
Q_BLOCK = 128
N_GROUPS = 4
EXPERTS_PER_GROUP = 8
N_EXPERTS = N_GROUPS * EXPERTS_PER_GROUP
DEPTH = 1
ALPHA = (2.0 * DEPTH) ** 0.25
ATTN_SCALE = 1.0 / math.sqrt(HEAD_DIM)
EPS = 1e-5
NEG_INF = -1e30
LOG2E = math.log2(math.e)

LANES = 128
ROW_SUB = 8
MOE_TM = 256
COMBINE_SLOTS = 3
ROW_UNROLL = 8
SWA_PAIRS = 4
SB_GROUP = 8
SB_TAIL_ROWS = 48
SB_SKIP_BITS = 150.0
VMEM_LIMIT = 48 * 1024 * 1024


def _params(sem, vmem=VMEM_LIMIT):
    return pltpu.CompilerParams(dimension_semantics=sem, vmem_limit_bytes=vmem)


def _store_row_tiles(ref_2d, y):
    n = y.shape[0]
    for s in range(ROW_SUB):
        lo = pltpu.bitcast(y[:, 2 * s * LANES:(2 * s + 1) * LANES].astype(BF16).astype(F32), jnp.uint32)
        hi = pltpu.bitcast(y[:, (2 * s + 1) * LANES:(2 * s + 2) * LANES].astype(BF16).astype(F32), jnp.uint32)
        ref_2d[pl.ds(s, n, stride=ROW_SUB), :] = hi | (lo >> 16)


def _load_row_tiles(ref_2d, n):
    chunks = []
    for s in range(ROW_SUB):
        p = ref_2d[pl.ds(s, n, stride=ROW_SUB), :]
        chunks.append(pltpu.bitcast(p << 16, F32))
        chunks.append(pltpu.bitcast(p & jnp.uint32(0xFFFF0000), F32))
    return chunks


def _adaln_kernel(c_ref, w_ref, b_ref, o_ref):
    c = c_ref[...]
    ca = (c * jax.nn.sigmoid(c)).astype(BF16)
    o_ref[...] = jnp.dot(ca, w_ref[0].astype(BF16), preferred_element_type=F32) + b_ref[...]


def _adaln(c, w_ada, b_ada, tn=1024):
    bsz, d = c.shape
    n = w_ada.shape[-1]
    return pl.pallas_call(
        _adaln_kernel,
        grid=(n // tn,),
        in_specs=[pl.BlockSpec((bsz, d), lambda j: (0, 0)),
                  pl.BlockSpec((1, d, tn), lambda j: (0, 0, j)),
                  pl.BlockSpec((1, tn), lambda j: (0, j))],
        out_specs=pl.BlockSpec((bsz, tn), lambda j: (0, j)),
        out_shape=jax.ShapeDtypeStruct((bsz, n), F32),
        compiler_params=_params(("arbitrary",)),
        name="adaln",
    )(c, w_ada, b_ada)


def _qkv_kernel(x_ref, mod_ref, w_ref, o_ref):
    shift = mod_ref[0, 0:1, :]
    scale = mod_ref[0, 1:2, :]
    h = (x_ref[0] * (1.0 + scale) + shift).astype(BF16)
    o_ref[0] = jnp.dot(h, w_ref[...], preferred_element_type=F32).astype(BF16)


def _qkv(x, mod3, w_in_bf, tm=512, nj=2):
    bsz, s, d = x.shape
    n = w_in_bf.shape[1]
    tn = n // nj
    tm = min(tm, s)
    return pl.pallas_call(
        _qkv_kernel,
        grid=(nj, bsz, s // tm),
        in_specs=[pl.BlockSpec((1, tm, d), lambda j, b, i: (b, i, 0)),
                  pl.BlockSpec((1, 6, d), lambda j, b, i: (b, 0, 0)),
                  pl.BlockSpec((d, tn), lambda j, b, i: (0, j))],
        out_specs=pl.BlockSpec((1, tm, tn), lambda j, b, i: (b, i, j)),
        out_shape=jax.ShapeDtypeStruct((bsz, s, n), BF16),
        compiler_params=_params(("arbitrary", "arbitrary", "arbitrary")),
        name="qkv",
    )(x, mod3, w_in_bf)


def _bucket_map():
    qi = np.arange(WINDOW)[:, None]
    kj = np.arange(2 * WINDOW)[None, :]
    dist = qi + WINDOW - kj
    n = np.maximum(dist, 0)
    max_exact = NUM_BUCKETS // 2
    ratio = np.maximum(n, max_exact).astype(np.float32) / np.float32(max_exact)
    large = max_exact + (np.log(ratio) / np.float32(math.log(MAX_DISTANCE / max_exact))
                         * np.float32(NUM_BUCKETS - max_exact)).astype(np.int32)
    large = np.minimum(large, NUM_BUCKETS - 1)
    bucket = np.where(n < max_exact, n, large)
    band = (dist >= 0) & (dist < WINDOW)
    return np.where(band, bucket, -1).astype(np.int32)


def _swa_bias_kernel(rb_ref, bucket_ref, o_ref):
    first = pl.program_id(0) == 0
    bucket = bucket_ref[...]
    col = lax.broadcasted_iota(I32, bucket.shape, 1)
    hidden = jnp.logical_and(first, col < WINDOW)
    for h in range(o_ref.shape[1]):
        acc = jnp.full(bucket.shape, NEG_INF, F32)
        for b in range(NUM_BUCKETS):
            acc = jnp.where(bucket == b, rb_ref[b, h] * LOG2E, acc)
        o_ref[0, h] = jnp.where(hidden, NEG_INF, acc)


def _swa_bias(rel_bias):
    nh = rel_bias.shape[1]
    bucket = jnp.asarray(_bucket_map())
    return pl.pallas_call(
        _swa_bias_kernel,
        grid=(2,),
        in_specs=[pl.BlockSpec(memory_space=pltpu.SMEM),
                  pl.BlockSpec((WINDOW, 2 * WINDOW), lambda v: (0, 0))],
        out_specs=pl.BlockSpec((1, nh, WINDOW, 2 * WINDOW), lambda v: (v, 0, 0, 0)),
        out_shape=jax.ShapeDtypeStruct((2, nh, WINDOW, 2 * WINDOW), F32),
        compiler_params=_params(("arbitrary",)),
        name="swa_bias",
    )(rel_bias, bucket)


def _swa_kernel(sink_ref, q_ref, kvc_ref, kvp_ref, bias_ref, o_ref, *, n_heads):
    group = n_heads // KV_A
    kv = jnp.concatenate([kvp_ref[0], kvc_ref[0]], axis=0)
    lane = lax.broadcasted_iota(I32, (2 * WINDOW, LANES), 1)
    low = lane < HEAD_DIM

    def halves(pair):
        zero = jnp.zeros_like(pair)
        return [jnp.where(low, pair, zero), jnp.where(low, zero, pair)]

    kz = halves(kv[:, 0:LANES])
    vz = halves(kv[:, LANES:2 * LANES])

    n_pairs = n_heads // KV_A
    for p0 in range(0, n_pairs, SWA_PAIRS):
        pairs = range(p0, min(p0 + SWA_PAIRS, n_pairs))
        heads = [(p, g) for p in pairs for g in range(KV_A)]
        logits, e, den, o = {}, {}, {}, {}
        for p, g in heads:
            qp = q_ref[0, :, p * LANES:(p + 1) * LANES]
            s = lax.dot_general(qp, kz[g], (((1,), (1,)), ((), ())), preferred_element_type=F32)
            logits[p, g] = s + bias_ref[0, g * group + p]
        for p, g in heads:
            sink = sink_ref[g * group + p]
            m = jnp.maximum(jnp.max(logits[p, g], axis=-1, keepdims=True), sink)
            e[p, g] = jnp.exp2(logits[p, g] - m)
            den[p, g] = jnp.sum(e[p, g], axis=-1, keepdims=True) + jnp.exp2(sink - m)
        for p, g in heads:
            o[p, g] = jnp.dot(e[p, g].astype(BF16), vz[g], preferred_element_type=F32)
        for p in pairs:
            acc = o[p, 0] * (1.0 / den[p, 0])
            for g in range(1, KV_A):
                acc = acc + o[p, g] * (1.0 / den[p, g])
            o_ref[0, :, p * LANES:(p + 1) * LANES] = acc.astype(BF16)


def _swa(qkv, sinks, bias, d_a, d_b):
    bsz, s, _ = qkv.shape
    n_heads = d_a // HEAD_DIM
    q_blk = 3 * d_b // d_a
    kv_blk = (3 * d_b + d_a) // (2 * LANES)
    return pl.pallas_call(
        functools.partial(_swa_kernel, n_heads=n_heads),
        grid=(bsz, s // WINDOW),
        in_specs=[pl.BlockSpec(memory_space=pltpu.SMEM),
                  pl.BlockSpec((1, WINDOW, d_a), lambda b, i: (b, i, q_blk)),
                  pl.BlockSpec((1, WINDOW, 2 * LANES), lambda b, i: (b, i, kv_blk)),
                  pl.BlockSpec((1, WINDOW, 2 * LANES),
                               lambda b, i: (b, jnp.maximum(i - 1, 0), kv_blk)),
                  pl.BlockSpec((1, n_heads, WINDOW, 2 * WINDOW),
                               lambda b, i: (jnp.minimum(i, 1), 0, 0, 0))],
        out_specs=pl.BlockSpec((1, WINDOW, d_a), lambda b, i: (b, i, 0)),
        out_shape=jax.ShapeDtypeStruct((bsz, s, d_a), BF16),
        compiler_params=_params(("arbitrary", "arbitrary")),
        name="swa",
    )(sinks, qkv, qkv, qkv, bias)


def _suffix_matrix():
    j = np.arange(Q_BLOCK)[:, None]
    s = np.arange(Q_BLOCK)[None, :]
    return np.concatenate([(j > s), np.ones((Q_BLOCK, Q_BLOCK), bool)], axis=1).astype(np.float32)


def _sb_kernel(q_ref, k_ref, v_ref, lt_ref, o_ref, acc_ref, carry_ref, *, group):
    i = pl.program_id(2)
    lane = lax.broadcasted_iota(I32, (Q_BLOCK, LANES), 1)
    low = lane < HEAD_DIM
    row = lax.broadcasted_iota(I32, (2 * Q_BLOCK, Q_BLOCK), 0)
    col = lax.broadcasted_iota(I32, (2 * Q_BLOCK, Q_BLOCK), 1)
    strict = col < jnp.where(row >= Q_BLOCK, row - Q_BLOCK, row)
    sign = jnp.uint32(0x80000000)

    qh = []
    for g in range(group):
        q = q_ref[0, :, g * LANES:(g + 1) * LANES]
        zero = jnp.zeros_like(q)
        qh.append([jnp.where(low, q, zero), jnp.where(low, zero, q)])

    gs = range(group)

    def scores(j, g, lo, hi):
        return lax.dot_general(jnp.concatenate([qh[g][0][lo:hi], qh[g][1][lo:hi]], axis=0),
                               k_ref[0, pl.ds(pl.multiple_of(j * Q_BLOCK, Q_BLOCK), Q_BLOCK),
                                     g * LANES:(g + 1) * LANES],
                               (((1,), (1,)), ((), ())), preferred_element_type=F32)

    def softplus2(z):
        neg_abs = pltpu.bitcast(pltpu.bitcast(z, jnp.uint32) | sign, F32)
        return jnp.maximum(z, 0.0) + jnp.log2(1.0 + jnp.exp2(neg_abs))

    def suffix(sp):
        return jnp.dot(sp.astype(BF16), lt_ref[...], preferred_element_type=F32)

    def weighted_values(a, j, g, rows):
        a = a.astype(BF16)
        a2 = jnp.concatenate([a[:rows], a[rows:]], axis=1)
        vj = v_ref[0, pl.ds(pl.multiple_of(j * Q_BLOCK, Q_BLOCK), Q_BLOCK), g * LANES:(g + 1) * LANES]
        vzero = jnp.zeros_like(vj)
        vz = jnp.concatenate([jnp.where(low, vj, vzero), jnp.where(low, vzero, vj)], axis=0)
        return jnp.dot(a2, vz, preferred_element_type=F32)

    t = SB_TAIL_ROWS

    def carry_mins(carry_min, lo, hi):
        n = hi - lo
        n_top = max(min(hi, t) - lo, 0)
        top = jnp.min(jnp.minimum(carry_min[:n_top], carry_min[n:n + n_top])) if n_top else None
        rest = jnp.min(jnp.minimum(carry_min[n_top:n], carry_min[n + n_top:])) if n_top < n else None
        return top, rest

    def first_blocks(n_before):
        pieces = [(i, 0, Q_BLOCK)]
        if n_before >= 1:
            pieces.append((i - 1, 0, Q_BLOCK))
        if n_before >= 2:
            pieces.append((i - 2, 0, t))
        z = {(b, g): scores(j, g, lo, hi) for b, (j, lo, hi) in enumerate(pieces) for g in gs}
        for g in gs:
            z[0, g] = jnp.where(strict, z[0, g], NEG_INF)
        sp = {bg: softplus2(zz) for bg, zz in z.items()}
        cs = {bg: suffix(s) for bg, s in sp.items()}
        a, carry_min, head_min = {}, None, None
        for g in gs:
            a[0, g] = jnp.exp2(z[0, g] - sp[0, g] - cs[0, g][:, :Q_BLOCK])
            carry = cs[0, g][:, Q_BLOCK:]
            if n_before >= 1:
                a[1, g] = jnp.exp2(z[1, g] - sp[1, g] - cs[1, g][:, :Q_BLOCK] - carry)
                carry = carry + cs[1, g][:, Q_BLOCK:]
            carry_ref[g, 0] = carry[:Q_BLOCK]
            carry_ref[g, 1] = carry[Q_BLOCK:]
            carry_min = carry if carry_min is None else jnp.minimum(carry_min, carry)
            if n_before >= 2:
                head = jnp.concatenate([carry[:t], carry[Q_BLOCK:Q_BLOCK + t]], axis=0)
                a[2, g] = jnp.exp2(z[2, g] - sp[2, g] - cs[2, g][:, :Q_BLOCK] - head)
                head = head + cs[2, g][:, Q_BLOCK:]
                carry_ref[g, 0, :t] = head[:t]
                carry_ref[g, 1, :t] = head[t:]
                head_min = head if head_min is None else jnp.minimum(head_min, head)
        for g in gs:
            acc = weighted_values(a[0, g], i, g, Q_BLOCK)
            if n_before >= 1:
                acc = acc + weighted_values(a[1, g], i - 1, g, Q_BLOCK)
            acc_ref[g] = acc
            if n_before >= 2:
                acc_ref[g, :t] += weighted_values(a[2, g], i - 2, g, t)
        top, rest = carry_mins(carry_min, 0, Q_BLOCK)
        if n_before >= 2:
            top = carry_mins(head_min, 0, t)[0]
        return top, rest

    def block(j, lo, hi):
        n = hi - lo
        z = [scores(j, g, lo, hi) for g in gs]
        sp = [softplus2(zz) for zz in z]
        cs = [suffix(s) for s in sp]
        a, carry_min = [], None
        for g in gs:
            carry = jnp.concatenate([carry_ref[g, 0, lo:hi], carry_ref[g, 1, lo:hi]], axis=0)
            a.append(jnp.exp2(z[g] - sp[g] - cs[g][:, :Q_BLOCK] - carry))
            carry = carry + cs[g][:, Q_BLOCK:]
            carry_ref[g, 0, lo:hi] = carry[:n]
            carry_ref[g, 1, lo:hi] = carry[n:]
            carry_min = carry if carry_min is None else jnp.minimum(carry_min, carry)
        for g in gs:
            acc_ref[g, lo:hi] += weighted_values(a[g], j, g, n)
        return carry_mins(carry_min, lo, hi)

    def two_before():
        top, rest = first_blocks(2)
        rest = lax.cond(rest < SB_SKIP_BITS, lambda: block(i - 2, t, Q_BLOCK)[1], lambda: rest)
        return top, rest

    top0, rest0 = lax.cond(
        i >= 2, two_before,
        lambda: lax.cond(i == 1, lambda: first_blocks(1), lambda: first_blocks(0)))

    def more(state):
        jj, top, rest = state
        return jnp.logical_and(jj < i - 2, jnp.minimum(top, rest) < SB_SKIP_BITS)

    def body(state):
        jj, _, rest = state
        j = i - 3 - jj

        def tail_rows():
            return block(j, 0, t)[0], rest

        def all_rows():
            return block(j, 0, Q_BLOCK)

        top, rest = lax.cond(rest >= SB_SKIP_BITS, tail_rows, all_rows)
        return jj + 1, top, rest

    lax.while_loop(more, body, (jnp.int32(0), top0, rest0))
    for g in range(group):
        o_ref[0, :, g * LANES:(g + 1) * LANES] = acc_ref[g].astype(BF16)


def _sb(qkv, d_b, group=SB_GROUP):
    bsz, s, _ = qkv.shape
    pairs = d_b // LANES
    ng = pairs // group
    w = group * LANES
    lt = jnp.asarray(_suffix_matrix(), BF16)
    return pl.pallas_call(
        functools.partial(_sb_kernel, group=group),
        grid=(bsz, ng, s // Q_BLOCK),
        in_specs=[pl.BlockSpec((1, Q_BLOCK, w), lambda b, p, i: (b, i, p)),
                  pl.BlockSpec((1, s, w), lambda b, p, i: (b, 0, ng + p)),
                  pl.BlockSpec((1, s, w), lambda b, p, i: (b, 0, 2 * ng + p)),
                  pl.BlockSpec((Q_BLOCK, 2 * Q_BLOCK), lambda b, p, i: (0, 0))],
        out_specs=pl.BlockSpec((1, Q_BLOCK, w), lambda b, p, i: (b, i, p)),
        out_shape=jax.ShapeDtypeStruct((bsz, s, d_b), BF16),
        scratch_shapes=[pltpu.VMEM((group, Q_BLOCK, LANES), F32),
                        pltpu.VMEM((group, 2, Q_BLOCK, LANES), F32)],
        compiler_params=_params(("arbitrary", "arbitrary", "arbitrary")),
        name="sb",
    )(qkv, qkv, qkv, lt)


def _layer_norm(y, g, b):
    mu = jnp.mean(y, axis=-1, keepdims=True)
    yc = y - mu
    var = jnp.mean(yc * yc, axis=-1, keepdims=True)
    return yc * lax.rsqrt(var + EPS) * g + b


def _rms(o, g):
    return o * lax.rsqrt(jnp.mean(o * o, axis=-1, keepdims=True) + EPS) * g


def _mix_ln1_kernel(oa_ref, ob_ref, x_ref, mod_ref, na_ref, nb_ref, wo_ref, g_ref, b_ref,
                    wrc_ref, br_ref, x1_ref, h2_ref, lg_ref, *, d_a, parts):
    hm = x_ref.shape[1] // parts
    rows = [pl.ds(p * hm, hm) for p in range(parts)]
    gate1 = mod_ref[0, 2:3, :]
    shift2 = mod_ref[0, 3:4, :]
    scale2 = mod_ref[0, 4:5, :]
    ra = [_rms(oa_ref[0, r, :].astype(F32), na_ref[...]).astype(BF16) for r in rows]
    rb = [_rms(ob_ref[0, r, :].astype(F32), nb_ref[...]).astype(BF16) for r in rows]
    mix = [jnp.dot(ra[p], wo_ref[:d_a, :], preferred_element_type=F32)
           + jnp.dot(rb[p], wo_ref[d_a:, :], preferred_element_type=F32) for p in range(parts)]
    hi, lo = [], []
    for p, r in enumerate(rows):
        x1 = _layer_norm(ALPHA * x_ref[0, r, :] + (1.0 + gate1) * mix[p], g_ref[...], b_ref[...])
        x1_ref[0, r, :] = x1
        h2 = x1 * (1.0 + scale2) + shift2
        _store_row_tiles(h2_ref.at[0, pl.ds(p * hm * ROW_SUB, hm * ROW_SUB)], h2)
        hi.append(h2.astype(BF16))
        lo.append((h2 - hi[p].astype(F32)).astype(BF16))
    for p, r in enumerate(rows):
        both = jnp.dot(hi[p], wrc_ref[...], preferred_element_type=F32)
        lg_ref[0, r, :] = (both[:, :LANES] + both[:, LANES:]
                           + jnp.dot(lo[p], wrc_ref[:, :LANES], preferred_element_type=F32)
                           + br_ref[...])


def _mix_ln1(o_a, o_b, x, mod3, norm_a, norm_b, w_out_bf, ln_g, ln_b, wr_cat, b_r, tm=512, parts=2):
    bsz, s, d = x.shape
    d_a = o_a.shape[-1]
    d_b = o_b.shape[-1]
    tm = min(tm, s)
    row = lambda b, i: (b, i, 0)
    const2 = lambda b, i: (0, 0)
    once = pl.Buffered(1)
    return pl.pallas_call(
        functools.partial(_mix_ln1_kernel, d_a=d_a, parts=parts),
        grid=(bsz, s // tm),
        in_specs=[pl.BlockSpec((1, tm, d_a), row),
                  pl.BlockSpec((1, tm, d_b), row),
                  pl.BlockSpec((1, tm, d), row),
                  pl.BlockSpec((1, 6, d), lambda b, i: (b, 0, 0)),
                  pl.BlockSpec((1, d_a), const2),
                  pl.BlockSpec((1, d_b), const2),
                  pl.BlockSpec((d_a + d_b, d), const2, pipeline_mode=once),
                  pl.BlockSpec((1, d), const2),
                  pl.BlockSpec((1, d), const2),
                  pl.BlockSpec((d, 2 * LANES), const2, pipeline_mode=once),
                  pl.BlockSpec((1, LANES), const2)],
        out_specs=[pl.BlockSpec((1, tm, d), row),
                   pl.BlockSpec((1, tm * ROW_SUB, LANES), row),
                   pl.BlockSpec((1, tm, LANES), row)],
        out_shape=[jax.ShapeDtypeStruct((bsz, s, d), F32),
                   jax.ShapeDtypeStruct((bsz, s * ROW_SUB, LANES), jnp.uint32),
                   jax.ShapeDtypeStruct((bsz, s, LANES), F32)],
        compiler_params=_params(("arbitrary", "arbitrary")),
        name="mix_ln1",
    )(o_a, o_b, x, mod3, norm_a, norm_b, w_out_bf, ln_g, ln_b, wr_cat, b_r)


def _route_kernel(lg_ref, tri_ref, sel_ref, gate_ref, cnt_ref, base_ref):
    step = pl.program_id(0)

    @pl.when(step == 0)
    def _():
        base_ref[...] = jnp.zeros_like(base_ref)

    lg = lg_ref[...]
    tm = lg.shape[0]
    lane = lax.broadcasted_iota(I32, (tm, LANES), 1)
    big = jnp.int32(2 * LANES)
    glog = jnp.where(lane < N_GROUPS, lg, -jnp.inf)
    gmax = jnp.max(glog, axis=-1, keepdims=True)
    g_sel = jnp.min(jnp.where(glog == gmax, lane, big), axis=-1, keepdims=True)
    p_g = 1.0 / jnp.sum(jnp.exp(glog - gmax), axis=-1, keepdims=True)
    lo = N_GROUPS + g_sel * EXPERTS_PER_GROUP
    in_grp = jnp.logical_and(lane >= lo, lane < lo + EXPERTS_PER_GROUP)
    el = jnp.where(in_grp, lg, -jnp.inf)
    v1 = jnp.max(el, axis=-1, keepdims=True)
    i1 = jnp.min(jnp.where(el == v1, lane, big), axis=-1, keepdims=True)
    el2 = jnp.where(lane == i1, -jnp.inf, el)
    v2 = jnp.max(el2, axis=-1, keepdims=True)
    i2 = jnp.min(jnp.where(el2 == v2, lane, big), axis=-1, keepdims=True)
    r = jnp.exp(v2 - v1)
    w1 = 1.0 / (1.0 + r)
    g1 = p_g * w1
    g2 = p_g * (r * w1)
    e1 = i1 - N_GROUPS
    e2 = i2 - N_GROUPS
    oh1 = (lane == e1)
    oh2 = (lane == e2)
    occ = oh1.astype(F32) + oh2.astype(F32)
    before = jnp.dot(tri_ref[...], occ.astype(BF16), preferred_element_type=F32) + base_ref[...]
    r1 = jnp.sum(jnp.where(oh1, before, 0.0), axis=-1, keepdims=True)
    r2 = jnp.sum(jnp.where(oh2, before, 0.0), axis=-1, keepdims=True)
    base_ref[...] += jnp.sum(occ, axis=0, keepdims=True)
    cnt_ref[...] = base_ref[...]
    sel = jnp.where(lane == 0, e1, jnp.where(lane == 1, e2, 0))
    sel = jnp.where(lane == 2, r1.astype(I32), jnp.where(lane == 3, r2.astype(I32), sel))
    sel_ref[...] = sel
    gate_ref[...] = jnp.where(lane == 0, g1, jnp.where(lane == 1, g2, 0.0))


def _route(logits, tm=1024):
    t = logits.shape[0]
    tm = min(tm, t)
    tri = jnp.asarray(np.tril(np.ones((tm, tm), np.float32), -1), BF16)
    return pl.pallas_call(
        _route_kernel,
        grid=(t // tm,),
        in_specs=[pl.BlockSpec((tm, LANES), lambda i: (i, 0)),
                  pl.BlockSpec((tm, tm), lambda i: (0, 0))],
        out_specs=[pl.BlockSpec((tm, LANES), lambda i: (i, 0)),
                   pl.BlockSpec((tm, LANES), lambda i: (i, 0)),
                   pl.BlockSpec((1, LANES), lambda i: (0, 0))],
        out_shape=[jax.ShapeDtypeStruct((t, LANES), I32),
                   jax.ShapeDtypeStruct((t, LANES), F32),
                   jax.ShapeDtypeStruct((1, LANES), F32)],
        scratch_shapes=[pltpu.VMEM((1, LANES), F32)],
        compiler_params=_params(("arbitrary",)),
        name="route",
    )(logits, tri)


def _lane_prefix(x, lane):
    shift = 1
    while shift < LANES:
        x = x + jnp.where(lane >= shift, pltpu.roll(x, shift, axis=1), 0)
        shift *= 2
    return x


def _dest_kernel(sel_ref, cnt_ref, dest_ref, blk_ref, plan_ref, *, n_blk_pad):
    tm = sel_ref.shape[0]
    lane1 = lax.broadcasted_iota(I32, (8, LANES), 1)
    cnt = jnp.broadcast_to(cnt_ref[...].astype(I32), (8, LANES))
    cnt = jnp.where(lane1 < N_EXPERTS, cnt, 0)
    padded = jnp.bitwise_and(cnt + (MOE_TM - 1), -MOE_TM)
    pend = _lane_prefix(padded, lane1)
    pstart = (pend - padded)[0:1, :]
    sel = sel_ref[...]
    lane = lax.broadcasted_iota(I32, (tm, LANES), 1)
    e1 = sel[:, 0:1]
    e2 = sel[:, 1:2]
    d1 = jnp.sum(jnp.where(lane == e1, pstart, 0), axis=-1, keepdims=True) + sel[:, 2:3]
    d2 = jnp.sum(jnp.where(lane == e2, pstart, 0), axis=-1, keepdims=True) + sel[:, 3:4]
    dest_ref[...] = jnp.where(lane == 0, d1, jnp.where(lane == 1, d2, 0)) * ROW_SUB

    @pl.when(pl.program_id(0) == 0)
    def _():
        brow = lax.broadcasted_iota(I32, (n_blk_pad, LANES), 0) * MOE_TM
        blane = lax.broadcasted_iota(I32, (n_blk_pad, LANES), 1)
        ended = jnp.logical_and(blane < N_EXPERTS, pend[0:1, :] <= brow)
        be = jnp.minimum(jnp.sum(ended.astype(I32), axis=-1, keepdims=True), N_EXPERTS - 1)
        blk_ref[...] = jnp.broadcast_to(be, (n_blk_pad, LANES))
        sub = lax.broadcasted_iota(I32, (8, LANES), 0)
        used = jnp.max(pend, axis=-1, keepdims=True) >> (MOE_TM.bit_length() - 1)
        plan_ref[...] = jnp.where(sub == 0, (pend - padded + cnt) * ROW_SUB,
                                  jnp.where(sub == 1, padded - cnt, used))


def _dest(sel, counts, n_blk, tm=1024):
    t = sel.shape[0]
    tm = min(tm, t)
    n_blk_pad = -(-n_blk // 8) * 8
    return pl.pallas_call(
        functools.partial(_dest_kernel, n_blk_pad=n_blk_pad),
        grid=(t // tm,),
        in_specs=[pl.BlockSpec((tm, LANES), lambda i: (i, 0)),
                  pl.BlockSpec((1, LANES), lambda i: (0, 0))],
        out_specs=[pl.BlockSpec((tm, LANES), lambda i: (i, 0)),
                   pl.BlockSpec((n_blk_pad, LANES), lambda i: (0, 0)),
                   pl.BlockSpec((8, LANES), lambda i: (0, 0))],
        out_shape=[jax.ShapeDtypeStruct((t, LANES), I32),
                   jax.ShapeDtypeStruct((n_blk_pad, LANES), I32),
                   jax.ShapeDtypeStruct((8, LANES), I32)],
        compiler_params=_params(("arbitrary",)),
        name="dest",
    )(sel, counts)


def _dispatch_kernel(dest_ref, plan_ref, h_ref, xs_ref, stage, zeros, sems, zsem, *, n_blk):
    tm = h_ref.shape[0] // ROW_SUB
    step = pl.program_id(0)
    last = pl.num_programs(0) - 1
    slot = lax.rem(step, 2)
    base = step * (2 * tm)
    blk_sub = MOE_TM * ROW_SUB

    def fill(wait):
        def run(copy):
            copy.wait() if wait else copy.start()

        def pads(e, carry):
            first = plan_ref[e]
            n = plan_ref[N_EXPERTS + e]
            bit = MOE_TM // 2
            while bit >= 1:
                @pl.when(jnp.bitwise_and(n, bit) != 0)
                def _(bit=bit):
                    done = jnp.bitwise_and(n, -2 * bit)
                    dst = pl.multiple_of(first + done * ROW_SUB, ROW_SUB)
                    run(pltpu.make_async_copy(zeros.at[pl.ds(0, bit * ROW_SUB)],
                                              xs_ref.at[pl.ds(dst, bit * ROW_SUB)], zsem))
                bit //= 2
            return carry

        lax.fori_loop(0, N_EXPERTS, pads, 0)

        def unused(b, carry):
            dst = pl.multiple_of(b * blk_sub, blk_sub)
            run(pltpu.make_async_copy(zeros, xs_ref.at[pl.ds(dst, blk_sub)], zsem))
            return carry

        lax.fori_loop(plan_ref[2 * N_EXPERTS], n_blk, unused, 0)

    @pl.when(step == 0)
    def _():
        zeros[...] = jnp.zeros_like(zeros)
        fill(wait=False)
        fill(wait=True)

    def wait_tile(sl):
        for _ in range(2):
            pltpu.make_async_copy(stage.at[sl], xs_ref.at[pl.ds(0, tm * ROW_SUB)], sems.at[sl]).wait()

    @pl.when(step >= 2)
    def _():
        wait_tile(slot)

    stage[slot] = h_ref[...]

    def start(c, carry):
        for u in range(ROW_UNROLL):
            r = c * ROW_UNROLL + u
            src = stage.at[slot, pl.ds(pl.multiple_of(r * ROW_SUB, ROW_SUB), ROW_SUB)]
            for k in range(2):
                dst = pl.multiple_of(dest_ref[base + 2 * r + k], ROW_SUB)
                pltpu.make_async_copy(src, xs_ref.at[pl.ds(dst, ROW_SUB)], sems.at[slot]).start()
        return carry

    lax.fori_loop(0, tm // ROW_UNROLL, start, 0)

    @pl.when(step == last)
    def _():
        wait_tile(slot)

        @pl.when(step >= 1)
        def _():
            wait_tile(1 - slot)


def _dispatch(dest_flat, plan_flat, h2p, m_pad, tm=256):
    t = h2p.shape[0] // ROW_SUB
    tm = min(tm, t)
    return pl.pallas_call(
        functools.partial(_dispatch_kernel, n_blk=m_pad // MOE_TM),
        grid_spec=pltpu.PrefetchScalarGridSpec(
            num_scalar_prefetch=2,
            grid=(t // tm,),
            in_specs=[pl.BlockSpec((tm * ROW_SUB, LANES), lambda i, dest, plan: (i, 0))],
            out_specs=pl.BlockSpec(memory_space=pl.ANY),
            scratch_shapes=[pltpu.VMEM((2, tm * ROW_SUB, LANES), h2p.dtype),
                            pltpu.VMEM((MOE_TM * ROW_SUB, LANES), h2p.dtype),
                            pltpu.SemaphoreType.DMA((2,)),
                            pltpu.SemaphoreType.DMA(())]),
        out_shape=jax.ShapeDtypeStruct((m_pad * ROW_SUB, LANES), h2p.dtype),
        compiler_params=_params(("arbitrary",)),
        name="dispatch",
    )(dest_flat, plan_flat, h2p)


def _experts_kernel(blk_ref, xs_ref, wg_hbm, wu_hbm, wd_hbm, ys_ref,
                    wg_f32, wu_f32, wd_f32, wg_bf, wu_bf, wd_bf, slot_ref, sems):
    i = pl.program_id(0)
    n_blk = pl.num_programs(0)
    n_used = blk_ref[n_blk]
    e = blk_ref[i]
    in_use = i < n_used
    first_of_run = jnp.logical_and(
        in_use, jnp.logical_or(i == 0, blk_ref[jnp.maximum(i - 1, 0)] != e))

    def weight_copies(expert, sl):
        return [pltpu.make_async_copy(wg_hbm.at[0, expert], wg_f32.at[sl], sems.at[sl]),
                pltpu.make_async_copy(wu_hbm.at[0, expert], wu_f32.at[sl], sems.at[sl]),
                pltpu.make_async_copy(wd_hbm.at[0, expert], wd_f32.at[sl], sems.at[sl])]

    @pl.when(i == 0)
    def _():
        slot_ref[0] = 0
        for c in weight_copies(e, 0):
            c.start()

    @pl.when(first_of_run)
    def _():
        sl = slot_ref[0]
        for c in weight_copies(e, sl):
            c.wait()
        wg_bf[...] = wg_f32[sl].astype(BF16)
        wu_bf[...] = wu_f32[sl].astype(BF16)
        wd_bf[...] = wd_f32[sl].astype(BF16)
        nxt = lax.while_loop(
            lambda j: jnp.logical_and(j < n_used, blk_ref[jnp.minimum(j, n_blk - 1)] == e),
            lambda j: j + 1, i + 1)

        @pl.when(nxt < n_used)
        def _():
            for c in weight_copies(blk_ref[nxt], 1 - sl):
                c.start(priority=1)

        slot_ref[0] = 1 - sl

    @pl.when(in_use)
    def _():
        xb = jnp.concatenate([c.astype(BF16) for c in _load_row_tiles(xs_ref, MOE_TM)], axis=1)
        g = jnp.dot(xb, wg_bf[...], preferred_element_type=F32)
        u = jnp.dot(xb, wu_bf[...], preferred_element_type=F32)
        hmid = (g * jax.nn.sigmoid(g) * u).astype(BF16)
        _store_row_tiles(ys_ref, jnp.dot(hmid, wd_bf[...], preferred_element_type=F32))

    @pl.when(jnp.logical_not(in_use))
    def _():
        ys_ref[...] = jnp.zeros_like(ys_ref)


def _experts(blk_expert, xs, w_gate, w_up, w_down):
    d, de = w_gate.shape[-2:]
    n_blk = xs.shape[0] // (MOE_TM * ROW_SUB)
    return pl.pallas_call(
        _experts_kernel,
        grid_spec=pltpu.PrefetchScalarGridSpec(
            num_scalar_prefetch=1,
            grid=(n_blk,),
            in_specs=[pl.BlockSpec((MOE_TM * ROW_SUB, LANES), lambda i, blk: (i, 0)),
                      pl.BlockSpec(memory_space=pl.ANY),
                      pl.BlockSpec(memory_space=pl.ANY),
                      pl.BlockSpec(memory_space=pl.ANY)],
            out_specs=pl.BlockSpec((MOE_TM * ROW_SUB, LANES), lambda i, blk: (i, 0)),
            scratch_shapes=[pltpu.VMEM((2, d, de), F32),
                            pltpu.VMEM((2, d, de), F32),
                            pltpu.VMEM((2, de, d), F32),
                            pltpu.VMEM((d, de), BF16),
                            pltpu.VMEM((d, de), BF16),
                            pltpu.VMEM((de, d), BF16),
                            pltpu.SMEM((1,), I32),
                            pltpu.SemaphoreType.DMA((2,))]),
        out_shape=jax.ShapeDtypeStruct(xs.shape, jnp.uint32),
        compiler_params=_params(("arbitrary",)),
        name="experts",
    )(blk_expert, xs, w_gate, w_up, w_down)


def _combine_kernel(dest_ref, ys_ref, gate_ref, x1_ref, mod_ref, g_ref, b_ref, o_ref, *scratch):
    ybufs, sems = scratch[:COMBINE_SLOTS], scratch[COMBINE_SLOTS]
    tm = x1_ref.shape[1]
    n_steps = pl.num_programs(0) * pl.num_programs(1)
    step = pl.program_id(0) * pl.num_programs(1) + pl.program_id(1)

    def row_copy(base, r, k, sl):
        src = pl.multiple_of(dest_ref[base + 2 * r + k], ROW_SUB)
        return pltpu.make_async_copy(ys_ref.at[pl.ds(src, ROW_SUB)],
                                     ybufs[sl].at[k, pl.ds(pl.multiple_of(r * ROW_SUB, ROW_SUB), ROW_SUB)],
                                     sems.at[sl])

    def wait_slot(sl):
        for k in range(2):
            pltpu.make_async_copy(ys_ref.at[pl.ds(0, tm * ROW_SUB)], ybufs[sl].at[k], sems.at[sl]).wait()

    @pl.when(step == 0)
    def _():
        for ahead in range(COMBINE_SLOTS - 1):
            base = jnp.minimum(ahead, n_steps - 1) * (2 * tm)

            def start(c, carry, base=base, ahead=ahead):
                for u in range(ROW_UNROLL):
                    for k in range(2):
                        row_copy(base, c * ROW_UNROLL + u, k, ahead).start()
                return carry

            lax.fori_loop(0, tm // ROW_UNROLL, start, 0)

    def run(sl):
        wait_slot(sl)
        ahead_base = jnp.minimum(step + COMBINE_SLOTS - 1, n_steps - 1) * (2 * tm)
        ahead_slot = (sl + COMBINE_SLOTS - 1) % COMBINE_SLOTS
        for r in range(tm):
            for k in range(2):
                row_copy(ahead_base, r, k, ahead_slot).start()

        gates = gate_ref[0]
        g0 = gates[:, 0:1]
        g1 = gates[:, 1:2]
        y0 = _load_row_tiles(ybufs[sl].at[0], tm)
        y1 = _load_row_tiles(ybufs[sl].at[1], tm)
        ffn = jnp.concatenate([g0 * a + g1 * b for a, b in zip(y0, y1)], axis=1)
        gate2 = mod_ref[0, 5:6, :]
        o_ref[0] = _layer_norm(ALPHA * x1_ref[0] + (1.0 + gate2) * ffn, g_ref[...], b_ref[...])

        @pl.when(step == n_steps - 1)
        def _():
            for ahead in range(1, COMBINE_SLOTS):
                wait_slot((sl + ahead) % COMBINE_SLOTS)

    for sl in range(COMBINE_SLOTS):
        pl.when(lax.rem(step, COMBINE_SLOTS) == sl)(functools.partial(run, sl))


def _combine(dest_flat, ys, gates3, x1, mod3, ln_g, ln_b, tm=256):
    bsz, s, d = x1.shape
    tm = min(tm, s)
    return pl.pallas_call(
        _combine_kernel,
        grid_spec=pltpu.PrefetchScalarGridSpec(
            num_scalar_prefetch=1,
            grid=(bsz, s // tm),
            in_specs=[pl.BlockSpec(memory_space=pl.ANY),
                      pl.BlockSpec((1, tm, LANES), lambda b, i, dest: (b, i, 0)),
                      pl.BlockSpec((1, tm, d), lambda b, i, dest: (b, i, 0)),
                      pl.BlockSpec((1, 6, d), lambda b, i, dest: (b, 0, 0)),
                      pl.BlockSpec((1, d), lambda b, i, dest: (0, 0)),
                      pl.BlockSpec((1, d), lambda b, i, dest: (0, 0))],
            out_specs=pl.BlockSpec((1, tm, d), lambda b, i, dest: (b, i, 0)),
            scratch_shapes=[pltpu.VMEM((2, tm * ROW_SUB, LANES), jnp.uint32)] * COMBINE_SLOTS
            + [pltpu.SemaphoreType.DMA((COMBINE_SLOTS,))]),
        out_shape=jax.ShapeDtypeStruct((bsz, s, d), F32),
        compiler_params=_params(("arbitrary", "arbitrary")),
        name="combine",
    )(dest_flat, ys, gates3, x1, mod3, ln_g, ln_b)


def kernel(x, c, w_in, w_out, sinks, rel_bias, norm_a, norm_b, w_ada, b_ada, ln1_g, ln1_b,
           ln2_g, ln2_b, w_grp, b_grp, w_rtr, b_rtr, w_gate, w_up, w_down):
    bsz, s, d = x.shape
    t = bsz * s
    d_a = norm_a.shape[-1]
    d_b = norm_b.shape[-1]

    mod3 = _adaln(c, w_ada, b_ada).reshape(bsz, 6, d)

    assert KV_A * HEAD_DIM == LANES
    kv_w = 2 * KV_A * HEAD_DIM
    group_a = d_a // HEAD_DIM // KV_A
    head_order = [g * group_a + p for p in range(group_a) for g in range(KV_A)]
    perm_a = np.concatenate([np.arange(HEAD_DIM) + HEAD_DIM * h for h in head_order])
    w0 = w_in[0]
    w_in_bf = jnp.concatenate(
        [w0[:, d_a + kv_w:d_a + kv_w + d_b] * (ATTN_SCALE * LOG2E),
         w0[:, d_a + kv_w + d_b:],
         w0[:, :d_a][:, perm_a] * (ATTN_SCALE * LOG2E),
         w0[:, d_a:d_a + kv_w]], axis=1).astype(BF16)
    qkv = _qkv(x, mod3, w_in_bf)
    norm_a = norm_a[:, perm_a]
    w_out_bf = jnp.concatenate([w_out[0][:d_a][perm_a], w_out[0][d_a:]], axis=0).astype(BF16)

    o_a = _swa(qkv, sinks[0] * LOG2E, _swa_bias(rel_bias), d_a, d_b)
    o_b = _sb(qkv, d_b)

    w_r = jnp.concatenate([w_grp[0], w_rtr[0]], axis=1)
    w_r = jnp.pad(w_r, ((0, 0), (0, LANES - w_r.shape[1])))
    b_r = jnp.pad(jnp.concatenate([b_grp[0], b_rtr[0]]), (0, LANES - N_GROUPS - N_EXPERTS))[None, :]
    wr_hi = w_r.astype(BF16)
    wr_lo = (w_r - wr_hi.astype(F32)).astype(BF16)
    assert d == 2 * LANES * ROW_SUB, "row tiles hold 256 * ROW_SUB features"
    x1, h2, logits = _mix_ln1(o_a, o_b, x, mod3, norm_a, norm_b, w_out_bf,
                              ln1_g, ln1_b, jnp.concatenate([wr_hi, wr_lo], axis=1), b_r)

    sel, gates, counts = _route(logits.reshape(t, LANES))
    m_pad = 2 * t + N_EXPERTS * MOE_TM
    n_blk = m_pad // MOE_TM
    dest, blk, plan = _dest(sel, counts, n_blk)
    dest_flat = dest[:, :2].reshape(2 * t)
    blk_expert = jnp.concatenate([blk[:n_blk, 0], plan[2, :1]])
    plan_flat = jnp.concatenate([plan[0, :N_EXPERTS], plan[1, :N_EXPERTS], plan[2, :1]])

    xs = _dispatch(dest_flat, plan_flat, h2.reshape(t * ROW_SUB, LANES), m_pad)
    ys = _experts(blk_expert, xs, w_gate, w_up, w_down)
    return _combine(dest_flat, ys, gates.reshape(bsz, s, LANES), x1, mod3, ln2_g, ln2_b)
```

```python
import functools
import math

import jax
import jax.numpy as jnp
import numpy as np
from jax import lax
from jax.experimental import pallas as pl
from jax.experimental.pallas import tpu as pltpu

F32 = jnp.float32
BF16 = jnp.bfloat16
I32 = jnp.int32

HEAD_DIM = 64
KV_A = 2
NUM_BUCKETS = 32
MAX_DISTANCE = 128
WINDOW = 128
Q_BLOCK = 128
N_GROUPS = 4
EXPERTS_PER_GROUP = 8
N_EXPERTS = N_GROUPS * EXPERTS_PER_GROUP
DEPTH = 1
ALPHA = (2.0 * DEPTH) ** 0.25
ATTN_SCALE = 1.0 / math.sqrt(HEAD_DIM)
EPS = 1e-5
NEG_INF = -1e30
LOG2E = math.log2(math.e)

LANES = 128
ROW_SUB = 8
MOE_TM = 256
COMBINE_SLOTS = 3
ROW_UNROLL = 8
SWA_PAIRS = 4
SB_GROUP = 8
SB_TAIL_ROWS = 48
SB_SKIP_BITS = 150.0
VMEM_LIMIT = 48 * 1024 * 1024


def _params(sem, vmem=VMEM_LIMIT):
    return pltpu.CompilerParams(dimension_semantics=sem, vmem_limit_bytes=vmem)


def _store_row_tiles(ref_2d, y):
    n = y.shape[0]
    for s in range(ROW_SUB):
        lo = pltpu.bitcast(y[:, 2 * s * LANES:(2 * s + 1) * LANES].astype(BF16).astype(F32), jnp.uint32)
        hi = pltpu.bitcast(y[:, (2 * s + 1) * LANES:(2 * s + 2) * LANES].astype(BF16).astype(F32), jnp.uint32)
        ref_2d[pl.ds(s, n, stride=ROW_SUB), :] = hi | (lo >> 16)


def _load_row_tiles(ref_2d, n):
    chunks = []
    for s in range(ROW_SUB):
        p = ref_2d[pl.ds(s, n, stride=ROW_SUB), :]
        chunks.append(pltpu.bitcast(p << 16, F32))
        chunks.append(pltpu.bitcast(p & jnp.uint32(0xFFFF0000), F32))
    return chunks


def _adaln_kernel(c_ref, w_ref, b_ref, o_ref):
    c = c_ref[...]
    ca = (c * jax.nn.sigmoid(c)).astype(BF16)
    o_ref[...] = jnp.dot(ca, w_ref[0].astype(BF16), preferred_element_type=F32) + b_ref[...]


def _adaln(c, w_ada, b_ada, tn=1024):
    bsz, d = c.shape
    n = w_ada.shape[-1]
    return pl.pallas_call(
        _adaln_kernel,
        grid=(n // tn,),
        in_specs=[pl.BlockSpec((bsz, d), lambda j: (0, 0)),
                  pl.BlockSpec((1, d, tn), lambda j: (0, 0, j)),
                  pl.BlockSpec((1, tn), lambda j: (0, j))],
        out_specs=pl.BlockSpec((bsz, tn), lambda j: (0, j)),
        out_shape=jax.ShapeDtypeStruct((bsz, n), F32),
        compiler_params=_params(("arbitrary",)),
        name="adaln",
    )(c, w_ada, b_ada)


def _qkv_kernel(x_ref, mod_ref, w_ref, o_ref):
    shift = mod_ref[0, 0:1, :]
    scale = mod_ref[0, 1:2, :]
    h = (x_ref[0] * (1.0 + scale) + shift).astype(BF16)
    o_ref[0] = jnp.dot(h, w_ref[...], preferred_element_type=F32).astype(BF16)


def _qkv(x, mod3, w_in_bf, tm=512, nj=2):
    bsz, s, d = x.shape
    n = w_in_bf.shape[1]
    tn = n // nj
    tm = min(tm, s)
    return pl.pallas_call(
        _qkv_kernel,
        grid=(nj, bsz, s // tm),
        in_specs=[pl.BlockSpec((1, tm, d), lambda j, b, i: (b, i, 0)),
                  pl.BlockSpec((1, 6, d), lambda j, b, i: (b, 0, 0)),
                  pl.BlockSpec((d, tn), lambda j, b, i: (0, j))],
        out_specs=pl.BlockSpec((1, tm, tn), lambda j, b, i: (b, i, j)),
        out_shape=jax.ShapeDtypeStruct((bsz, s, n), BF16),
        compiler_params=_params(("arbitrary", "arbitrary", "arbitrary")),
        name="qkv",
    )(x, mod3, w_in_bf)


def _bucket_map():
    qi = np.arange(WINDOW)[:, None]
    kj = np.arange(2 * WINDOW)[None, :]
    dist = qi + WINDOW - kj
    n = np.maximum(dist, 0)
    max_exact = NUM_BUCKETS // 2
    ratio = np.maximum(n, max_exact).astype(np.float32) / np.float32(max_exact)
    large = max_exact + (np.log(ratio) / np.float32(math.log(MAX_DISTANCE / max_exact))
                         * np.float32(NUM_BUCKETS - max_exact)).astype(np.int32)
    large = np.minimum(large, NUM_BUCKETS - 1)
    bucket = np.where(n < max_exact, n, large)
    band = (dist >= 0) & (dist < WINDOW)
    return np.where(band, bucket, -1).astype(np.int32)


def _swa_bias_kernel(rb_ref, bucket_ref, o_ref):
    first = pl.program_id(0) == 0
    bucket = bucket_ref[...]
    col = lax.broadcasted_iota(I32, bucket.shape, 1)
    hidden = jnp.logical_and(first, col < WINDOW)
    for h in range(o_ref.shape[1]):
        acc = jnp.full(bucket.shape, NEG_INF, F32)
        for b in range(NUM_BUCKETS):
            acc = jnp.where(bucket == b, rb_ref[b, h] * LOG2E, acc)
        o_ref[0, h] = jnp.where(hidden, NEG_INF, acc)


def _swa_bias(rel_bias):
    nh = rel_bias.shape[1]
    bucket = jnp.asarray(_bucket_map())
    return pl.pallas_call(
        _swa_bias_kernel,
        grid=(2,),
        in_specs=[pl.BlockSpec(memory_space=pltpu.SMEM),
                  pl.BlockSpec((WINDOW, 2 * WINDOW), lambda v: (0, 0))],
        out_specs=pl.BlockSpec((1, nh, WINDOW, 2 * WINDOW), lambda v: (v, 0, 0, 0)),
        out_shape=jax.ShapeDtypeStruct((2, nh, WINDOW, 2 * WINDOW), F32),
        compiler_params=_params(("arbitrary",)),
        name="swa_bias",
    )(rel_bias, bucket)


def _swa_kernel(sink_ref, q_ref, kvc_ref, kvp_ref, bias_ref, o_ref, *, n_heads):
    group = n_heads // KV_A
    kv = jnp.concatenate([kvp_ref[0], kvc_ref[0]], axis=0)
    lane = lax.broadcasted_iota(I32, (2 * WINDOW, LANES), 1)
    low = lane < HEAD_DIM

    def halves(pair):
        zero = jnp.zeros_like(pair)
        return [jnp.where(low, pair, zero), jnp.where(low, zero, pair)]

    kz = halves(kv[:, 0:LANES])
    vz = halves(kv[:, LANES:2 * LANES])

    n_pairs = n_heads // KV_A
    for p0 in range(0, n_pairs, SWA_PAIRS):
        pairs = range(p0, min(p0 + SWA_PAIRS, n_pairs))
        heads = [(p, g) for p in pairs for g in range(KV_A)]
        logits, e, den, o = {}, {}, {}, {}
        for p, g in heads:
            qp = q_ref[0, :, p * LANES:(p + 1) * LANES]
            s = lax.dot_general(qp, kz[g], (((1,), (1,)), ((), ())), preferred_element_type=F32)
            logits[p, g] = s + bias_ref[0, g * group + p]
        for p, g in heads:
            sink = sink_ref[g * group + p]
            m = jnp.maximum(jnp.max(logits[p, g], axis=-1, keepdims=True), sink)
            e[p, g] = jnp.exp2(logits[p, g] - m)
            den[p, g] = jnp.sum(e[p, g], axis=-1, keepdims=True) + jnp.exp2(sink - m)
        for p, g in heads:
            o[p, g] = jnp.dot(e[p, g].astype(BF16), vz[g], preferred_element_type=F32)
        for p in pairs:
            acc = o[p, 0] * (1.0 / den[p, 0])
            for g in range(1, KV_A):
                acc = acc + o[p, g] * (1.0 / den[p, g])
            o_ref[0, :, p * LANES:(p + 1) * LANES] = acc.astype(BF16)


def _swa(qkv, sinks, bias, d_a, d_b):
    bsz, s, _ = qkv.shape
    n_heads = d_a // HEAD_DIM
    q_blk = 3 * d_b // d_a
    kv_blk = (3 * d_b + d_a) // (2 * LANES)
    return pl.pallas_call(
        functools.partial(_swa_kernel, n_heads=n_heads),
        grid=(bsz, s // WINDOW),
        in_specs=[pl.BlockSpec(memory_space=pltpu.SMEM),
                  pl.BlockSpec((1, WINDOW, d_a), lambda b, i: (b, i, q_blk)),
                  pl.BlockSpec((1, WINDOW, 2 * LANES), lambda b, i: (b, i, kv_blk)),
                  pl.BlockSpec((1, WINDOW, 2 * LANES),
                               lambda b, i: (b, jnp.maximum(i - 1, 0), kv_blk)),
                  pl.BlockSpec((1, n_heads, WINDOW, 2 * WINDOW),
                               lambda b, i: (jnp.minimum(i, 1), 0, 0, 0))],
        out_specs=pl.BlockSpec((1, WINDOW, d_a), lambda b, i: (b, i, 0)),
        out_shape=jax.ShapeDtypeStruct((bsz, s, d_a), BF16),
        compiler_params=_params(("arbitrary", "arbitrary")),
        name="swa",
    )(sinks, qkv, qkv, qkv, bias)


def _suffix_matrix():
    j = np.arange(Q_BLOCK)[:, None]
    s = np.arange(Q_BLOCK)[None, :]
    return np.concatenate([(j > s), np.ones((Q_BLOCK, Q_BLOCK), bool)], axis=1).astype(np.float32)


def _sb_kernel(q_ref, k_ref, v_ref, lt_ref, o_ref, acc_ref, carry_ref, *, group):
    i = pl.program_id(2)
    lane = lax.broadcasted_iota(I32, (Q_BLOCK, LANES), 1)
    low = lane < HEAD_DIM
    row = lax.broadcasted_iota(I32, (2 * Q_BLOCK, Q_BLOCK), 0)
    col = lax.broadcasted_iota(I32, (2 * Q_BLOCK, Q_BLOCK), 1)
    strict = col < jnp.where(row >= Q_BLOCK, row - Q_BLOCK, row)
    sign = jnp.uint32(0x80000000)

    qh = []
    for g in range(group):
        q = q_ref[0, :, g * LANES:(g + 1) * LANES]
        zero = jnp.zeros_like(q)
        qh.append([jnp.where(low, q, zero), jnp.where(low, zero, q)])

    gs = range(group)

    def scores(j, g, lo, hi):
        return lax.dot_general(jnp.concatenate([qh[g][0][lo:hi], qh[g][1][lo:hi]], axis=0),
                               k_ref[0, pl.ds(pl.multiple_of(j * Q_BLOCK, Q_BLOCK), Q_BLOCK),
                                     g * LANES:(g + 1) * LANES],
                               (((1,), (1,)), ((), ())), preferred_element_type=F32)

    def softplus2(z):
        neg_abs = pltpu.bitcast(pltpu.bitcast(z, jnp.uint32) | sign, F32)
        return jnp.maximum(z, 0.0) + jnp.log2(1.0 + jnp.exp2(neg_abs))

    def suffix(sp):
        return jnp.dot(sp.astype(BF16), lt_ref[...], preferred_element_type=F32)

    def weighted_values(a, j, g, rows):
        a = a.astype(BF16)
        a2 = jnp.concatenate([a[:rows], a[rows:]], axis=1)
        vj = v_ref[0, pl.ds(pl.multiple_of(j * Q_BLOCK, Q_BLOCK), Q_BLOCK), g * LANES:(g + 1) * LANES]
        vzero = jnp.zeros_like(vj)
        vz = jnp.concatenate([jnp.where(low, vj, vzero), jnp.where(low, vzero, vj)], axis=0)
        return jnp.dot(a2, vz, preferred_element_type=F32)

    t = SB_TAIL_ROWS

    def carry_mins(carry_min, lo, hi):
        n = hi - lo
        n_top = max(min(hi, t) - lo, 0)
        top = jnp.min(jnp.minimum(carry_min[:n_top], carry_min[n:n + n_top])) if n_top else None
        rest = jnp.min(jnp.minimum(carry_min[n_top:n], carry_min[n + n_top:])) if n_top < n else None
        return top, rest

    def first_blocks(n_before):
        pieces = [(i, 0, Q_BLOCK)]
        if n_before >= 1:
            pieces.append((i - 1, 0, Q_BLOCK))
        if n_before >= 2:
            pieces.append((i - 2, 0, t))
        z = {(b, g): scores(j, g, lo, hi) for b, (j, lo, hi) in enumerate(pieces) for g in gs}
        for g in gs:
            z[0, g] = jnp.where(strict, z[0, g], NEG_INF)
        sp = {bg: softplus2(zz) for bg, zz in z.items()}
        cs = {bg: suffix(s) for bg, s in sp.items()}
        a, carry_min, head_min = {}, None, None
        for g in gs:
            a[0, g] = jnp.exp2(z[0, g] - sp[0, g] - cs[0, g][:, :Q_BLOCK])
            carry = cs[0, g][:, Q_BLOCK:]
            if n_before >= 1:
                a[1, g] = jnp.exp2(z[1, g] - sp[1, g] - cs[1, g][:, :Q_BLOCK] - carry)
                carry = carry + cs[1, g][:, Q_BLOCK:]
            carry_ref[g, 0] = carry[:Q_BLOCK]
            carry_ref[g, 1] = carry[Q_BLOCK:]
            carry_min = carry if carry_min is None else jnp.minimum(carry_min, carry)
            if n_before >= 2:
                head = jnp.concatenate([carry[:t], carry[Q_BLOCK:Q_BLOCK + t]], axis=0)
                a[2, g] = jnp.exp2(z[2, g] - sp[2, g] - cs[2, g][:, :Q_BLOCK] - head)
                head = head + cs[2, g][:, Q_BLOCK:]
                carry_ref[g, 0, :t] = head[:t]
                carry_ref[g, 1, :t] = head[t:]
                head_min = head if head_min is None else jnp.minimum(head_min, head)
        for g in gs:
            acc = weighted_values(a[0, g], i, g, Q_BLOCK)
            if n_before >= 1:
                acc = acc + weighted_values(a[1, g], i - 1, g, Q_BLOCK)
            acc_ref[g] = acc
            if n_before >= 2:
                acc_ref[g, :t] += weighted_values(a[2, g], i - 2, g, t)
        top, rest = carry_mins(carry_min, 0, Q_BLOCK)
        if n_before >= 2:
            top = carry_mins(head_min, 0, t)[0]
        return top, rest

    def block(j, lo, hi):
        n = hi - lo
        z = [scores(j, g, lo, hi) for g in gs]
        sp = [softplus2(zz) for zz in z]
        cs = [suffix(s) for s in sp]
        a, carry_min = [], None
        for g in gs:
            carry = jnp.concatenate([carry_ref[g, 0, lo:hi], carry_ref[g, 1, lo:hi]], axis=0)
            a.append(jnp.exp2(z[g] - sp[g] - cs[g][:, :Q_BLOCK] - carry))
            carry = carry + cs[g][:, Q_BLOCK:]
            carry_ref[g, 0, lo:hi] = carry[:n]
            carry_ref[g, 1, lo:hi] = carry[n:]
            carry_min = carry if carry_min is None else jnp.minimum(carry_min, carry)
        for g in gs:
            acc_ref[g, lo:hi] += weighted_values(a[g], j, g, n)
        return carry_mins(carry_min, lo, hi)

    def two_before():
        top, rest = first_blocks(2)
        rest = lax.cond(rest < SB_SKIP_BITS, lambda: block(i - 2, t, Q_BLOCK)[1], lambda: rest)
        return top, rest

    top0, rest0 = lax.cond(
        i >= 2, two_before,
        lambda: lax.cond(i == 1, lambda: first_blocks(1), lambda: first_blocks(0)))

    def more(state):
        jj, top, rest = state
        return jnp.logical_and(jj < i - 2, jnp.minimum(top, rest) < SB_SKIP_BITS)

    def body(state):
        jj, _, rest = state
        j = i - 3 - jj

        def tail_rows():
            return block(j, 0, t)[0], rest

        def all_rows():
            return block(j, 0, Q_BLOCK)

        top, rest = lax.cond(rest >= SB_SKIP_BITS, tail_rows, all_rows)
        return jj + 1, top, rest

    lax.while_loop(more, body, (jnp.int32(0), top0, rest0))
    for g in range(group):
        o_ref[0, :, g * LANES:(g + 1) * LANES] = acc_ref[g].astype(BF16)


def _sb(qkv, d_b, group=SB_GROUP):
    bsz, s, _ = qkv.shape
    pairs = d_b // LANES
    ng = pairs // group
    w = group * LANES
    lt = jnp.asarray(_suffix_matrix(), BF16)
    return pl.pallas_call(
        functools.partial(_sb_kernel, group=group),
        grid=(bsz, ng, s // Q_BLOCK),
        in_specs=[pl.BlockSpec((1, Q_BLOCK, w), lambda b, p, i: (b, i, p)),
                  pl.BlockSpec((1, s, w), lambda b, p, i: (b, 0, ng + p)),
                  pl.BlockSpec((1, s, w), lambda b, p, i: (b, 0, 2 * ng + p)),
                  pl.BlockSpec((Q_BLOCK, 2 * Q_BLOCK), lambda b, p, i: (0, 0))],
        out_specs=pl.BlockSpec((1, Q_BLOCK, w), lambda b, p, i: (b, i, p)),
        out_shape=jax.ShapeDtypeStruct((bsz, s, d_b), BF16),
        scratch_shapes=[pltpu.VMEM((group, Q_BLOCK, LANES), F32),
                        pltpu.VMEM((group, 2, Q_BLOCK, LANES), F32)],
        compiler_params=_params(("arbitrary", "arbitrary", "arbitrary")),
        name="sb",
    )(qkv, qkv, qkv, lt)


def _layer_norm(y, g, b):
    mu = jnp.mean(y, axis=-1, keepdims=True)
    yc = y - mu
    var = jnp.mean(yc * yc, axis=-1, keepdims=True)
    return yc * lax.rsqrt(var + EPS) * g + b


def _rms(o, g):
    return o * lax.rsqrt(jnp.mean(o * o, axis=-1, keepdims=True) + EPS) * g


def _mix_ln1_kernel(oa_ref, ob_ref, x_ref, mod_ref, na_ref, nb_ref, wo_ref, g_ref, b_ref,
                    wrc_ref, br_ref, x1_ref, h2_ref, lg_ref, *, d_a, parts):
    hm = x_ref.shape[1] // parts
    rows = [pl.ds(p * hm, hm) for p in range(parts)]
    gate1 = mod_ref[0, 2:3, :]
    shift2 = mod_ref[0, 3:4, :]
    scale2 = mod_ref[0, 4:5, :]
    ra = [_rms(oa_ref[0, r, :].astype(F32), na_ref[...]).astype(BF16) for r in rows]
    rb = [_rms(ob_ref[0, r, :].astype(F32), nb_ref[...]).astype(BF16) for r in rows]
    mix = [jnp.dot(ra[p], wo_ref[:d_a, :], preferred_element_type=F32)
           + jnp.dot(rb[p], wo_ref[d_a:, :], preferred_element_type=F32) for p in range(parts)]
    hi, lo = [], []
    for p, r in enumerate(rows):
        x1 = _layer_norm(ALPHA * x_ref[0, r, :] + (1.0 + gate1) * mix[p], g_ref[...], b_ref[...])
        x1_ref[0, r, :] = x1
        h2 = x1 * (1.0 + scale2) + shift2
        _store_row_tiles(h2_ref.at[0, pl.ds(p * hm * ROW_SUB, hm * ROW_SUB)], h2)
        hi.append(h2.astype(BF16))
        lo.append((h2 - hi[p].astype(F32)).astype(BF16))
    for p, r in enumerate(rows):
        both = jnp.dot(hi[p], wrc_ref[...], preferred_element_type=F32)
        lg_ref[0, r, :] = (both[:, :LANES] + both[:, LANES:]
                           + jnp.dot(lo[p], wrc_ref[:, :LANES], preferred_element_type=F32)
                           + br_ref[...])


def _mix_ln1(o_a, o_b, x, mod3, norm_a, norm_b, w_out_bf, ln_g, ln_b, wr_cat, b_r, tm=512, parts=2):
    bsz, s, d = x.shape
    d_a = o_a.shape[-1]
    d_b = o_b.shape[-1]
    tm = min(tm, s)
    row = lambda b, i: (b, i, 0)
    const2 = lambda b, i: (0, 0)
    once = pl.Buffered(1)
    return pl.pallas_call(
        functools.partial(_mix_ln1_kernel, d_a=d_a, parts=parts),
        grid=(bsz, s // tm),
        in_specs=[pl.BlockSpec((1, tm, d_a), row),
                  pl.BlockSpec((1, tm, d_b), row),
                  pl.BlockSpec((1, tm, d), row),
                  pl.BlockSpec((1, 6, d), lambda b, i: (b, 0, 0)),
                  pl.BlockSpec((1, d_a), const2),
                  pl.BlockSpec((1, d_b), const2),
                  pl.BlockSpec((d_a + d_b, d), const2, pipeline_mode=once),
                  pl.BlockSpec((1, d), const2),
                  pl.BlockSpec((1, d), const2),
                  pl.BlockSpec((d, 2 * LANES), const2, pipeline_mode=once),
                  pl.BlockSpec((1, LANES), const2)],
        out_specs=[pl.BlockSpec((1, tm, d), row),
                   pl.BlockSpec((1, tm * ROW_SUB, LANES), row),
                   pl.BlockSpec((1, tm, LANES), row)],
        out_shape=[jax.ShapeDtypeStruct((bsz, s, d), F32),
                   jax.ShapeDtypeStruct((bsz, s * ROW_SUB, LANES), jnp.uint32),
                   jax.ShapeDtypeStruct((bsz, s, LANES), F32)],
        compiler_params=_params(("arbitrary", "arbitrary")),
        name="mix_ln1",
    )(o_a, o_b, x, mod3, norm_a, norm_b, w_out_bf, ln_g, ln_b, wr_cat, b_r)


def _route_kernel(lg_ref, tri_ref, sel_ref, gate_ref, cnt_ref, base_ref):
    step = pl.program_id(0)

    @pl.when(step == 0)
    def _():
        base_ref[...] = jnp.zeros_like(base_ref)

    lg = lg_ref[...]
    tm = lg.shape[0]
    lane = lax.broadcasted_iota(I32, (tm, LANES), 1)
    big = jnp.int32(2 * LANES)
    glog = jnp.where(lane < N_GROUPS, lg, -jnp.inf)
    gmax = jnp.max(glog, axis=-1, keepdims=True)
    g_sel = jnp.min(jnp.where(glog == gmax, lane, big), axis=-1, keepdims=True)
    p_g = 1.0 / jnp.sum(jnp.exp(glog - gmax), axis=-1, keepdims=True)
    lo = N_GROUPS + g_sel * EXPERTS_PER_GROUP
    in_grp = jnp.logical_and(lane >= lo, lane < lo + EXPERTS_PER_GROUP)
    el = jnp.where(in_grp, lg, -jnp.inf)
    v1 = jnp.max(el, axis=-1, keepdims=True)
    i1 = jnp.min(jnp.where(el == v1, lane, big), axis=-1, keepdims=True)
    el2 = jnp.where(lane == i1, -jnp.inf, el)
    v2 = jnp.max(el2, axis=-1, keepdims=True)
    i2 = jnp.min(jnp.where(el2 == v2, lane, big), axis=-1, keepdims=True)
    r = jnp.exp(v2 - v1)
    w1 = 1.0 / (1.0 + r)
    g1 = p_g * w1
    g2 = p_g * (r * w1)
    e1 = i1 - N_GROUPS
    e2 = i2 - N_GROUPS
    oh1 = (lane == e1)
    oh2 = (lane == e2)
    occ = oh1.astype(F32) + oh2.astype(F32)
    before = jnp.dot(tri_ref[...], occ.astype(BF16), preferred_element_type=F32) + base_ref[...]
    r1 = jnp.sum(jnp.where(oh1, before, 0.0), axis=-1, keepdims=True)
    r2 = jnp.sum(jnp.where(oh2, before, 0.0), axis=-1, keepdims=True)
    base_ref[...] += jnp.sum(occ, axis=0, keepdims=True)
    cnt_ref[...] = base_ref[...]
    sel = jnp.where(lane == 0, e1, jnp.where(lane == 1, e2, 0))
    sel = jnp.where(lane == 2, r1.astype(I32), jnp.where(lane == 3, r2.astype(I32), sel))
    sel_ref[...] = sel
    gate_ref[...] = jnp.where(lane == 0, g1, jnp.where(lane == 1, g2, 0.0))


def _route(logits, tm=1024):
    t = logits.shape[0]
    tm = min(tm, t)
    tri = jnp.asarray(np.tril(np.ones((tm, tm), np.float32), -1), BF16)
    return pl.pallas_call(
        _route_kernel,
        grid=(t // tm,),
        in_specs=[pl.BlockSpec((tm, LANES), lambda i: (i, 0)),
                  pl.BlockSpec((tm, tm), lambda i: (0, 0))],
        out_specs=[pl.BlockSpec((tm, LANES), lambda i: (i, 0)),
                   pl.BlockSpec((tm, LANES), lambda i: (i, 0)),
                   pl.BlockSpec((1, LANES), lambda i: (0, 0))],
        out_shape=[jax.ShapeDtypeStruct((t, LANES), I32),
                   jax.ShapeDtypeStruct((t, LANES), F32),
                   jax.ShapeDtypeStruct((1, LANES), F32)],
        scratch_shapes=[pltpu.VMEM((1, LANES), F32)],
        compiler_params=_params(("arbitrary",)),
        name="route",
    )(logits, tri)


def _lane_prefix(x, lane):
    shift = 1
    while shift < LANES:
        x = x + jnp.where(lane >= shift, pltpu.roll(x, shift, axis=1), 0)
        shift *= 2
    return x


def _dest_kernel(sel_ref, cnt_ref, dest_ref, blk_ref, plan_ref, *, n_blk_pad):
    tm = sel_ref.shape[0]
    lane1 = lax.broadcasted_iota(I32, (8, LANES), 1)
    cnt = jnp.broadcast_to(cnt_ref[...].astype(I32), (8, LANES))
    cnt = jnp.where(lane1 < N_EXPERTS, cnt, 0)
    padded = jnp.bitwise_and(cnt + (MOE_TM - 1), -MOE_TM)
    pend = _lane_prefix(padded, lane1)
    pstart = (pend - padded)[0:1, :]
    sel = sel_ref[...]
    lane = lax.broadcasted_iota(I32, (tm, LANES), 1)
    e1 = sel[:, 0:1]
    e2 = sel[:, 1:2]
    d1 = jnp.sum(jnp.where(lane == e1, pstart, 0), axis=-1, keepdims=True) + sel[:, 2:3]
    d2 = jnp.sum(jnp.where(lane == e2, pstart, 0), axis=-1, keepdims=True) + sel[:, 3:4]
    dest_ref[...] = jnp.where(lane == 0, d1, jnp.where(lane == 1, d2, 0)) * ROW_SUB

    @pl.when(pl.program_id(0) == 0)
    def _():
        brow = lax.broadcasted_iota(I32, (n_blk_pad, LANES), 0) * MOE_TM
        blane = lax.broadcasted_iota(I32, (n_blk_pad, LANES), 1)
        ended = jnp.logical_and(blane < N_EXPERTS, pend[0:1, :] <= brow)
        be = jnp.minimum(jnp.sum(ended.astype(I32), axis=-1, keepdims=True), N_EXPERTS - 1)
        blk_ref[...] = jnp.broadcast_to(be, (n_blk_pad, LANES))
        sub = lax.broadcasted_iota(I32, (8, LANES), 0)
        used = jnp.max(pend, axis=-1, keepdims=True) >> (MOE_TM.bit_length() - 1)
        plan_ref[...] = jnp.where(sub == 0, (pend - padded + cnt) * ROW_SUB,
                                  jnp.where(sub == 1, padded - cnt, used))


def _dest(sel, counts, n_blk, tm=1024):
    t = sel.shape[0]
    tm = min(tm, t)
    n_blk_pad = -(-n_blk // 8) * 8
    return pl.pallas_call(
        functools.partial(_dest_kernel, n_blk_pad=n_blk_pad),
        grid=(t // tm,),
        in_specs=[pl.BlockSpec((tm, LANES), lambda i: (i, 0)),
                  pl.BlockSpec((1, LANES), lambda i: (0, 0))],
        out_specs=[pl.BlockSpec((tm, LANES), lambda i: (i, 0)),
                   pl.BlockSpec((n_blk_pad, LANES), lambda i: (0, 0)),
                   pl.BlockSpec((8, LANES), lambda i: (0, 0))],
        out_shape=[jax.ShapeDtypeStruct((t, LANES), I32),
                   jax.ShapeDtypeStruct((n_blk_pad, LANES), I32),
                   jax.ShapeDtypeStruct((8, LANES), I32)],
        compiler_params=_params(("arbitrary",)),
        name="dest",
    )(sel, counts)


def _dispatch_kernel(dest_ref, plan_ref, h_ref, xs_ref, stage, zeros, sems, zsem, *, n_blk):
    tm = h_ref.shape[0] // ROW_SUB
    step = pl.program_id(0)
    last = pl.num_programs(0) - 1
    slot = lax.rem(step, 2)
    base = step * (2 * tm)
    blk_sub = MOE_TM * ROW_SUB

    def fill(wait):
        def run(copy):
            copy.wait() if wait else copy.start()

        def pads(e, carry):
            first = plan_ref[e]
            n = plan_ref[N_EXPERTS + e]
            bit = MOE_TM // 2
            while bit >= 1:
                @pl.when(jnp.bitwise_and(n, bit) != 0)
                def _(bit=bit):
                    done = jnp.bitwise_and(n, -2 * bit)
                    dst = pl.multiple_of(first + done * ROW_SUB, ROW_SUB)
                    run(pltpu.make_async_copy(zeros.at[pl.ds(0, bit * ROW_SUB)],
                                              xs_ref.at[pl.ds(dst, bit * ROW_SUB)], zsem))
                bit //= 2
            return carry

        lax.fori_loop(0, N_EXPERTS, pads, 0)

        def unused(b, carry):
            dst = pl.multiple_of(b * blk_sub, blk_sub)
            run(pltpu.make_async_copy(zeros, xs_ref.at[pl.ds(dst, blk_sub)], zsem))
            return carry

        lax.fori_loop(plan_ref[2 * N_EXPERTS], n_blk, unused, 0)

    @pl.when(step == 0)
    def _():
        zeros[...] = jnp.zeros_like(zeros)
        fill(wait=False)
        fill(wait=True)

    def wait_tile(sl):
        for _ in range(2):
            pltpu.make_async_copy(stage.at[sl], xs_ref.at[pl.ds(0, tm * ROW_SUB)], sems.at[sl]).wait()

    @pl.when(step >= 2)
    def _():
        wait_tile(slot)

    stage[slot] = h_ref[...]

    def start(c, carry):
        for u in range(ROW_UNROLL):
            r = c * ROW_UNROLL + u
            src = stage.at[slot, pl.ds(pl.multiple_of(r * ROW_SUB, ROW_SUB), ROW_SUB)]
            for k in range(2):
                dst = pl.multiple_of(dest_ref[base + 2 * r + k], ROW_SUB)
                pltpu.make_async_copy(src, xs_ref.at[pl.ds(dst, ROW_SUB)], sems.at[slot]).start()
        return carry

    lax.fori_loop(0, tm // ROW_UNROLL, start, 0)

    @pl.when(step == last)
    def _():
        wait_tile(slot)

        @pl.when(step >= 1)
        def _():
            wait_tile(1 - slot)


def _dispatch(dest_flat, plan_flat, h2p, m_pad, tm=256):
    t = h2p.shape[0] // ROW_SUB
    tm = min(tm, t)
    return pl.pallas_call(
        functools.partial(_dispatch_kernel, n_blk=m_pad // MOE_TM),
        grid_spec=pltpu.PrefetchScalarGridSpec(
            num_scalar_prefetch=2,
            grid=(t // tm,),
            in_specs=[pl.BlockSpec((tm * ROW_SUB, LANES), lambda i, dest, plan: (i, 0))],
            out_specs=pl.BlockSpec(memory_space=pl.ANY),
            scratch_shapes=[pltpu.VMEM((2, tm * ROW_SUB, LANES), h2p.dtype),
                            pltpu.VMEM((MOE_TM * ROW_SUB, LANES), h2p.dtype),
                            pltpu.SemaphoreType.DMA((2,)),
                            pltpu.SemaphoreType.DMA(())]),
        out_shape=jax.ShapeDtypeStruct((m_pad * ROW_SUB, LANES), h2p.dtype),
        compiler_params=_params(("arbitrary",)),
        name="dispatch",
    )(dest_flat, plan_flat, h2p)


def _experts_kernel(blk_ref, xs_ref, wg_hbm, wu_hbm, wd_hbm, ys_ref,
                    wg_f32, wu_f32, wd_f32, wg_bf, wu_bf, wd_bf, slot_ref, sems):
    i = pl.program_id(0)
    n_blk = pl.num_programs(0)
    n_used = blk_ref[n_blk]
    e = blk_ref[i]
    in_use = i < n_used
    first_of_run = jnp.logical_and(
        in_use, jnp.logical_or(i == 0, blk_ref[jnp.maximum(i - 1, 0)] != e))

    def weight_copies(expert, sl):
        return [pltpu.make_async_copy(wg_hbm.at[0, expert], wg_f32.at[sl], sems.at[sl]),
                pltpu.make_async_copy(wu_hbm.at[0, expert], wu_f32.at[sl], sems.at[sl]),
                pltpu.make_async_copy(wd_hbm.at[0, expert], wd_f32.at[sl], sems.at[sl])]

    @pl.when(i == 0)
    def _():
        slot_ref[0] = 0
        for c in weight_copies(e, 0):
            c.start()

    @pl.when(first_of_run)
    def _():
        sl = slot_ref[0]
        for c in weight_copies(e, sl):
            c.wait()
        wg_bf[...] = wg_f32[sl].astype(BF16)
        wu_bf[...] = wu_f32[sl].astype(BF16)
        wd_bf[...] = wd_f32[sl].astype(BF16)
        nxt = lax.while_loop(
            lambda j: jnp.logical_and(j < n_used, blk_ref[jnp.minimum(j, n_blk - 1)] == e),
            lambda j: j + 1, i + 1)

        @pl.when(nxt < n_used)
        def _():
            for c in weight_copies(blk_ref[nxt], 1 - sl):
                c.start(priority=1)

        slot_ref[0] = 1 - sl

    @pl.when(in_use)
    def _():
        xb = jnp.concatenate([c.astype(BF16) for c in _load_row_tiles(xs_ref, MOE_TM)], axis=1)
        g = jnp.dot(xb, wg_bf[...], preferred_element_type=F32)
        u = jnp.dot(xb, wu_bf[...], preferred_element_type=F32)
        hmid = (g * jax.nn.sigmoid(g) * u).astype(BF16)
        _store_row_tiles(ys_ref, jnp.dot(hmid, wd_bf[...], preferred_element_type=F32))

    @pl.when(jnp.logical_not(in_use))
    def _():
        ys_ref[...] = jnp.zeros_like(ys_ref)


def _experts(blk_expert, xs, w_gate, w_up, w_down):
    d, de = w_gate.shape[-2:]
    n_blk = xs.shape[0] // (MOE_TM * ROW_SUB)
    return pl.pallas_call(
        _experts_kernel,
        grid_spec=pltpu.PrefetchScalarGridSpec(
            num_scalar_prefetch=1,
            grid=(n_blk,),
            in_specs=[pl.BlockSpec((MOE_TM * ROW_SUB, LANES), lambda i, blk: (i, 0)),
                      pl.BlockSpec(memory_space=pl.ANY),
                      pl.BlockSpec(memory_space=pl.ANY),
                      pl.BlockSpec(memory_space=pl.ANY)],
            out_specs=pl.BlockSpec((MOE_TM * ROW_SUB, LANES), lambda i, blk: (i, 0)),
            scratch_shapes=[pltpu.VMEM((2, d, de), F32),
                            pltpu.VMEM((2, d, de), F32),
                            pltpu.VMEM((2, de, d), F32),
                            pltpu.VMEM((d, de), BF16),
                            pltpu.VMEM((d, de), BF16),
                            pltpu.VMEM((de, d), BF16),
                            pltpu.SMEM((1,), I32),
                            pltpu.SemaphoreType.DMA((2,))]),
        out_shape=jax.ShapeDtypeStruct(xs.shape, jnp.uint32),
        compiler_params=_params(("arbitrary",)),
        name="experts",
    )(blk_expert, xs, w_gate, w_up, w_down)


def _combine_kernel(dest_ref, ys_ref, gate_ref, x1_ref, mod_ref, g_ref, b_ref, o_ref, *scratch):
    ybufs, sems = scratch[:COMBINE_SLOTS], scratch[COMBINE_SLOTS]
    tm = x1_ref.shape[1]
    n_steps = pl.num_programs(0) * pl.num_programs(1)
    step = pl.program_id(0) * pl.num_programs(1) + pl.program_id(1)

    def row_copy(base, r, k, sl):
        src = pl.multiple_of(dest_ref[base + 2 * r + k], ROW_SUB)
        return pltpu.make_async_copy(ys_ref.at[pl.ds(src, ROW_SUB)],
                                     ybufs[sl].at[k, pl.ds(pl.multiple_of(r * ROW_SUB, ROW_SUB), ROW_SUB)],
                                     sems.at[sl])

    def wait_slot(sl):
        for k in range(2):
            pltpu.make_async_copy(ys_ref.at[pl.ds(0, tm * ROW_SUB)], ybufs[sl].at[k], sems.at[sl]).wait()

    @pl.when(step == 0)
    def _():
        for ahead in range(COMBINE_SLOTS - 1):
            base = jnp.minimum(ahead, n_steps - 1) * (2 * tm)

            def start(c, carry, base=base, ahead=ahead):
                for u in range(ROW_UNROLL):
                    for k in range(2):
                        row_copy(base, c * ROW_UNROLL + u, k, ahead).start()
                return carry

            lax.fori_loop(0, tm // ROW_UNROLL, start, 0)

    def run(sl):
        wait_slot(sl)
        ahead_base = jnp.minimum(step + COMBINE_SLOTS - 1, n_steps - 1) * (2 * tm)
        ahead_slot = (sl + COMBINE_SLOTS - 1) % COMBINE_SLOTS
        for r in range(tm):
            for k in range(2):
                row_copy(ahead_base, r, k, ahead_slot).start()

        gates = gate_ref[0]
        g0 = gates[:, 0:1]
        g1 = gates[:, 1:2]
        y0 = _load_row_tiles(ybufs[sl].at[0], tm)
        y1 = _load_row_tiles(ybufs[sl].at[1], tm)
        ffn = jnp.concatenate([g0 * a + g1 * b for a, b in zip(y0, y1)], axis=1)
        gate2 = mod_ref[0, 5:6, :]
        o_ref[0] = _layer_norm(ALPHA * x1_ref[0] + (1.0 + gate2) * ffn, g_ref[...], b_ref[...])

        @pl.when(step == n_steps - 1)
        def _():
            for ahead in range(1, COMBINE_SLOTS):
                wait_slot((sl + ahead) % COMBINE_SLOTS)

    for sl in range(COMBINE_SLOTS):
        pl.when(lax.rem(step, COMBINE_SLOTS) == sl)(functools.partial(run, sl))


def _combine(dest_flat, ys, gates3, x1, mod3, ln_g, ln_b, tm=256):
    bsz, s, d = x1.shape
    tm = min(tm, s)
    return pl.pallas_call(
        _combine_kernel,
        grid_spec=pltpu.PrefetchScalarGridSpec(
            num_scalar_prefetch=1,
            grid=(bsz, s // tm),
            in_specs=[pl.BlockSpec(memory_space=pl.ANY),
                      pl.BlockSpec((1, tm, LANES), lambda b, i, dest: (b, i, 0)),
                      pl.BlockSpec((1, tm, d), lambda b, i, dest: (b, i, 0)),
                      pl.BlockSpec((1, 6, d), lambda b, i, dest: (b, 0, 0)),
                      pl.BlockSpec((1, d), lambda b, i, dest: (0, 0)),
                      pl.BlockSpec((1, d), lambda b, i, dest: (0, 0))],
            out_specs=pl.BlockSpec((1, tm, d), lambda b, i, dest: (b, i, 0)),
            scratch_shapes=[pltpu.VMEM((2, tm * ROW_SUB, LANES), jnp.uint32)] * COMBINE_SLOTS
            + [pltpu.SemaphoreType.DMA((COMBINE_SLOTS,))]),
        out_shape=jax.ShapeDtypeStruct((bsz, s, d), F32),
        compiler_params=_params(("arbitrary", "arbitrary")),
        name="combine",
    )(dest_flat, ys, gates3, x1, mod3, ln_g, ln_b)


def kernel(x, c, w_in, w_out, sinks, rel_bias, norm_a, norm_b, w_ada, b_ada, ln1_g, ln1_b,
           ln2_g, ln2_b, w_grp, b_grp, w_rtr, b_rtr, w_gate, w_up, w_down):
    bsz, s, d = x.shape
    t = bsz * s
    d_a = norm_a.shape[-1]
    d_b = norm_b.shape[-1]

    mod3 = _adaln(c, w_ada, b_ada).reshape(bsz, 6, d)

    assert KV_A * HEAD_DIM == LANES
    kv_w = 2 * KV_A * HEAD_DIM
    group_a = d_a // HEAD_DIM // KV_A

    def pair_heads(a, axis):
        shape = a.shape[:axis] + (KV_A, group_a, HEAD_DIM) + a.shape[axis + 1:]
        return jnp.swapaxes(a.reshape(shape), axis, axis + 1).reshape(a.shape)

    w0 = w_in[0]
    w_in_bf = jnp.concatenate(
        [w0[:, d_a + kv_w:d_a + kv_w + d_b] * (ATTN_SCALE * LOG2E),
         w0[:, d_a + kv_w + d_b:],
         pair_heads(w0[:, :d_a], 1) * (ATTN_SCALE * LOG2E),
         w0[:, d_a:d_a + kv_w]], axis=1).astype(BF16)
    qkv = _qkv(x, mod3, w_in_bf)
    norm_a = pair_heads(norm_a, 1)
    w_out_bf = jnp.concatenate([pair_heads(w_out[0][:d_a], 0), w_out[0][d_a:]], axis=0).astype(BF16)

    o_a = _swa(qkv, sinks[0] * LOG2E, _swa_bias(rel_bias), d_a, d_b)
    o_b = _sb(qkv, d_b)

    w_r = jnp.concatenate([w_grp[0], w_rtr[0]], axis=1)
    w_r = jnp.pad(w_r, ((0, 0), (0, LANES - w_r.shape[1])))
    b_r = jnp.pad(jnp.concatenate([b_grp[0], b_rtr[0]]), (0, LANES - N_GROUPS - N_EXPERTS))[None, :]
    wr_hi = w_r.astype(BF16)
    wr_lo = (w_r - wr_hi.astype(F32)).astype(BF16)
    assert d == 2 * LANES * ROW_SUB, "row tiles hold 256 * ROW_SUB features"
    x1, h2, logits = _mix_ln1(o_a, o_b, x, mod3, norm_a, norm_b, w_out_bf,
                              ln1_g, ln1_b, jnp.concatenate([wr_hi, wr_lo], axis=1), b_r)

    sel, gates, counts = _route(logits.reshape(t, LANES))
    m_pad = 2 * t + N_EXPERTS * MOE_TM
    n_blk = m_pad // MOE_TM
    dest, blk, plan = _dest(sel, counts, n_blk)
    dest_flat = dest[:, :2].reshape(2 * t)
    blk_expert = jnp.concatenate([blk[:n_blk, 0], plan[2, :1]])
    plan_flat = jnp.concatenate([plan[0, :N_EXPERTS], plan[1, :N_EXPERTS], plan[2, :1]])

    xs = _dispatch(dest_flat, plan_flat, h2.reshape(t * ROW_SUB, LANES), m_pad)
    ys = _experts(blk_expert, xs, w_gate, w_up, w_down)
    return _combine(dest_flat, ys, gates.reshape(bsz, s, LANES), x1, mod3, ln2_g, ln2_b)
```

```python
import functools
import math

import jax
import jax.numpy as jnp
import numpy as np
from jax import lax
from jax.experimental import pallas as pl
from jax.experimental.pallas import tpu as pltpu

F32 = jnp.float32
BF16 = jnp.bfloat16
I32 = jnp.int32

HEAD_DIM = 64
KV_A = 2
NUM_BUCKETS = 32
MAX_DISTANCE = 128
WINDOW = 128
Q_BLOCK = 128
N_GROUPS = 4
EXPERTS_PER_GROUP = 8
N_EXPERTS = N_GROUPS * EXPERTS_PER_GROUP
DEPTH = 1
ALPHA = (2.0 * DEPTH) ** 0.25
ATTN_SCALE = 1.0 / math.sqrt(HEAD_DIM)
EPS = 1e-5
NEG_INF = -1e30
LOG2E = math.log2(math.e)

LANES = 128
ROW_SUB = 8
MOE_TM = 256
COMBINE_SLOTS = 3
ROW_UNROLL = 8
SWA_PAIRS = 4
SB_GROUP = 8
SB_TAIL_ROWS = 48
SB_SKIP_BITS = 150.0
VMEM_LIMIT = 48 * 1024 * 1024


def _params(sem, vmem=VMEM_LIMIT):
    return pltpu.CompilerParams(dimension_semantics=sem, vmem_limit_bytes=vmem)


def _store_row_tiles(ref_2d, y):
    n = y.shape[0]
    for s in range(ROW_SUB):
        lo = pltpu.bitcast(y[:, 2 * s * LANES:(2 * s + 1) * LANES].astype(BF16).astype(F32), jnp.uint32)
        hi = pltpu.bitcast(y[:, (2 * s + 1) * LANES:(2 * s + 2) * LANES].astype(BF16).astype(F32), jnp.uint32)
        ref_2d[pl.ds(s, n, stride=ROW_SUB), :] = hi | (lo >> 16)


def _load_row_tiles(ref_2d, n):
    chunks = []
    for s in range(ROW_SUB):
        p = ref_2d[pl.ds(s, n, stride=ROW_SUB), :]
        chunks.append(pltpu.bitcast(p << 16, F32))
        chunks.append(pltpu.bitcast(p & jnp.uint32(0xFFFF0000), F32))
    return chunks


def _adaln_kernel(c_ref, w_ref, b_ref, o_ref):
    c = c_ref[...]
    ca = (c * jax.nn.sigmoid(c)).astype(BF16)
    o_ref[...] = jnp.dot(ca, w_ref[0].astype(BF16), preferred_element_type=F32) + b_ref[...]


def _adaln(c, w_ada, b_ada, tn=1024):
    bsz, d = c.shape
    n = w_ada.shape[-1]
    return pl.pallas_call(
        _adaln_kernel,
        grid=(n // tn,),
        in_specs=[pl.BlockSpec((bsz, d), lambda j: (0, 0)),
                  pl.BlockSpec((1, d, tn), lambda j: (0, 0, j)),
                  pl.BlockSpec((1, tn), lambda j: (0, j))],
        out_specs=pl.BlockSpec((bsz, tn), lambda j: (0, j)),
        out_shape=jax.ShapeDtypeStruct((bsz, n), F32),
        compiler_params=_params(("arbitrary",)),
        name="adaln",
    )(c, w_ada, b_ada)


def _qkv_kernel(x_ref, mod_ref, w_ref, o_ref):
    shift = mod_ref[0, 0:1, :]
    scale = mod_ref[0, 1:2, :]
    h = (x_ref[0] * (1.0 + scale) + shift).astype(BF16)
    o_ref[0] = jnp.dot(h, w_ref[...], preferred_element_type=F32).astype(BF16)


def _qkv(x, mod3, w_in_bf, tm=512, nj=2):
    bsz, s, d = x.shape
    n = w_in_bf.shape[1]
    tn = n // nj
    tm = min(tm, s)
    return pl.pallas_call(
        _qkv_kernel,
        grid=(nj, bsz, s // tm),
        in_specs=[pl.BlockSpec((1, tm, d), lambda j, b, i: (b, i, 0)),
                  pl.BlockSpec((1, 6, d), lambda j, b, i: (b, 0, 0)),
                  pl.BlockSpec((d, tn), lambda j, b, i: (0, j))],
        out_specs=pl.BlockSpec((1, tm, tn), lambda j, b, i: (b, i, j)),
        out_shape=jax.ShapeDtypeStruct((bsz, s, n), BF16),
        compiler_params=_params(("arbitrary", "arbitrary", "arbitrary")),
        name="qkv",
    )(x, mod3, w_in_bf)


def _bucket_map():
    qi = np.arange(WINDOW)[:, None]
    kj = np.arange(2 * WINDOW)[None, :]
    dist = qi + WINDOW - kj
    n = np.maximum(dist, 0)
    max_exact = NUM_BUCKETS // 2
    ratio = np.maximum(n, max_exact).astype(np.float32) / np.float32(max_exact)
    large = max_exact + (np.log(ratio) / np.float32(math.log(MAX_DISTANCE / max_exact))
                         * np.float32(NUM_BUCKETS - max_exact)).astype(np.int32)
    large = np.minimum(large, NUM_BUCKETS - 1)
    bucket = np.where(n < max_exact, n, large)
    band = (dist >= 0) & (dist < WINDOW)
    return np.where(band, bucket, -1).astype(np.int32)


def _swa_bias_kernel(rb_ref, bucket_ref, o_ref):
    first = pl.program_id(0) == 0
    bucket = bucket_ref[...]
    col = lax.broadcasted_iota(I32, bucket.shape, 1)
    hidden = jnp.logical_and(first, col < WINDOW)
    for h in range(o_ref.shape[1]):
        acc = jnp.full(bucket.shape, NEG_INF, F32)
        for b in range(NUM_BUCKETS):
            acc = jnp.where(bucket == b, rb_ref[b, h] * LOG2E, acc)
        o_ref[0, h] = jnp.where(hidden, NEG_INF, acc)


def _swa_bias(rel_bias):
    nh = rel_bias.shape[1]
    bucket = jnp.asarray(_bucket_map())
    return pl.pallas_call(
        _swa_bias_kernel,
        grid=(2,),
        in_specs=[pl.BlockSpec(memory_space=pltpu.SMEM),
                  pl.BlockSpec((WINDOW, 2 * WINDOW), lambda v: (0, 0))],
        out_specs=pl.BlockSpec((1, nh, WINDOW, 2 * WINDOW), lambda v: (v, 0, 0, 0)),
        out_shape=jax.ShapeDtypeStruct((2, nh, WINDOW, 2 * WINDOW), F32),
        compiler_params=_params(("arbitrary",)),
        name="swa_bias",
    )(rel_bias, bucket)


def _swa_kernel(sink_ref, q_ref, kvc_ref, kvp_ref, bias_ref, o_ref, *, n_heads):
    group = n_heads // KV_A
    kv = jnp.concatenate([kvp_ref[0], kvc_ref[0]], axis=0)
    lane = lax.broadcasted_iota(I32, (2 * WINDOW, LANES), 1)
    low = lane < HEAD_DIM

    def halves(pair):
        zero = jnp.zeros_like(pair)
        return [jnp.where(low, pair, zero), jnp.where(low, zero, pair)]

    kz = halves(kv[:, 0:LANES])
    vz = halves(kv[:, LANES:2 * LANES])

    n_pairs = n_heads // KV_A
    for p0 in range(0, n_pairs, SWA_PAIRS):
        pairs = range(p0, min(p0 + SWA_PAIRS, n_pairs))
        heads = [(p, g) for p in pairs for g in range(KV_A)]
        logits, e, den, o = {}, {}, {}, {}
        for p, g in heads:
            qp = q_ref[0, :, p * LANES:(p + 1) * LANES]
            s = lax.dot_general(qp, kz[g], (((1,), (1,)), ((), ())), preferred_element_type=F32)
            logits[p, g] = s + bias_ref[0, g * group + p]
        for p, g in heads:
            sink = sink_ref[g * group + p]
            m = jnp.maximum(jnp.max(logits[p, g], axis=-1, keepdims=True), sink)
            e[p, g] = jnp.exp2(logits[p, g] - m)
            den[p, g] = jnp.sum(e[p, g], axis=-1, keepdims=True) + jnp.exp2(sink - m)
        for p, g in heads:
            o[p, g] = jnp.dot(e[p, g].astype(BF16), vz[g], preferred_element_type=F32)
        for p in pairs:
            acc = o[p, 0] * (1.0 / den[p, 0])
            for g in range(1, KV_A):
                acc = acc + o[p, g] * (1.0 / den[p, g])
            o_ref[0, :, p * LANES:(p + 1) * LANES] = acc.astype(BF16)


def _swa(qkv, sinks, bias, d_a, d_b):
    bsz, s, _ = qkv.shape
    n_heads = d_a // HEAD_DIM
    q_blk = 3 * d_b // d_a
    kv_blk = (3 * d_b + d_a) // (2 * LANES)
    return pl.pallas_call(
        functools.partial(_swa_kernel, n_heads=n_heads),
        grid=(bsz, s // WINDOW),
        in_specs=[pl.BlockSpec(memory_space=pltpu.SMEM),
                  pl.BlockSpec((1, WINDOW, d_a), lambda b, i: (b, i, q_blk)),
                  pl.BlockSpec((1, WINDOW, 2 * LANES), lambda b, i: (b, i, kv_blk)),
                  pl.BlockSpec((1, WINDOW, 2 * LANES),
                               lambda b, i: (b, jnp.maximum(i - 1, 0), kv_blk)),
                  pl.BlockSpec((1, n_heads, WINDOW, 2 * WINDOW),
                               lambda b, i: (jnp.minimum(i, 1), 0, 0, 0))],
        out_specs=pl.BlockSpec((1, WINDOW, d_a), lambda b, i: (b, i, 0)),
        out_shape=jax.ShapeDtypeStruct((bsz, s, d_a), BF16),
        compiler_params=_params(("arbitrary", "arbitrary")),
        name="swa",
    )(sinks, qkv, qkv, qkv, bias)


def _suffix_matrix():
    j = np.arange(Q_BLOCK)[:, None]
    s = np.arange(Q_BLOCK)[None, :]
    return np.concatenate([(j > s), np.ones((Q_BLOCK, Q_BLOCK), bool)], axis=1).astype(np.float32)


def _sb_kernel(q_ref, k_ref, v_ref, lt_ref, o_ref, acc_ref, carry_ref, *, group):
    i = pl.program_id(2)
    lane = lax.broadcasted_iota(I32, (Q_BLOCK, LANES), 1)
    low = lane < HEAD_DIM
    row = lax.broadcasted_iota(I32, (2 * Q_BLOCK, Q_BLOCK), 0)
    col = lax.broadcasted_iota(I32, (2 * Q_BLOCK, Q_BLOCK), 1)
    strict = col < jnp.where(row >= Q_BLOCK, row - Q_BLOCK, row)
    sign = jnp.uint32(0x80000000)

    qh = []
    for g in range(group):
        q = q_ref[0, :, g * LANES:(g + 1) * LANES]
        zero = jnp.zeros_like(q)
        qh.append([jnp.where(low, q, zero), jnp.where(low, zero, q)])

    gs = range(group)

    def scores(j, g, lo, hi):
        return lax.dot_general(jnp.concatenate([qh[g][0][lo:hi], qh[g][1][lo:hi]], axis=0),
                               k_ref[0, pl.ds(pl.multiple_of(j * Q_BLOCK, Q_BLOCK), Q_BLOCK),
                                     g * LANES:(g + 1) * LANES],
                               (((1,), (1,)), ((), ())), preferred_element_type=F32)

    def softplus2(z):
        neg_abs = pltpu.bitcast(pltpu.bitcast(z, jnp.uint32) | sign, F32)
        return jnp.maximum(z, 0.0) + jnp.log2(1.0 + jnp.exp2(neg_abs))

    def suffix(sp):
        return jnp.dot(sp.astype(BF16), lt_ref[...], preferred_element_type=F32)

    def weighted_values(a, j, g, rows):
        a = a.astype(BF16)
        a2 = jnp.concatenate([a[:rows], a[rows:]], axis=1)
        vj = v_ref[0, pl.ds(pl.multiple_of(j * Q_BLOCK, Q_BLOCK), Q_BLOCK), g * LANES:(g + 1) * LANES]
        vzero = jnp.zeros_like(vj)
        vz = jnp.concatenate([jnp.where(low, vj, vzero), jnp.where(low, vzero, vj)], axis=0)
        return jnp.dot(a2, vz, preferred_element_type=F32)

    t = SB_TAIL_ROWS

    def carry_mins(carry_min, lo, hi):
        n = hi - lo
        n_top = max(min(hi, t) - lo, 0)
        top = jnp.min(jnp.minimum(carry_min[:n_top], carry_min[n:n + n_top])) if n_top else None
        rest = jnp.min(jnp.minimum(carry_min[n_top:n], carry_min[n + n_top:])) if n_top < n else None
        return top, rest

    def first_blocks(n_before):
        pieces = [(i, 0, Q_BLOCK)]
        if n_before >= 1:
            pieces.append((i - 1, 0, Q_BLOCK))
        if n_before >= 2:
            pieces.append((i - 2, 0, t))
        z = {(b, g): scores(j, g, lo, hi) for b, (j, lo, hi) in enumerate(pieces) for g in gs}
        for g in gs:
            z[0, g] = jnp.where(strict, z[0, g], NEG_INF)
        sp = {bg: softplus2(zz) for bg, zz in z.items()}
        cs = {bg: suffix(s) for bg, s in sp.items()}
        a, carry_min, head_min = {}, None, None
        for g in gs:
            a[0, g] = jnp.exp2(z[0, g] - sp[0, g] - cs[0, g][:, :Q_BLOCK])
            carry = cs[0, g][:, Q_BLOCK:]
            if n_before >= 1:
                a[1, g] = jnp.exp2(z[1, g] - sp[1, g] - cs[1, g][:, :Q_BLOCK] - carry)
                carry = carry + cs[1, g][:, Q_BLOCK:]
            carry_ref[g, 0] = carry[:Q_BLOCK]
            carry_ref[g, 1] = carry[Q_BLOCK:]
            carry_min = carry if carry_min is None else jnp.minimum(carry_min, carry)
            if n_before >= 2:
                head = jnp.concatenate([carry[:t], carry[Q_BLOCK:Q_BLOCK + t]], axis=0)
                a[2, g] = jnp.exp2(z[2, g] - sp[2, g] - cs[2, g][:, :Q_BLOCK] - head)
                head = head + cs[2, g][:, Q_BLOCK:]
                carry_ref[g, 0, :t] = head[:t]
                carry_ref[g, 1, :t] = head[t:]
                head_min = head if head_min is None else jnp.minimum(head_min, head)
        for g in gs:
            acc = weighted_values(a[0, g], i, g, Q_BLOCK)
            if n_before >= 1:
                acc = acc + weighted_values(a[1, g], i - 1, g, Q_BLOCK)
            acc_ref[g] = acc
            if n_before >= 2:
                acc_ref[g, :t] += weighted_values(a[2, g], i - 2, g, t)
        top, rest = carry_mins(carry_min, 0, Q_BLOCK)
        if n_before >= 2:
            top = carry_mins(head_min, 0, t)[0]
        return top, rest

    def block(j, lo, hi):
        n = hi - lo
        z = [scores(j, g, lo, hi) for g in gs]
        sp = [softplus2(zz) for zz in z]
        cs = [suffix(s) for s in sp]
        a, carry_min = [], None
        for g in gs:
            carry = jnp.concatenate([carry_ref[g, 0, lo:hi], carry_ref[g, 1, lo:hi]], axis=0)
            a.append(jnp.exp2(z[g] - sp[g] - cs[g][:, :Q_BLOCK] - carry))
            carry = carry + cs[g][:, Q_BLOCK:]
            carry_ref[g, 0, lo:hi] = carry[:n]
            carry_ref[g, 1, lo:hi] = carry[n:]
            carry_min = carry if carry_min is None else jnp.minimum(carry_min, carry)
        for g in gs:
            acc_ref[g, lo:hi] += weighted_values(a[g], j, g, n)
        return carry_mins(carry_min, lo, hi)

    def two_before():
        top, rest = first_blocks(2)
        rest = lax.cond(rest < SB_SKIP_BITS, lambda: block(i - 2, t, Q_BLOCK)[1], lambda: rest)
        return top, rest

    top0, rest0 = lax.cond(
        i >= 2, two_before,
        lambda: lax.cond(i == 1, lambda: first_blocks(1), lambda: first_blocks(0)))

    def more(state):
        jj, top, rest = state
        return jnp.logical_and(jj < i - 2, jnp.minimum(top, rest) < SB_SKIP_BITS)

    def body(state):
        jj, _, rest = state
        j = i - 3 - jj

        def tail_rows():
            return block(j, 0, t)[0], rest

        def all_rows():
            return block(j, 0, Q_BLOCK)

        top, rest = lax.cond(rest >= SB_SKIP_BITS, tail_rows, all_rows)
        return jj + 1, top, rest

    lax.while_loop(more, body, (jnp.int32(0), top0, rest0))
    for g in range(group):
        o_ref[0, :, g * LANES:(g + 1) * LANES] = acc_ref[g].astype(BF16)


def _sb(qkv, d_b, group=SB_GROUP):
    bsz, s, _ = qkv.shape
    pairs = d_b // LANES
    ng = pairs // group
    w = group * LANES
    lt = jnp.asarray(_suffix_matrix(), BF16)
    return pl.pallas_call(
        functools.partial(_sb_kernel, group=group),
        grid=(bsz, ng, s // Q_BLOCK),
        in_specs=[pl.BlockSpec((1, Q_BLOCK, w), lambda b, p, i: (b, i, p)),
                  pl.BlockSpec((1, s, w), lambda b, p, i: (b, 0, ng + p)),
                  pl.BlockSpec((1, s, w), lambda b, p, i: (b, 0, 2 * ng + p)),
                  pl.BlockSpec((Q_BLOCK, 2 * Q_BLOCK), lambda b, p, i: (0, 0))],
        out_specs=pl.BlockSpec((1, Q_BLOCK, w), lambda b, p, i: (b, i, p)),
        out_shape=jax.ShapeDtypeStruct((bsz, s, d_b), BF16),
        scratch_shapes=[pltpu.VMEM((group, Q_BLOCK, LANES), F32),
                        pltpu.VMEM((group, 2, Q_BLOCK, LANES), F32)],
        compiler_params=_params(("arbitrary", "arbitrary", "arbitrary")),
        name="sb",
    )(qkv, qkv, qkv, lt)


def _layer_norm(y, g, b):
    mu = jnp.mean(y, axis=-1, keepdims=True)
    yc = y - mu
    var = jnp.mean(yc * yc, axis=-1, keepdims=True)
    return yc * lax.rsqrt(var + EPS) * g + b


def _rms(o, g):
    return o * lax.rsqrt(jnp.mean(o * o, axis=-1, keepdims=True) + EPS) * g


def _mix_ln1_kernel(oa_ref, ob_ref, x_ref, mod_ref, na_ref, nb_ref, wo_ref, g_ref, b_ref,
                    wrc_ref, br_ref, x1_ref, h2_ref, lg_ref, *, d_a, parts):
    hm = x_ref.shape[1] // parts
    rows = [pl.ds(p * hm, hm) for p in range(parts)]
    gate1 = mod_ref[0, 2:3, :]
    shift2 = mod_ref[0, 3:4, :]
    scale2 = mod_ref[0, 4:5, :]
    ra = [_rms(oa_ref[0, r, :].astype(F32), na_ref[...]).astype(BF16) for r in rows]
    rb = [_rms(ob_ref[0, r, :].astype(F32), nb_ref[...]).astype(BF16) for r in rows]
    mix = [jnp.dot(ra[p], wo_ref[:d_a, :], preferred_element_type=F32)
           + jnp.dot(rb[p], wo_ref[d_a:, :], preferred_element_type=F32) for p in range(parts)]
    hi, lo = [], []
    for p, r in enumerate(rows):
        x1 = _layer_norm(ALPHA * x_ref[0, r, :] + (1.0 + gate1) * mix[p], g_ref[...], b_ref[...])
        x1_ref[0, r, :] = x1
        h2 = x1 * (1.0 + scale2) + shift2
        _store_row_tiles(h2_ref.at[0, pl.ds(p * hm * ROW_SUB, hm * ROW_SUB)], h2)
        hi.append(h2.astype(BF16))
        lo.append((h2 - hi[p].astype(F32)).astype(BF16))
    for p, r in enumerate(rows):
        both = jnp.dot(hi[p], wrc_ref[...], preferred_element_type=F32)
        lg_ref[0, r, :] = (both[:, :LANES] + both[:, LANES:]
                           + jnp.dot(lo[p], wrc_ref[:, :LANES], preferred_element_type=F32)
                           + br_ref[...])


def _mix_ln1(o_a, o_b, x, mod3, norm_a, norm_b, w_out_bf, ln_g, ln_b, wr_cat, b_r, tm=512, parts=2):
    bsz, s, d = x.shape
    d_a = o_a.shape[-1]
    d_b = o_b.shape[-1]
    tm = min(tm, s)
    row = lambda b, i: (b, i, 0)
    const2 = lambda b, i: (0, 0)
    once = pl.Buffered(1)
    return pl.pallas_call(
        functools.partial(_mix_ln1_kernel, d_a=d_a, parts=parts),
        grid=(bsz, s // tm),
        in_specs=[pl.BlockSpec((1, tm, d_a), row),
                  pl.BlockSpec((1, tm, d_b), row),
                  pl.BlockSpec((1, tm, d), row),
                  pl.BlockSpec((1, 6, d), lambda b, i: (b, 0, 0)),
                  pl.BlockSpec((1, d_a), const2),
                  pl.BlockSpec((1, d_b), const2),
                  pl.BlockSpec((d_a + d_b, d), const2, pipeline_mode=once),
                  pl.BlockSpec((1, d), const2),
                  pl.BlockSpec((1, d), const2),
                  pl.BlockSpec((d, 2 * LANES), const2, pipeline_mode=once),
                  pl.BlockSpec((1, LANES), const2)],
        out_specs=[pl.BlockSpec((1, tm, d), row),
                   pl.BlockSpec((1, tm * ROW_SUB, LANES), row),
                   pl.BlockSpec((1, tm, LANES), row)],
        out_shape=[jax.ShapeDtypeStruct((bsz, s, d), F32),
                   jax.ShapeDtypeStruct((bsz, s * ROW_SUB, LANES), jnp.uint32),
                   jax.ShapeDtypeStruct((bsz, s, LANES), F32)],
        compiler_params=_params(("arbitrary", "arbitrary")),
        name="mix_ln1",
    )(o_a, o_b, x, mod3, norm_a, norm_b, w_out_bf, ln_g, ln_b, wr_cat, b_r)


def _route_kernel(lg_ref, tri_ref, sel_ref, gate_ref, cnt_ref, base_ref):
    step = pl.program_id(0)

    @pl.when(step == 0)
    def _():
        base_ref[...] = jnp.zeros_like(base_ref)

    lg = lg_ref[...]
    tm = lg.shape[0]
    lane = lax.broadcasted_iota(I32, (tm, LANES), 1)
    big = jnp.int32(2 * LANES)
    glog = jnp.where(lane < N_GROUPS, lg, -jnp.inf)
    gmax = jnp.max(glog, axis=-1, keepdims=True)
    g_sel = jnp.min(jnp.where(glog == gmax, lane, big), axis=-1, keepdims=True)
    p_g = 1.0 / jnp.sum(jnp.exp(glog - gmax), axis=-1, keepdims=True)
    lo = N_GROUPS + g_sel * EXPERTS_PER_GROUP
    in_grp = jnp.logical_and(lane >= lo, lane < lo + EXPERTS_PER_GROUP)
    el = jnp.where(in_grp, lg, -jnp.inf)
    v1 = jnp.max(el, axis=-1, keepdims=True)
    i1 = jnp.min(jnp.where(el == v1, lane, big), axis=-1, keepdims=True)
    el2 = jnp.where(lane == i1, -jnp.inf, el)
    v2 = jnp.max(el2, axis=-1, keepdims=True)
    i2 = jnp.min(jnp.where(el2 == v2, lane, big), axis=-1, keepdims=True)
    r = jnp.exp(v2 - v1)
    w1 = 1.0 / (1.0 + r)
    g1 = p_g * w1
    g2 = p_g * (r * w1)
    e1 = i1 - N_GROUPS
    e2 = i2 - N_GROUPS
    oh1 = (lane == e1)
    oh2 = (lane == e2)
    occ = oh1.astype(F32) + oh2.astype(F32)
    before = jnp.dot(tri_ref[...], occ.astype(BF16), preferred_element_type=F32) + base_ref[...]
    r1 = jnp.sum(jnp.where(oh1, before, 0.0), axis=-1, keepdims=True)
    r2 = jnp.sum(jnp.where(oh2, before, 0.0), axis=-1, keepdims=True)
    base_ref[...] += jnp.sum(occ, axis=0, keepdims=True)
    cnt_ref[...] = base_ref[...]
    sel = jnp.where(lane == 0, e1, jnp.where(lane == 1, e2, 0))
    sel = jnp.where(lane == 2, r1.astype(I32), jnp.where(lane == 3, r2.astype(I32), sel))
    sel_ref[...] = sel
    gate_ref[...] = jnp.where(lane == 0, g1, jnp.where(lane == 1, g2, 0.0))


def _route(logits, tm=1024):
    t = logits.shape[0]
    tm = min(tm, t)
    tri = jnp.asarray(np.tril(np.ones((tm, tm), np.float32), -1), BF16)
    return pl.pallas_call(
        _route_kernel,
        grid=(t // tm,),
        in_specs=[pl.BlockSpec((tm, LANES), lambda i: (i, 0)),
                  pl.BlockSpec((tm, tm), lambda i: (0, 0))],
        out_specs=[pl.BlockSpec((tm, LANES), lambda i: (i, 0)),
                   pl.BlockSpec((tm, LANES), lambda i: (i, 0)),
                   pl.BlockSpec((1, LANES), lambda i: (0, 0))],
        out_shape=[jax.ShapeDtypeStruct((t, LANES), I32),
                   jax.ShapeDtypeStruct((t, LANES), F32),
                   jax.ShapeDtypeStruct((1, LANES), F32)],
        scratch_shapes=[pltpu.VMEM((1, LANES), F32)],
        compiler_params=_params(("arbitrary",)),
        name="route",
    )(logits, tri)


def _lane_prefix(x, lane):
    shift = 1
    while shift < LANES:
        x = x + jnp.where(lane >= shift, pltpu.roll(x, shift, axis=1), 0)
        shift *= 2
    return x


def _dest_kernel(sel_ref, cnt_ref, dest_ref, blk_ref, plan_ref, *, n_blk_pad):
    tm = sel_ref.shape[0]
    lane1 = lax.broadcasted_iota(I32, (8, LANES), 1)
    cnt = jnp.broadcast_to(cnt_ref[...].astype(I32), (8, LANES))
    cnt = jnp.where(lane1 < N_EXPERTS, cnt, 0)
    padded = jnp.bitwise_and(cnt + (MOE_TM - 1), -MOE_TM)
    pend = _lane_prefix(padded, lane1)
    pstart = (pend - padded)[0:1, :]
    sel = sel_ref[...]
    lane = lax.broadcasted_iota(I32, (tm, LANES), 1)
    e1 = sel[:, 0:1]
    e2 = sel[:, 1:2]
    d1 = jnp.sum(jnp.where(lane == e1, pstart, 0), axis=-1, keepdims=True) + sel[:, 2:3]
    d2 = jnp.sum(jnp.where(lane == e2, pstart, 0), axis=-1, keepdims=True) + sel[:, 3:4]
    dest_ref[...] = jnp.where(lane == 0, d1, jnp.where(lane == 1, d2, 0)) * ROW_SUB

    @pl.when(pl.program_id(0) == 0)
    def _():
        brow = lax.broadcasted_iota(I32, (n_blk_pad, LANES), 0) * MOE_TM
        blane = lax.broadcasted_iota(I32, (n_blk_pad, LANES), 1)
        ended = jnp.logical_and(blane < N_EXPERTS, pend[0:1, :] <= brow)
        be = jnp.minimum(jnp.sum(ended.astype(I32), axis=-1, keepdims=True), N_EXPERTS - 1)
        blk_ref[...] = jnp.broadcast_to(be, (n_blk_pad, LANES))
        sub = lax.broadcasted_iota(I32, (8, LANES), 0)
        used = jnp.max(pend, axis=-1, keepdims=True) >> (MOE_TM.bit_length() - 1)
        plan_ref[...] = jnp.where(sub == 0, (pend - padded + cnt) * ROW_SUB,
                                  jnp.where(sub == 1, padded - cnt, used))


def _dest(sel, counts, n_blk, tm=1024):
    t = sel.shape[0]
    tm = min(tm, t)
    n_blk_pad = -(-n_blk // 8) * 8
    return pl.pallas_call(
        functools.partial(_dest_kernel, n_blk_pad=n_blk_pad),
        grid=(t // tm,),
        in_specs=[pl.BlockSpec((tm, LANES), lambda i: (i, 0)),
                  pl.BlockSpec((1, LANES), lambda i: (0, 0))],
        out_specs=[pl.BlockSpec((tm, LANES), lambda i: (i, 0)),
                   pl.BlockSpec((n_blk_pad, LANES), lambda i: (0, 0)),
                   pl.BlockSpec((8, LANES), lambda i: (0, 0))],
        out_shape=[jax.ShapeDtypeStruct((t, LANES), I32),
                   jax.ShapeDtypeStruct((n_blk_pad, LANES), I32),
                   jax.ShapeDtypeStruct((8, LANES), I32)],
        compiler_params=_params(("arbitrary",)),
        name="dest",
    )(sel, counts)


def _dispatch_kernel(dest_ref, plan_ref, h_ref, xs_ref, stage, zeros, sems, zsem, *, n_blk):
    tm = h_ref.shape[0] // ROW_SUB
    step = pl.program_id(0)
    last = pl.num_programs(0) - 1
    slot = lax.rem(step, 2)
    base = step * (2 * tm)
    blk_sub = MOE_TM * ROW_SUB

    def fill(wait):
        def run(copy):
            copy.wait() if wait else copy.start()

        def pads(e, carry):
            first = plan_ref[e]
            n = plan_ref[N_EXPERTS + e]
            bit = MOE_TM // 2
            while bit >= 1:
                @pl.when(jnp.bitwise_and(n, bit) != 0)
                def _(bit=bit):
                    done = jnp.bitwise_and(n, -2 * bit)
                    dst = pl.multiple_of(first + done * ROW_SUB, ROW_SUB)
                    run(pltpu.make_async_copy(zeros.at[pl.ds(0, bit * ROW_SUB)],
                                              xs_ref.at[pl.ds(dst, bit * ROW_SUB)], zsem))
                bit //= 2
            return carry

        lax.fori_loop(0, N_EXPERTS, pads, 0)

        def unused(b, carry):
            dst = pl.multiple_of(b * blk_sub, blk_sub)
            run(pltpu.make_async_copy(zeros, xs_ref.at[pl.ds(dst, blk_sub)], zsem))
            return carry

        lax.fori_loop(plan_ref[2 * N_EXPERTS], n_blk, unused, 0)

    @pl.when(step == 0)
    def _():
        zeros[...] = jnp.zeros_like(zeros)
        fill(wait=False)
        fill(wait=True)

    def wait_tile(sl):
        for _ in range(2):
            pltpu.make_async_copy(stage.at[sl], xs_ref.at[pl.ds(0, tm * ROW_SUB)], sems.at[sl]).wait()

    @pl.when(step >= 2)
    def _():
        wait_tile(slot)

    stage[slot] = h_ref[...]

    def start(c, carry):
        for u in range(ROW_UNROLL):
            r = c * ROW_UNROLL + u
            src = stage.at[slot, pl.ds(pl.multiple_of(r * ROW_SUB, ROW_SUB), ROW_SUB)]
            for k in range(2):
                dst = pl.multiple_of(dest_ref[base + 2 * r + k], ROW_SUB)
                pltpu.make_async_copy(src, xs_ref.at[pl.ds(dst, ROW_SUB)], sems.at[slot]).start()
        return carry

    lax.fori_loop(0, tm // ROW_UNROLL, start, 0)

    @pl.when(step == last)
    def _():
        wait_tile(slot)

        @pl.when(step >= 1)
        def _():
            wait_tile(1 - slot)


def _dispatch(dest_flat, plan_flat, h2p, m_pad, tm=256):
    t = h2p.shape[0] // ROW_SUB
    tm = min(tm, t)
    return pl.pallas_call(
        functools.partial(_dispatch_kernel, n_blk=m_pad // MOE_TM),
        grid_spec=pltpu.PrefetchScalarGridSpec(
            num_scalar_prefetch=2,
            grid=(t // tm,),
            in_specs=[pl.BlockSpec((tm * ROW_SUB, LANES), lambda i, dest, plan: (i, 0))],
            out_specs=pl.BlockSpec(memory_space=pl.ANY),
            scratch_shapes=[pltpu.VMEM((2, tm * ROW_SUB, LANES), h2p.dtype),
                            pltpu.VMEM((MOE_TM * ROW_SUB, LANES), h2p.dtype),
                            pltpu.SemaphoreType.DMA((2,)),
                            pltpu.SemaphoreType.DMA(())]),
        out_shape=jax.ShapeDtypeStruct((m_pad * ROW_SUB, LANES), h2p.dtype),
        compiler_params=_params(("arbitrary",)),
        name="dispatch",
    )(dest_flat, plan_flat, h2p)


def _experts_kernel(blk_ref, xs_ref, wg_hbm, wu_hbm, wd_hbm, ys_ref,
                    wg_f32, wu_f32, wd_f32, wg_bf, wu_bf, wd_bf, slot_ref, sems):
    i = pl.program_id(0)
    n_blk = pl.num_programs(0)
    n_used = blk_ref[n_blk]
    e = blk_ref[i]
    in_use = i < n_used
    first_of_run = jnp.logical_and(
        in_use, jnp.logical_or(i == 0, blk_ref[jnp.maximum(i - 1, 0)] != e))

    def weight_copies(expert, sl):
        return [pltpu.make_async_copy(wg_hbm.at[0, expert], wg_f32.at[sl], sems.at[sl]),
                pltpu.make_async_copy(wu_hbm.at[0, expert], wu_f32.at[sl], sems.at[sl]),
                pltpu.make_async_copy(wd_hbm.at[0, expert], wd_f32.at[sl], sems.at[sl])]

    @pl.when(i == 0)
    def _():
        slot_ref[0] = 0
        for c in weight_copies(e, 0):
            c.start()

    @pl.when(first_of_run)
    def _():
        sl = slot_ref[0]
        for c in weight_copies(e, sl):
            c.wait()
        wg_bf[...] = wg_f32[sl].astype(BF16)
        wu_bf[...] = wu_f32[sl].astype(BF16)
        wd_bf[...] = wd_f32[sl].astype(BF16)
        nxt = lax.while_loop(
            lambda j: jnp.logical_and(j < n_used, blk_ref[jnp.minimum(j, n_blk - 1)] == e),
            lambda j: j + 1, i + 1)

        @pl.when(nxt < n_used)
        def _():
            for c in weight_copies(blk_ref[nxt], 1 - sl):
                c.start(priority=1)

        slot_ref[0] = 1 - sl

    @pl.when(in_use)
    def _():
        xb = jnp.concatenate([c.astype(BF16) for c in _load_row_tiles(xs_ref, MOE_TM)], axis=1)
        g = jnp.dot(xb, wg_bf[...], preferred_element_type=F32)
        u = jnp.dot(xb, wu_bf[...], preferred_element_type=F32)
        hmid = (g * jax.nn.sigmoid(g) * u).astype(BF16)
        _store_row_tiles(ys_ref, jnp.dot(hmid, wd_bf[...], preferred_element_type=F32))

    @pl.when(jnp.logical_not(in_use))
    def _():
        ys_ref[...] = jnp.zeros_like(ys_ref)


def _experts(blk_expert, xs, w_gate, w_up, w_down):
    d, de = w_gate.shape[-2:]
    n_blk = xs.shape[0] // (MOE_TM * ROW_SUB)
    return pl.pallas_call(
        _experts_kernel,
        grid_spec=pltpu.PrefetchScalarGridSpec(
            num_scalar_prefetch=1,
            grid=(n_blk,),
            in_specs=[pl.BlockSpec((MOE_TM * ROW_SUB, LANES), lambda i, blk: (i, 0)),
                      pl.BlockSpec(memory_space=pl.ANY),
                      pl.BlockSpec(memory_space=pl.ANY),
                      pl.BlockSpec(memory_space=pl.ANY)],
            out_specs=pl.BlockSpec((MOE_TM * ROW_SUB, LANES), lambda i, blk: (i, 0)),
            scratch_shapes=[pltpu.VMEM((2, d, de), F32),
                            pltpu.VMEM((2, d, de), F32),
                            pltpu.VMEM((2, de, d), F32),
                            pltpu.VMEM((d, de), BF16),
                            pltpu.VMEM((d, de), BF16),
                            pltpu.VMEM((de, d), BF16),
                            pltpu.SMEM((1,), I32),
                            pltpu.SemaphoreType.DMA((2,))]),
        out_shape=jax.ShapeDtypeStruct(xs.shape, jnp.uint32),
        compiler_params=_params(("arbitrary",)),
        name="experts",
    )(blk_expert, xs, w_gate, w_up, w_down)


def _combine_kernel(dest_ref, ys_ref, gate_ref, x1_ref, mod_ref, g_ref, b_ref, o_ref, *scratch):
    ybufs, sems = scratch[:COMBINE_SLOTS], scratch[COMBINE_SLOTS]
    tm = x1_ref.shape[1]
    n_steps = pl.num_programs(0) * pl.num_programs(1)
    step = pl.program_id(0) * pl.num_programs(1) + pl.program_id(1)

    def row_copy(base, r, k, sl):
        src = pl.multiple_of(dest_ref[base + 2 * r + k], ROW_SUB)
        return pltpu.make_async_copy(ys_ref.at[pl.ds(src, ROW_SUB)],
                                     ybufs[sl].at[k, pl.ds(pl.multiple_of(r * ROW_SUB, ROW_SUB), ROW_SUB)],
                                     sems.at[sl])

    def wait_slot(sl):
        for k in range(2):
            pltpu.make_async_copy(ys_ref.at[pl.ds(0, tm * ROW_SUB)], ybufs[sl].at[k], sems.at[sl]).wait()

    @pl.when(step == 0)
    def _():
        for ahead in range(COMBINE_SLOTS - 1):
            base = jnp.minimum(ahead, n_steps - 1) * (2 * tm)

            def start(c, carry, base=base, ahead=ahead):
                for u in range(ROW_UNROLL):
                    for k in range(2):
                        row_copy(base, c * ROW_UNROLL + u, k, ahead).start()
                return carry

            lax.fori_loop(0, tm // ROW_UNROLL, start, 0)

    def run(sl):
        wait_slot(sl)
        ahead_base = jnp.minimum(step + COMBINE_SLOTS - 1, n_steps - 1) * (2 * tm)
        ahead_slot = (sl + COMBINE_SLOTS - 1) % COMBINE_SLOTS
        for r in range(tm):
            for k in range(2):
                row_copy(ahead_base, r, k, ahead_slot).start()

        gates = gate_ref[0]
        g0 = gates[:, 0:1]
        g1 = gates[:, 1:2]
        y0 = _load_row_tiles(ybufs[sl].at[0], tm)
        y1 = _load_row_tiles(ybufs[sl].at[1], tm)
        ffn = jnp.concatenate([g0 * a + g1 * b for a, b in zip(y0, y1)], axis=1)
        gate2 = mod_ref[0, 5:6, :]
        o_ref[0] = _layer_norm(ALPHA * x1_ref[0] + (1.0 + gate2) * ffn, g_ref[...], b_ref[...])

        @pl.when(step == n_steps - 1)
        def _():
            for ahead in range(1, COMBINE_SLOTS):
                wait_slot((sl + ahead) % COMBINE_SLOTS)

    for sl in range(COMBINE_SLOTS):
        pl.when(lax.rem(step, COMBINE_SLOTS) == sl)(functools.partial(run, sl))


def _combine(dest_flat, ys, gates3, x1, mod3, ln_g, ln_b, tm=256):
    bsz, s, d = x1.shape
    tm = min(tm, s)
    return pl.pallas_call(
        _combine_kernel,
        grid_spec=pltpu.PrefetchScalarGridSpec(
            num_scalar_prefetch=1,
            grid=(bsz, s // tm),
            in_specs=[pl.BlockSpec(memory_space=pl.ANY),
                      pl.BlockSpec((1, tm, LANES), lambda b, i, dest: (b, i, 0)),
                      pl.BlockSpec((1, tm, d), lambda b, i, dest: (b, i, 0)),
                      pl.BlockSpec((1, 6, d), lambda b, i, dest: (b, 0, 0)),
                      pl.BlockSpec((1, d), lambda b, i, dest: (0, 0)),
                      pl.BlockSpec((1, d), lambda b, i, dest: (0, 0))],
            out_specs=pl.BlockSpec((1, tm, d), lambda b, i, dest: (b, i, 0)),
            scratch_shapes=[pltpu.VMEM((2, tm * ROW_SUB, LANES), jnp.uint32)] * COMBINE_SLOTS
            + [pltpu.SemaphoreType.DMA((COMBINE_SLOTS,))]),
        out_shape=jax.ShapeDtypeStruct((bsz, s, d), F32),
        compiler_params=_params(("arbitrary", "arbitrary")),
        name="combine",
    )(dest_flat, ys, gates3, x1, mod3, ln_g, ln_b)


def kernel(x, c, w_in, w_out, sinks, rel_bias, norm_a, norm_b, w_ada, b_ada, ln1_g, ln1_b,
           ln2_g, ln2_b, w_grp, b_grp, w_rtr, b_rtr, w_gate, w_up, w_down):
    bsz, s, d = x.shape
    t = bsz * s
    d_a = norm_a.shape[-1]
    d_b = norm_b.shape[-1]

    mod3 = _adaln(c, w_ada, b_ada).reshape(bsz, 6, d)

    assert KV_A * HEAD_DIM == LANES
    kv_w = 2 * KV_A * HEAD_DIM
    group_a = d_a // HEAD_DIM // KV_A

    def pair_heads(a, axis):
        shape = a.shape[:axis] + (KV_A, group_a, HEAD_DIM) + a.shape[axis + 1:]
        return jnp.swapaxes(a.reshape(shape), axis, axis + 1).reshape(a.shape)

    w0 = w_in[0]
    w_in_bf = jnp.concatenate(
        [w0[:, d_a + kv_w:d_a + kv_w + d_b] * (ATTN_SCALE * LOG2E),
         w0[:, d_a + kv_w + d_b:],
         pair_heads(w0[:, :d_a], 1) * (ATTN_SCALE * LOG2E),
         w0[:, d_a:d_a + kv_w]], axis=1).astype(BF16)
    qkv = _qkv(x, mod3, w_in_bf)
    norm_a = pair_heads(norm_a, 1)
    w_out_bf = jnp.concatenate([pair_heads(w_out[0][:d_a], 0), w_out[0][d_a:]], axis=0).astype(BF16)

    o_a = _swa(qkv, sinks[0] * LOG2E, _swa_bias(rel_bias), d_a, d_b)
    o_b = _sb(qkv, d_b)

    n_r = N_GROUPS + N_EXPERTS
    w_r = jnp.zeros((d, LANES), F32).at[:, :N_GROUPS].set(w_grp[0]).at[:, N_GROUPS:n_r].set(w_rtr[0])
    b_r = jnp.zeros((1, LANES), F32).at[0, :N_GROUPS].set(b_grp[0]).at[0, N_GROUPS:n_r].set(b_rtr[0])
    wr_hi = w_r.astype(BF16)
    wr_lo = (w_r - wr_hi.astype(F32)).astype(BF16)
    assert d == 2 * LANES * ROW_SUB, "row tiles hold 256 * ROW_SUB features"
    x1, h2, logits = _mix_ln1(o_a, o_b, x, mod3, norm_a, norm_b, w_out_bf,
                              ln1_g, ln1_b, jnp.concatenate([wr_hi, wr_lo], axis=1), b_r)

    sel, gates, counts = _route(logits.reshape(t, LANES))
    m_pad = 2 * t + N_EXPERTS * MOE_TM
    n_blk = m_pad // MOE_TM
    dest, blk, plan = _dest(sel, counts, n_blk)
    dest_flat = dest[:, :2].reshape(2 * t)
    blk_expert = jnp.concatenate([blk[:n_blk, 0], plan[2, :1]])
    plan_flat = jnp.concatenate([plan[0, :N_EXPERTS], plan[1, :N_EXPERTS], plan[2, :1]])

    xs = _dispatch(dest_flat, plan_flat, h2.reshape(t * ROW_SUB, LANES), m_pad)
    ys = _experts(blk_expert, xs, w_gate, w_up, w_down)
    return _combine(dest_flat, ys, gates.reshape(bsz, s, LANES), x1, mod3, ln2_g, ln2_b)
```

```python
import functools
import math

import jax
import jax.numpy as jnp
import numpy as np
from jax import lax
from jax.experimental import pallas as pl
from jax.experimental.pallas import tpu as pltpu

F32 = jnp.float32
BF16 = jnp.bfloat16
I32 = jnp.int32

HEAD_DIM = 64
KV_A = 2
NUM_BUCKETS = 32
MAX_DISTANCE = 128
WINDOW = 128
Q_BLOCK = 128
N_GROUPS = 4
EXPERTS_PER_GROUP = 8
N_EXPERTS = N_GROUPS * EXPERTS_PER_GROUP
DEPTH = 1
ALPHA = (2.0 * DEPTH) ** 0.25
ATTN_SCALE = 1.0 / math.sqrt(HEAD_DIM)
EPS = 1e-5
NEG_INF = -1e30
LOG2E = math.log2(math.e)

LANES = 128
ROW_SUB = 8
MOE_TM = 256
COMBINE_SLOTS = 3
ROW_UNROLL = 8
SWA_PAIRS = 4
SB_GROUP = 8
SB_TAIL_ROWS = 48
SB_SKIP_BITS = 150.0
VMEM_LIMIT = 48 * 1024 * 1024


def _params(sem, vmem=VMEM_LIMIT):
    return pltpu.CompilerParams(dimension_semantics=sem, vmem_limit_bytes=vmem)


def _store_row_tiles(ref_2d, y):
    n = y.shape[0]
    for s in range(ROW_SUB):
        lo = pltpu.bitcast(y[:, 2 * s * LANES:(2 * s + 1) * LANES].astype(BF16).astype(F32), jnp.uint32)
        hi = pltpu.bitcast(y[:, (2 * s + 1) * LANES:(2 * s + 2) * LANES].astype(BF16).astype(F32), jnp.uint32)
        ref_2d[pl.ds(s, n, stride=ROW_SUB), :] = hi | (lo >> 16)


def _load_row_tiles(ref_2d, n):
    chunks = []
    for s in range(ROW_SUB):
        p = ref_2d[pl.ds(s, n, stride=ROW_SUB), :]
        chunks.append(pltpu.bitcast(p << 16, F32))
        chunks.append(pltpu.bitcast(p & jnp.uint32(0xFFFF0000), F32))
    return chunks


def _adaln_kernel(c_ref, w_ref, b_ref, o_ref):
    c = c_ref[...]
    ca = (c * jax.nn.sigmoid(c)).astype(BF16)
    o_ref[...] = jnp.dot(ca, w_ref[0].astype(BF16), preferred_element_type=F32) + b_ref[...]


def _adaln(c, w_ada, b_ada, tn=1024):
    bsz, d = c.shape
    n = w_ada.shape[-1]
    return pl.pallas_call(
        _adaln_kernel,
        grid=(n // tn,),
        in_specs=[pl.BlockSpec((bsz, d), lambda j: (0, 0)),
                  pl.BlockSpec((1, d, tn), lambda j: (0, 0, j)),
                  pl.BlockSpec((1, tn), lambda j: (0, j))],
        out_specs=pl.BlockSpec((bsz, tn), lambda j: (0, j)),
        out_shape=jax.ShapeDtypeStruct((bsz, n), F32),
        compiler_params=_params(("arbitrary",)),
        name="adaln",
    )(c, w_ada, b_ada)


def _qkv_kernel(x_ref, mod_ref, w_ref, o_ref):
    shift = mod_ref[0, 0:1, :]
    scale = mod_ref[0, 1:2, :]
    h = (x_ref[0] * (1.0 + scale) + shift).astype(BF16)
    o_ref[0] = jnp.dot(h, w_ref[...], preferred_element_type=F32).astype(BF16)


def _qkv(x, mod3, w_in_bf, tm=512, nj=2):
    bsz, s, d = x.shape
    n = w_in_bf.shape[1]
    tn = n // nj
    tm = min(tm, s)
    return pl.pallas_call(
        _qkv_kernel,
        grid=(nj, bsz, s // tm),
        in_specs=[pl.BlockSpec((1, tm, d), lambda j, b, i: (b, i, 0)),
                  pl.BlockSpec((1, 6, d), lambda j, b, i: (b, 0, 0)),
                  pl.BlockSpec((d, tn), lambda j, b, i: (0, j))],
        out_specs=pl.BlockSpec((1, tm, tn), lambda j, b, i: (b, i, j)),
        out_shape=jax.ShapeDtypeStruct((bsz, s, n), BF16),
        compiler_params=_params(("arbitrary", "arbitrary", "arbitrary")),
        name="qkv",
    )(x, mod3, w_in_bf)


def _bucket_map():
    qi = np.arange(WINDOW)[:, None]
    kj = np.arange(2 * WINDOW)[None, :]
    dist = qi + WINDOW - kj
    n = np.maximum(dist, 0)
    max_exact = NUM_BUCKETS // 2
    ratio = np.maximum(n, max_exact).astype(np.float32) / np.float32(max_exact)
    large = max_exact + (np.log(ratio) / np.float32(math.log(MAX_DISTANCE / max_exact))
                         * np.float32(NUM_BUCKETS - max_exact)).astype(np.int32)
    large = np.minimum(large, NUM_BUCKETS - 1)
    bucket = np.where(n < max_exact, n, large)
    band = (dist >= 0) & (dist < WINDOW)
    return np.where(band, bucket, -1).astype(np.int32)


def _swa_bias_kernel(rb_ref, bucket_ref, o_ref):
    first = pl.program_id(0) == 0
    bucket = bucket_ref[...]
    col = lax.broadcasted_iota(I32, bucket.shape, 1)
    hidden = jnp.logical_and(first, col < WINDOW)
    for h in range(o_ref.shape[1]):
        acc = jnp.full(bucket.shape, NEG_INF, F32)
        for b in range(NUM_BUCKETS):
            acc = jnp.where(bucket == b, rb_ref[b, h] * LOG2E, acc)
        o_ref[0, h] = jnp.where(hidden, NEG_INF, acc)


def _swa_bias(rel_bias):
    nh = rel_bias.shape[1]
    bucket = jnp.asarray(_bucket_map())
    return pl.pallas_call(
        _swa_bias_kernel,
        grid=(2,),
        in_specs=[pl.BlockSpec(memory_space=pltpu.SMEM),
                  pl.BlockSpec((WINDOW, 2 * WINDOW), lambda v: (0, 0))],
        out_specs=pl.BlockSpec((1, nh, WINDOW, 2 * WINDOW), lambda v: (v, 0, 0, 0)),
        out_shape=jax.ShapeDtypeStruct((2, nh, WINDOW, 2 * WINDOW), F32),
        compiler_params=_params(("arbitrary",)),
        name="swa_bias",
    )(rel_bias, bucket)


def _swa_kernel(sink_ref, q_ref, kvc_ref, kvp_ref, bias_ref, o_ref, *, n_heads):
    group = n_heads // KV_A
    kv = jnp.concatenate([kvp_ref[0], kvc_ref[0]], axis=0)
    lane = lax.broadcasted_iota(I32, (2 * WINDOW, LANES), 1)
    low = lane < HEAD_DIM

    def halves(pair):
        zero = jnp.zeros_like(pair)
        return [jnp.where(low, pair, zero), jnp.where(low, zero, pair)]

    kz = halves(kv[:, 0:LANES])
    vz = halves(kv[:, LANES:2 * LANES])

    n_pairs = n_heads // KV_A
    for p0 in range(0, n_pairs, SWA_PAIRS):
        pairs = range(p0, min(p0 + SWA_PAIRS, n_pairs))
        heads = [(p, g) for p in pairs for g in range(KV_A)]
        logits, e, den, o = {}, {}, {}, {}
        for p, g in heads:
            qp = q_ref[0, :, p * LANES:(p + 1) * LANES]
            s = lax.dot_general(qp, kz[g], (((1,), (1,)), ((), ())), preferred_element_type=F32)
            logits[p, g] = s + bias_ref[0, g * group + p]
        for p, g in heads:
            sink = sink_ref[g * group + p]
            m = jnp.maximum(jnp.max(logits[p, g], axis=-1, keepdims=True), sink)
            e[p, g] = jnp.exp2(logits[p, g] - m)
            den[p, g] = jnp.sum(e[p, g], axis=-1, keepdims=True) + jnp.exp2(sink - m)
        for p, g in heads:
            o[p, g] = jnp.dot(e[p, g].astype(BF16), vz[g], preferred_element_type=F32)
        for p in pairs:
            acc = o[p, 0] * (1.0 / den[p, 0])
            for g in range(1, KV_A):
                acc = acc + o[p, g] * (1.0 / den[p, g])
            o_ref[0, :, p * LANES:(p + 1) * LANES] = acc.astype(BF16)


def _swa(qkv, sinks, bias, d_a, d_b):
    bsz, s, _ = qkv.shape
    n_heads = d_a // HEAD_DIM
    q_blk = 3 * d_b // d_a
    kv_blk = (3 * d_b + d_a) // (2 * LANES)
    return pl.pallas_call(
        functools.partial(_swa_kernel, n_heads=n_heads),
        grid=(bsz, s // WINDOW),
        in_specs=[pl.BlockSpec(memory_space=pltpu.SMEM),
                  pl.BlockSpec((1, WINDOW, d_a), lambda b, i: (b, i, q_blk)),
                  pl.BlockSpec((1, WINDOW, 2 * LANES), lambda b, i: (b, i, kv_blk)),
                  pl.BlockSpec((1, WINDOW, 2 * LANES),
                               lambda b, i: (b, jnp.maximum(i - 1, 0), kv_blk)),
                  pl.BlockSpec((1, n_heads, WINDOW, 2 * WINDOW),
                               lambda b, i: (jnp.minimum(i, 1), 0, 0, 0))],
        out_specs=pl.BlockSpec((1, WINDOW, d_a), lambda b, i: (b, i, 0)),
        out_shape=jax.ShapeDtypeStruct((bsz, s, d_a), BF16),
        compiler_params=_params(("arbitrary", "arbitrary")),
        name="swa",
    )(sinks, qkv, qkv, qkv, bias)


def _suffix_matrix():
    j = np.arange(Q_BLOCK)[:, None]
    s = np.arange(Q_BLOCK)[None, :]
    return np.concatenate([(j > s), np.ones((Q_BLOCK, Q_BLOCK), bool)], axis=1).astype(np.float32)


def _sb_kernel(q_ref, k_ref, v_ref, lt_ref, o_ref, acc_ref, carry_ref, *, group):
    i = pl.program_id(2)
    lane = lax.broadcasted_iota(I32, (Q_BLOCK, LANES), 1)
    low = lane < HEAD_DIM
    row = lax.broadcasted_iota(I32, (2 * Q_BLOCK, Q_BLOCK), 0)
    col = lax.broadcasted_iota(I32, (2 * Q_BLOCK, Q_BLOCK), 1)
    strict = col < jnp.where(row >= Q_BLOCK, row - Q_BLOCK, row)
    sign = jnp.uint32(0x80000000)

    qh = []
    for g in range(group):
        q = q_ref[0, :, g * LANES:(g + 1) * LANES]
        zero = jnp.zeros_like(q)
        qh.append([jnp.where(low, q, zero), jnp.where(low, zero, q)])

    gs = range(group)

    def scores(j, g, lo, hi):
        return lax.dot_general(jnp.concatenate([qh[g][0][lo:hi], qh[g][1][lo:hi]], axis=0),
                               k_ref[0, pl.ds(pl.multiple_of(j * Q_BLOCK, Q_BLOCK), Q_BLOCK),
                                     g * LANES:(g + 1) * LANES],
                               (((1,), (1,)), ((), ())), preferred_element_type=F32)

    def softplus2(z):
        neg_abs = pltpu.bitcast(pltpu.bitcast(z, jnp.uint32) | sign, F32)
        return jnp.maximum(z, 0.0) + jnp.log2(1.0 + jnp.exp2(neg_abs))

    def suffix(sp):
        return jnp.dot(sp.astype(BF16), lt_ref[...], preferred_element_type=F32)

    def weighted_values(a, j, g, rows):
        a = a.astype(BF16)
        a2 = jnp.concatenate([a[:rows], a[rows:]], axis=1)
        vj = v_ref[0, pl.ds(pl.multiple_of(j * Q_BLOCK, Q_BLOCK), Q_BLOCK), g * LANES:(g + 1) * LANES]
        vzero = jnp.zeros_like(vj)
        vz = jnp.concatenate([jnp.where(low, vj, vzero), jnp.where(low, vzero, vj)], axis=0)
        return jnp.dot(a2, vz, preferred_element_type=F32)

    t = SB_TAIL_ROWS

    def carry_mins(carry_min, lo, hi):
        n = hi - lo
        n_top = max(min(hi, t) - lo, 0)
        top = jnp.min(jnp.minimum(carry_min[:n_top], carry_min[n:n + n_top])) if n_top else None
        rest = jnp.min(jnp.minimum(carry_min[n_top:n], carry_min[n + n_top:])) if n_top < n else None
        return top, rest

    def first_blocks(n_before):
        pieces = [(i, 0, Q_BLOCK)]
        if n_before >= 1:
            pieces.append((i - 1, 0, Q_BLOCK))
        if n_before >= 2:
            pieces.append((i - 2, 0, t))
        z = {(b, g): scores(j, g, lo, hi) for b, (j, lo, hi) in enumerate(pieces) for g in gs}
        for g in gs:
            z[0, g] = jnp.where(strict, z[0, g], NEG_INF)
        sp = {bg: softplus2(zz) for bg, zz in z.items()}
        cs = {bg: suffix(s) for bg, s in sp.items()}
        a, carry_min, head_min = {}, None, None
        for g in gs:
            a[0, g] = jnp.exp2(z[0, g] - sp[0, g] - cs[0, g][:, :Q_BLOCK])
            carry = cs[0, g][:, Q_BLOCK:]
            if n_before >= 1:
                a[1, g] = jnp.exp2(z[1, g] - sp[1, g] - cs[1, g][:, :Q_BLOCK] - carry)
                carry = carry + cs[1, g][:, Q_BLOCK:]
            carry_ref[g, 0] = carry[:Q_BLOCK]
            carry_ref[g, 1] = carry[Q_BLOCK:]
            carry_min = carry if carry_min is None else jnp.minimum(carry_min, carry)
            if n_before >= 2:
                head = jnp.concatenate([carry[:t], carry[Q_BLOCK:Q_BLOCK + t]], axis=0)
                a[2, g] = jnp.exp2(z[2, g] - sp[2, g] - cs[2, g][:, :Q_BLOCK] - head)
                head = head + cs[2, g][:, Q_BLOCK:]
                carry_ref[g, 0, :t] = head[:t]
                carry_ref[g, 1, :t] = head[t:]
                head_min = head if head_min is None else jnp.minimum(head_min, head)
        for g in gs:
            acc = weighted_values(a[0, g], i, g, Q_BLOCK)
            if n_before >= 1:
                acc = acc + weighted_values(a[1, g], i - 1, g, Q_BLOCK)
            acc_ref[g] = acc
            if n_before >= 2:
                acc_ref[g, :t] += weighted_values(a[2, g], i - 2, g, t)
        top, rest = carry_mins(carry_min, 0, Q_BLOCK)
        if n_before >= 2:
            top = carry_mins(head_min, 0, t)[0]
        return top, rest

    def block(j, lo, hi):
        n = hi - lo
        z = [scores(j, g, lo, hi) for g in gs]
        sp = [softplus2(zz) for zz in z]
        cs = [suffix(s) for s in sp]
        a, carry_min = [], None
        for g in gs:
            carry = jnp.concatenate([carry_ref[g, 0, lo:hi], carry_ref[g, 1, lo:hi]], axis=0)
            a.append(jnp.exp2(z[g] - sp[g] - cs[g][:, :Q_BLOCK] - carry))
            carry = carry + cs[g][:, Q_BLOCK:]
            carry_ref[g, 0, lo:hi] = carry[:n]
            carry_ref[g, 1, lo:hi] = carry[n:]
            carry_min = carry if carry_min is None else jnp.minimum(carry_min, carry)
        for g in gs:
            acc_ref[g, lo:hi] += weighted_values(a[g], j, g, n)
        return carry_mins(carry_min, lo, hi)

    def two_before():
        top, rest = first_blocks(2)
        rest = lax.cond(rest < SB_SKIP_BITS, lambda: block(i - 2, t, Q_BLOCK)[1], lambda: rest)
        return top, rest

    top0, rest0 = lax.cond(
        i >= 2, two_before,
        lambda: lax.cond(i == 1, lambda: first_blocks(1), lambda: first_blocks(0)))

    def more(state):
        jj, top, rest = state
        return jnp.logical_and(jj < i - 2, jnp.minimum(top, rest) < SB_SKIP_BITS)

    def body(state):
        jj, _, rest = state
        j = i - 3 - jj

        def tail_rows():
            return block(j, 0, t)[0], rest

        def all_rows():
            return block(j, 0, Q_BLOCK)

        top, rest = lax.cond(rest >= SB_SKIP_BITS, tail_rows, all_rows)
        return jj + 1, top, rest

    lax.while_loop(more, body, (jnp.int32(0), top0, rest0))
    for g in range(group):
        o_ref[0, :, g * LANES:(g + 1) * LANES] = acc_ref[g].astype(BF16)


def _sb(qkv, d_b, group=SB_GROUP):
    bsz, s, _ = qkv.shape
    pairs = d_b // LANES
    ng = pairs // group
    w = group * LANES
    lt = jnp.asarray(_suffix_matrix(), BF16)
    return pl.pallas_call(
        functools.partial(_sb_kernel, group=group),
        grid=(bsz, ng, s // Q_BLOCK),
        in_specs=[pl.BlockSpec((1, Q_BLOCK, w), lambda b, p, i: (b, i, p)),
                  pl.BlockSpec((1, s, w), lambda b, p, i: (b, 0, ng + p)),
                  pl.BlockSpec((1, s, w), lambda b, p, i: (b, 0, 2 * ng + p)),
                  pl.BlockSpec((Q_BLOCK, 2 * Q_BLOCK), lambda b, p, i: (0, 0))],
        out_specs=pl.BlockSpec((1, Q_BLOCK, w), lambda b, p, i: (b, i, p)),
        out_shape=jax.ShapeDtypeStruct((bsz, s, d_b), BF16),
        scratch_shapes=[pltpu.VMEM((group, Q_BLOCK, LANES), F32),
                        pltpu.VMEM((group, 2, Q_BLOCK, LANES), F32)],
        compiler_params=_params(("arbitrary", "arbitrary", "arbitrary")),
        name="sb",
    )(qkv, qkv, qkv, lt)


def _layer_norm(y, g, b):
    mu = jnp.mean(y, axis=-1, keepdims=True)
    yc = y - mu
    var = jnp.mean(yc * yc, axis=-1, keepdims=True)
    return yc * lax.rsqrt(var + EPS) * g + b


def _rms(o, g):
    return o * lax.rsqrt(jnp.mean(o * o, axis=-1, keepdims=True) + EPS) * g


def _mix_ln1_kernel(oa_ref, ob_ref, x_ref, mod_ref, na_ref, nb_ref, wo_ref, g_ref, b_ref,
                    wrc_ref, br_ref, x1_ref, h2_ref, lg_ref, *, d_a, parts):
    hm = x_ref.shape[1] // parts
    rows = [pl.ds(p * hm, hm) for p in range(parts)]
    gate1 = mod_ref[0, 2:3, :]
    shift2 = mod_ref[0, 3:4, :]
    scale2 = mod_ref[0, 4:5, :]
    ra = [_rms(oa_ref[0, r, :].astype(F32), na_ref[...]).astype(BF16) for r in rows]
    rb = [_rms(ob_ref[0, r, :].astype(F32), nb_ref[...]).astype(BF16) for r in rows]
    mix = [jnp.dot(ra[p], wo_ref[:d_a, :], preferred_element_type=F32)
           + jnp.dot(rb[p], wo_ref[d_a:, :], preferred_element_type=F32) for p in range(parts)]
    hi, lo = [], []
    for p, r in enumerate(rows):
        x1 = _layer_norm(ALPHA * x_ref[0, r, :] + (1.0 + gate1) * mix[p], g_ref[...], b_ref[...])
        x1_ref[0, r, :] = x1
        h2 = x1 * (1.0 + scale2) + shift2
        _store_row_tiles(h2_ref.at[0, pl.ds(p * hm * ROW_SUB, hm * ROW_SUB)], h2)
        hi.append(h2.astype(BF16))
        lo.append((h2 - hi[p].astype(F32)).astype(BF16))
    for p, r in enumerate(rows):
        both = jnp.dot(hi[p], wrc_ref[...], preferred_element_type=F32)
        lg_ref[0, r, :] = (both[:, :LANES] + both[:, LANES:]
                           + jnp.dot(lo[p], wrc_ref[:, :LANES], preferred_element_type=F32)
                           + br_ref[...])


def _mix_ln1(o_a, o_b, x, mod3, norm_a, norm_b, w_out_bf, ln_g, ln_b, wr_cat, b_r, tm=512, parts=2):
    bsz, s, d = x.shape
    d_a = o_a.shape[-1]
    d_b = o_b.shape[-1]
    tm = min(tm, s)
    row = lambda b, i: (b, i, 0)
    const2 = lambda b, i: (0, 0)
    once = pl.Buffered(1)
    return pl.pallas_call(
        functools.partial(_mix_ln1_kernel, d_a=d_a, parts=parts),
        grid=(bsz, s // tm),
        in_specs=[pl.BlockSpec((1, tm, d_a), row),
                  pl.BlockSpec((1, tm, d_b), row),
                  pl.BlockSpec((1, tm, d), row),
                  pl.BlockSpec((1, 6, d), lambda b, i: (b, 0, 0)),
                  pl.BlockSpec((1, d_a), const2),
                  pl.BlockSpec((1, d_b), const2),
                  pl.BlockSpec((d_a + d_b, d), const2, pipeline_mode=once),
                  pl.BlockSpec((1, d), const2),
                  pl.BlockSpec((1, d), const2),
                  pl.BlockSpec((d, 2 * LANES), const2, pipeline_mode=once),
                  pl.BlockSpec((1, LANES), const2)],
        out_specs=[pl.BlockSpec((1, tm, d), row),
                   pl.BlockSpec((1, tm * ROW_SUB, LANES), row),
                   pl.BlockSpec((1, tm, LANES), row)],
        out_shape=[jax.ShapeDtypeStruct((bsz, s, d), F32),
                   jax.ShapeDtypeStruct((bsz, s * ROW_SUB, LANES), jnp.uint32),
                   jax.ShapeDtypeStruct((bsz, s, LANES), F32)],
        compiler_params=_params(("arbitrary", "arbitrary")),
        name="mix_ln1",
    )(o_a, o_b, x, mod3, norm_a, norm_b, w_out_bf, ln_g, ln_b, wr_cat, b_r)


def _route_kernel(lg_ref, tri_ref, sel_ref, gate_ref, cnt_ref, base_ref):
    step = pl.program_id(0)

    @pl.when(step == 0)
    def _():
        base_ref[...] = jnp.zeros_like(base_ref)

    lg = lg_ref[...]
    tm = lg.shape[0]
    lane = lax.broadcasted_iota(I32, (tm, LANES), 1)
    big = jnp.int32(2 * LANES)
    glog = jnp.where(lane < N_GROUPS, lg, -jnp.inf)
    gmax = jnp.max(glog, axis=-1, keepdims=True)
    g_sel = jnp.min(jnp.where(glog == gmax, lane, big), axis=-1, keepdims=True)
    p_g = 1.0 / jnp.sum(jnp.exp(glog - gmax), axis=-1, keepdims=True)
    lo = N_GROUPS + g_sel * EXPERTS_PER_GROUP
    in_grp = jnp.logical_and(lane >= lo, lane < lo + EXPERTS_PER_GROUP)
    el = jnp.where(in_grp, lg, -jnp.inf)
    v1 = jnp.max(el, axis=-1, keepdims=True)
    i1 = jnp.min(jnp.where(el == v1, lane, big), axis=-1, keepdims=True)
    el2 = jnp.where(lane == i1, -jnp.inf, el)
    v2 = jnp.max(el2, axis=-1, keepdims=True)
    i2 = jnp.min(jnp.where(el2 == v2, lane, big), axis=-1, keepdims=True)
    r = jnp.exp(v2 - v1)
    w1 = 1.0 / (1.0 + r)
    g1 = p_g * w1
    g2 = p_g * (r * w1)
    e1 = i1 - N_GROUPS
    e2 = i2 - N_GROUPS
    oh1 = (lane == e1)
    oh2 = (lane == e2)
    occ = oh1.astype(F32) + oh2.astype(F32)
    before = jnp.dot(tri_ref[...], occ.astype(BF16), preferred_element_type=F32) + base_ref[...]
    r1 = jnp.sum(jnp.where(oh1, before, 0.0), axis=-1, keepdims=True)
    r2 = jnp.sum(jnp.where(oh2, before, 0.0), axis=-1, keepdims=True)
    base_ref[...] += jnp.sum(occ, axis=0, keepdims=True)
    cnt_ref[...] = base_ref[...]
    sel = jnp.where(lane == 0, e1, jnp.where(lane == 1, e2, 0))
    sel = jnp.where(lane == 2, r1.astype(I32), jnp.where(lane == 3, r2.astype(I32), sel))
    sel_ref[...] = sel
    gate_ref[...] = jnp.where(lane == 0, g1, jnp.where(lane == 1, g2, 0.0))


def _route(logits, tm=1024):
    t = logits.shape[0]
    tm = min(tm, t)
    tri = jnp.asarray(np.tril(np.ones((tm, tm), np.float32), -1), BF16)
    return pl.pallas_call(
        _route_kernel,
        grid=(t // tm,),
        in_specs=[pl.BlockSpec((tm, LANES), lambda i: (i, 0)),
                  pl.BlockSpec((tm, tm), lambda i: (0, 0))],
        out_specs=[pl.BlockSpec((tm, LANES), lambda i: (i, 0)),
                   pl.BlockSpec((tm, LANES), lambda i: (i, 0)),
                   pl.BlockSpec((1, LANES), lambda i: (0, 0))],
        out_shape=[jax.ShapeDtypeStruct((t, LANES), I32),
                   jax.ShapeDtypeStruct((t, LANES), F32),
                   jax.ShapeDtypeStruct((1, LANES), F32)],
        scratch_shapes=[pltpu.VMEM((1, LANES), F32)],
        compiler_params=_params(("arbitrary",)),
        name="route",
    )(logits, tri)


def _lane_prefix(x, lane):
    shift = 1
    while shift < LANES:
        x = x + jnp.where(lane >= shift, pltpu.roll(x, shift, axis=1), 0)
        shift *= 2
    return x


def _dest_kernel(sel_ref, cnt_ref, dest_ref, blk_ref, plan_ref, *, n_blk_pad):
    tm = sel_ref.shape[0]
    lane1 = lax.broadcasted_iota(I32, (8, LANES), 1)
    cnt = jnp.broadcast_to(cnt_ref[...].astype(I32), (8, LANES))
    cnt = jnp.where(lane1 < N_EXPERTS, cnt, 0)
    padded = jnp.bitwise_and(cnt + (MOE_TM - 1), -MOE_TM)
    pend = _lane_prefix(padded, lane1)
    pstart = (pend - padded)[0:1, :]
    sel = sel_ref[...]
    lane = lax.broadcasted_iota(I32, (tm, LANES), 1)
    e1 = sel[:, 0:1]
    e2 = sel[:, 1:2]
    d1 = jnp.sum(jnp.where(lane == e1, pstart, 0), axis=-1, keepdims=True) + sel[:, 2:3]
    d2 = jnp.sum(jnp.where(lane == e2, pstart, 0), axis=-1, keepdims=True) + sel[:, 3:4]
    dest_ref[...] = jnp.where(lane == 0, d1, jnp.where(lane == 1, d2, 0)) * ROW_SUB

    @pl.when(pl.program_id(0) == 0)
    def _():
        brow = lax.broadcasted_iota(I32, (n_blk_pad, LANES), 0) * MOE_TM
        blane = lax.broadcasted_iota(I32, (n_blk_pad, LANES), 1)
        ended = jnp.logical_and(blane < N_EXPERTS, pend[0:1, :] <= brow)
        be = jnp.minimum(jnp.sum(ended.astype(I32), axis=-1, keepdims=True), N_EXPERTS - 1)
        blk_ref[...] = jnp.broadcast_to(be, (n_blk_pad, LANES))
        sub = lax.broadcasted_iota(I32, (8, LANES), 0)
        used = jnp.max(pend, axis=-1, keepdims=True) >> (MOE_TM.bit_length() - 1)
        plan_ref[...] = jnp.where(sub == 0, (pend - padded + cnt) * ROW_SUB,
                                  jnp.where(sub == 1, padded - cnt, used))


def _dest(sel, counts, n_blk, tm=1024):
    t = sel.shape[0]
    tm = min(tm, t)
    n_blk_pad = -(-n_blk // 8) * 8
    return pl.pallas_call(
        functools.partial(_dest_kernel, n_blk_pad=n_blk_pad),
        grid=(t // tm,),
        in_specs=[pl.BlockSpec((tm, LANES), lambda i: (i, 0)),
                  pl.BlockSpec((1, LANES), lambda i: (0, 0))],
        out_specs=[pl.BlockSpec((tm, LANES), lambda i: (i, 0)),
                   pl.BlockSpec((n_blk_pad, LANES), lambda i: (0, 0)),
                   pl.BlockSpec((8, LANES), lambda i: (0, 0))],
        out_shape=[jax.ShapeDtypeStruct((t, LANES), I32),
                   jax.ShapeDtypeStruct((n_blk_pad, LANES), I32),
                   jax.ShapeDtypeStruct((8, LANES), I32)],
        compiler_params=_params(("arbitrary",)),
        name="dest",
    )(sel, counts)


def _dispatch_kernel(dest_ref, plan_ref, h_ref, xs_ref, stage, zeros, sems, zsem, *, n_blk):
    tm = h_ref.shape[0] // ROW_SUB
    step = pl.program_id(0)
    last = pl.num_programs(0) - 1
    slot = lax.rem(step, 2)
    base = step * (2 * tm)
    blk_sub = MOE_TM * ROW_SUB

    def fill(wait):
        def run(copy):
            copy.wait() if wait else copy.start()

        def pads(e, carry):
            first = plan_ref[e]
            n = plan_ref[N_EXPERTS + e]
            bit = MOE_TM // 2
            while bit >= 1:
                @pl.when(jnp.bitwise_and(n, bit) != 0)
                def _(bit=bit):
                    done = jnp.bitwise_and(n, -2 * bit)
                    dst = pl.multiple_of(first + done * ROW_SUB, ROW_SUB)
                    run(pltpu.make_async_copy(zeros.at[pl.ds(0, bit * ROW_SUB)],
                                              xs_ref.at[pl.ds(dst, bit * ROW_SUB)], zsem))
                bit //= 2
            return carry

        lax.fori_loop(0, N_EXPERTS, pads, 0)

        def unused(b, carry):
            dst = pl.multiple_of(b * blk_sub, blk_sub)
            run(pltpu.make_async_copy(zeros, xs_ref.at[pl.ds(dst, blk_sub)], zsem))
            return carry

        lax.fori_loop(plan_ref[2 * N_EXPERTS], n_blk, unused, 0)

    @pl.when(step == 0)
    def _():
        zeros[...] = jnp.zeros_like(zeros)
        fill(wait=False)
        fill(wait=True)

    def wait_tile(sl):
        for _ in range(2):
            pltpu.make_async_copy(stage.at[sl], xs_ref.at[pl.ds(0, tm * ROW_SUB)], sems.at[sl]).wait()

    @pl.when(step >= 2)
    def _():
        wait_tile(slot)

    stage[slot] = h_ref[...]

    def start(c, carry):
        for u in range(ROW_UNROLL):
            r = c * ROW_UNROLL + u
            src = stage.at[slot, pl.ds(pl.multiple_of(r * ROW_SUB, ROW_SUB), ROW_SUB)]
            for k in range(2):
                dst = pl.multiple_of(dest_ref[base + 2 * r + k], ROW_SUB)
                pltpu.make_async_copy(src, xs_ref.at[pl.ds(dst, ROW_SUB)], sems.at[slot]).start(priority=k)
        return carry

    lax.fori_loop(0, tm // ROW_UNROLL, start, 0)

    @pl.when(step == last)
    def _():
        wait_tile(slot)

        @pl.when(step >= 1)
        def _():
            wait_tile(1 - slot)


def _dispatch(dest_flat, plan_flat, h2p, m_pad, tm=256):
    t = h2p.shape[0] // ROW_SUB
    tm = min(tm, t)
    return pl.pallas_call(
        functools.partial(_dispatch_kernel, n_blk=m_pad // MOE_TM),
        grid_spec=pltpu.PrefetchScalarGridSpec(
            num_scalar_prefetch=2,
            grid=(t // tm,),
            in_specs=[pl.BlockSpec((tm * ROW_SUB, LANES), lambda i, dest, plan: (i, 0))],
            out_specs=pl.BlockSpec(memory_space=pl.ANY),
            scratch_shapes=[pltpu.VMEM((2, tm * ROW_SUB, LANES), h2p.dtype),
                            pltpu.VMEM((MOE_TM * ROW_SUB, LANES), h2p.dtype),
                            pltpu.SemaphoreType.DMA((2,)),
                            pltpu.SemaphoreType.DMA(())]),
        out_shape=jax.ShapeDtypeStruct((m_pad * ROW_SUB, LANES), h2p.dtype),
        compiler_params=_params(("arbitrary",)),
        name="dispatch",
    )(dest_flat, plan_flat, h2p)


def _experts_kernel(blk_ref, xs_ref, wg_hbm, wu_hbm, wd_hbm, ys_ref,
                    wg_f32, wu_f32, wd_f32, wg_bf, wu_bf, wd_bf, slot_ref, sems):
    i = pl.program_id(0)
    n_blk = pl.num_programs(0)
    n_used = blk_ref[n_blk]
    e = blk_ref[i]
    in_use = i < n_used
    first_of_run = jnp.logical_and(
        in_use, jnp.logical_or(i == 0, blk_ref[jnp.maximum(i - 1, 0)] != e))

    def weight_copies(expert, sl):
        return [pltpu.make_async_copy(wg_hbm.at[0, expert], wg_f32.at[sl], sems.at[sl]),
                pltpu.make_async_copy(wu_hbm.at[0, expert], wu_f32.at[sl], sems.at[sl]),
                pltpu.make_async_copy(wd_hbm.at[0, expert], wd_f32.at[sl], sems.at[sl])]

    @pl.when(i == 0)
    def _():
        slot_ref[0] = 0
        for c in weight_copies(e, 0):
            c.start()

    @pl.when(first_of_run)
    def _():
        sl = slot_ref[0]
        for c in weight_copies(e, sl):
            c.wait()
        wg_bf[...] = wg_f32[sl].astype(BF16)
        wu_bf[...] = wu_f32[sl].astype(BF16)
        wd_bf[...] = wd_f32[sl].astype(BF16)
        nxt = lax.while_loop(
            lambda j: jnp.logical_and(j < n_used, blk_ref[jnp.minimum(j, n_blk - 1)] == e),
            lambda j: j + 1, i + 1)

        @pl.when(nxt < n_used)
        def _():
            for c in weight_copies(blk_ref[nxt], 1 - sl):
                c.start(priority=1)

        slot_ref[0] = 1 - sl

    @pl.when(in_use)
    def _():
        xb = jnp.concatenate([c.astype(BF16) for c in _load_row_tiles(xs_ref, MOE_TM)], axis=1)
        g = jnp.dot(xb, wg_bf[...], preferred_element_type=F32)
        u = jnp.dot(xb, wu_bf[...], preferred_element_type=F32)
        hmid = (g * jax.nn.sigmoid(g) * u).astype(BF16)
        _store_row_tiles(ys_ref, jnp.dot(hmid, wd_bf[...], preferred_element_type=F32))

    @pl.when(jnp.logical_not(in_use))
    def _():
        ys_ref[...] = jnp.zeros_like(ys_ref)


def _experts(blk_expert, xs, w_gate, w_up, w_down):
    d, de = w_gate.shape[-2:]
    n_blk = xs.shape[0] // (MOE_TM * ROW_SUB)
    return pl.pallas_call(
        _experts_kernel,
        grid_spec=pltpu.PrefetchScalarGridSpec(
            num_scalar_prefetch=1,
            grid=(n_blk,),
            in_specs=[pl.BlockSpec((MOE_TM * ROW_SUB, LANES), lambda i, blk: (i, 0)),
                      pl.BlockSpec(memory_space=pl.ANY),
                      pl.BlockSpec(memory_space=pl.ANY),
                      pl.BlockSpec(memory_space=pl.ANY)],
            out_specs=pl.BlockSpec((MOE_TM * ROW_SUB, LANES), lambda i, blk: (i, 0)),
            scratch_shapes=[pltpu.VMEM((2, d, de), F32),
                            pltpu.VMEM((2, d, de), F32),
                            pltpu.VMEM((2, de, d), F32),
                            pltpu.VMEM((d, de), BF16),
                            pltpu.VMEM((d, de), BF16),
                            pltpu.VMEM((de, d), BF16),
                            pltpu.SMEM((1,), I32),
                            pltpu.SemaphoreType.DMA((2,))]),
        out_shape=jax.ShapeDtypeStruct(xs.shape, jnp.uint32),
        compiler_params=_params(("arbitrary",)),
        name="experts",
    )(blk_expert, xs, w_gate, w_up, w_down)


def _combine_kernel(dest_ref, ys_ref, gate_ref, x1_ref, mod_ref, g_ref, b_ref, o_ref, *scratch):
    ybufs, sems = scratch[:COMBINE_SLOTS], scratch[COMBINE_SLOTS]
    tm = x1_ref.shape[1]
    n_steps = pl.num_programs(0) * pl.num_programs(1)
    step = pl.program_id(0) * pl.num_programs(1) + pl.program_id(1)

    def row_copy(base, r, k, sl):
        src = pl.multiple_of(dest_ref[base + 2 * r + k], ROW_SUB)
        return pltpu.make_async_copy(ys_ref.at[pl.ds(src, ROW_SUB)],
                                     ybufs[sl].at[k, pl.ds(pl.multiple_of(r * ROW_SUB, ROW_SUB), ROW_SUB)],
                                     sems.at[sl])

    def wait_slot(sl):
        for k in range(2):
            pltpu.make_async_copy(ys_ref.at[pl.ds(0, tm * ROW_SUB)], ybufs[sl].at[k], sems.at[sl]).wait()

    @pl.when(step == 0)
    def _():
        for ahead in range(COMBINE_SLOTS - 1):
            base = jnp.minimum(ahead, n_steps - 1) * (2 * tm)

            def start(c, carry, base=base, ahead=ahead):
                for u in range(ROW_UNROLL):
                    for k in range(2):
                        row_copy(base, c * ROW_UNROLL + u, k, ahead).start(priority=k)
                return carry

            lax.fori_loop(0, tm // ROW_UNROLL, start, 0)

    def run(sl):
        wait_slot(sl)
        ahead_base = jnp.minimum(step + COMBINE_SLOTS - 1, n_steps - 1) * (2 * tm)
        ahead_slot = (sl + COMBINE_SLOTS - 1) % COMBINE_SLOTS
        for r in range(tm):
            for k in range(2):
                row_copy(ahead_base, r, k, ahead_slot).start(priority=k)

        gates = gate_ref[0]
        g0 = gates[:, 0:1]
        g1 = gates[:, 1:2]
        y0 = _load_row_tiles(ybufs[sl].at[0], tm)
        y1 = _load_row_tiles(ybufs[sl].at[1], tm)
        ffn = jnp.concatenate([g0 * a + g1 * b for a, b in zip(y0, y1)], axis=1)
        gate2 = mod_ref[0, 5:6, :]
        o_ref[0] = _layer_norm(ALPHA * x1_ref[0] + (1.0 + gate2) * ffn, g_ref[...], b_ref[...])

        @pl.when(step == n_steps - 1)
        def _():
            for ahead in range(1, COMBINE_SLOTS):
                wait_slot((sl + ahead) % COMBINE_SLOTS)

    for sl in range(COMBINE_SLOTS):
        pl.when(lax.rem(step, COMBINE_SLOTS) == sl)(functools.partial(run, sl))


def _combine(dest_flat, ys, gates3, x1, mod3, ln_g, ln_b, tm=256):
    bsz, s, d = x1.shape
    tm = min(tm, s)
    return pl.pallas_call(
        _combine_kernel,
        grid_spec=pltpu.PrefetchScalarGridSpec(
            num_scalar_prefetch=1,
            grid=(bsz, s // tm),
            in_specs=[pl.BlockSpec(memory_space=pl.ANY),
                      pl.BlockSpec((1, tm, LANES), lambda b, i, dest: (b, i, 0)),
                      pl.BlockSpec((1, tm, d), lambda b, i, dest: (b, i, 0)),
                      pl.BlockSpec((1, 6, d), lambda b, i, dest: (b, 0, 0)),
                      pl.BlockSpec((1, d), lambda b, i, dest: (0, 0)),
                      pl.BlockSpec((1, d), lambda b, i, dest: (0, 0))],
            out_specs=pl.BlockSpec((1, tm, d), lambda b, i, dest: (b, i, 0)),
            scratch_shapes=[pltpu.VMEM((2, tm * ROW_SUB, LANES), jnp.uint32)] * COMBINE_SLOTS
            + [pltpu.SemaphoreType.DMA((COMBINE_SLOTS,))]),
        out_shape=jax.ShapeDtypeStruct((bsz, s, d), F32),
        compiler_params=_params(("arbitrary", "arbitrary")),
        name="combine",
    )(dest_flat, ys, gates3, x1, mod3, ln_g, ln_b)


def kernel(x, c, w_in, w_out, sinks, rel_bias, norm_a, norm_b, w_ada, b_ada, ln1_g, ln1_b,
           ln2_g, ln2_b, w_grp, b_grp, w_rtr, b_rtr, w_gate, w_up, w_down):
    bsz, s, d = x.shape
    t = bsz * s
    d_a = norm_a.shape[-1]
    d_b = norm_b.shape[-1]

    mod3 = _adaln(c, w_ada, b_ada).reshape(bsz, 6, d)

    assert KV_A * HEAD_DIM == LANES
    kv_w = 2 * KV_A * HEAD_DIM
    group_a = d_a // HEAD_DIM // KV_A

    def pair_heads(a, axis):
        shape = a.shape[:axis] + (KV_A, group_a, HEAD_DIM) + a.shape[axis + 1:]
        return jnp.swapaxes(a.reshape(shape), axis, axis + 1).reshape(a.shape)

    w0 = w_in[0]
    w_in_bf = jnp.concatenate(
        [w0[:, d_a + kv_w:d_a + kv_w + d_b] * (ATTN_SCALE * LOG2E),
         w0[:, d_a + kv_w + d_b:],
         pair_heads(w0[:, :d_a], 1) * (ATTN_SCALE * LOG2E),
         w0[:, d_a:d_a + kv_w]], axis=1).astype(BF16)
    qkv = _qkv(x, mod3, w_in_bf)
    norm_a = pair_heads(norm_a, 1)
    w_out_bf = jnp.concatenate([pair_heads(w_out[0][:d_a], 0), w_out[0][d_a:]], axis=0).astype(BF16)

    o_a = _swa(qkv, sinks[0] * LOG2E, _swa_bias(rel_bias), d_a, d_b)
    o_b = _sb(qkv, d_b)

    w_r = jnp.concatenate([w_grp[0], w_rtr[0]], axis=1)
    w_r = jnp.pad(w_r, ((0, 0), (0, LANES - w_r.shape[1])))
    b_r = jnp.pad(jnp.concatenate([b_grp[0], b_rtr[0]]), (0, LANES - N_GROUPS - N_EXPERTS))[None, :]
    wr_hi = w_r.astype(BF16)
    wr_lo = (w_r - wr_hi.astype(F32)).astype(BF16)
    assert d == 2 * LANES * ROW_SUB, "row tiles hold 256 * ROW_SUB features"
    x1, h2, logits = _mix_ln1(o_a, o_b, x, mod3, norm_a, norm_b, w_out_bf,
                              ln1_g, ln1_b, jnp.concatenate([wr_hi, wr_lo], axis=1), b_r)

    sel, gates, counts = _route(logits.reshape(t, LANES))
    m_pad = 2 * t + N_EXPERTS * MOE_TM
    n_blk = m_pad // MOE_TM
    dest, blk, plan = _dest(sel, counts, n_blk)
    dest_flat = dest[:, :2].reshape(2 * t)
    blk_expert = jnp.concatenate([blk[:n_blk, 0], plan[2, :1]])
    plan_flat = jnp.concatenate([plan[0, :N_EXPERTS], plan[1, :N_EXPERTS], plan[2, :1]])

    xs = _dispatch(dest_flat, plan_flat, h2.reshape(t * ROW_SUB, LANES), m_pad)
    ys = _experts(blk_expert, xs, w_gate, w_up, w_down)
    return _combine(dest_flat, ys, gates.reshape(bsz, s, LANES), x1, mod3, ln2_g, ln2_b)
```
